```python
import math
import jax, jax.numpy as jnp
from jax import lax
import numpy as np

D_MODEL = 4096
BATCH = 2
SEQ = 8192
DEPTH = 2

EPS = 1e-6
GROUP = 128
W_MIX = D_MODEL // 4
N_BRANCH = 4
CHUNK = 128
CONV_A = 3
SGU_GROUPS = W_MIX // GROUP
SB_HEAD_DIM = 128
SB_HEADS = W_MIX // SB_HEAD_DIM
SSM_HEAD_DIM = 64
SSM_HEADS = W_MIX // SSM_HEAD_DIM
SSM_GROUPS = 2
SSM_HPG = SSM_HEADS // SSM_GROUPS
SSM_STATE = 128
SSM_CONV = 4
SSM_CONV_DIM = W_MIX + 2 * SSM_GROUPS * SSM_STATE
D_FF = -(-(8 * D_MODEL) // (3 * 256)) * 256
IN_SIZES = (W_MIX, W_MIX, W_MIX,
            W_MIX, W_MIX,
            W_MIX, W_MIX, W_MIX,
            W_MIX, SSM_CONV_DIM, SSM_HEADS,
            N_BRANCH * D_MODEL)
N_IN = sum(IN_SIZES)

kernel_name = "hybrid_gated_parallel_mixers"


def _split_points():
    return list(np.cumsum(np.array(IN_SIZES[:-1])))


def _rmsnorm(x, g):
    xf = x.astype(jnp.float32)
    y = xf * lax.rsqrt(jnp.mean(xf * xf, axis=-1, keepdims=True) + EPS)
    return (y * g.astype(jnp.float32)).astype(x.dtype)


def _causal_dwconv(x, w):
    K = w.shape[0]
    S = x.shape[1]
    xp = jnp.pad(x, ((0, 0), (K - 1, 0), (0, 0)))
    return sum(xp[:, k:k + S, :] * w[k] for k in range(K))


def _short_conv_mixer(b_g, c_g, xa, conv_w):
    return b_g * _causal_dwconv(c_g * xa, conv_w)


def _chunked_sgu(u, v, norm_g, w_s, b_s):
    Bsz, S, W = u.shape
    u = jax.nn.gelu(u)
    vf = jax.nn.gelu(v).astype(jnp.float32)
    mu = jnp.mean(vf, axis=-1, keepdims=True)
    var = jnp.mean(jnp.square(vf - mu), axis=-1, keepdims=True)
    vn = ((vf - mu) * lax.rsqrt(var + EPS) * norm_g.astype(jnp.float32)).astype(u.dtype)
    vn = vn.reshape(Bsz, S // CHUNK, CHUNK, SGU_GROUPS, GROUP)
    causal = jnp.tril(jnp.ones((CHUNK, CHUNK), dtype=bool))
    w = jnp.where(causal, w_s, jnp.zeros_like(w_s))
    mixed = jnp.einsum('gts,bnsgc->bntgc', w, vn) + b_s.T[None, None, :, :, None]
    return u * mixed.reshape(Bsz, S, W)


def _stick_breaking_attention(q, k, v, q_g, k_g):
    Bsz, S, W = q.shape
    f32 = jnp.float32
    q = _rmsnorm(q.reshape(Bsz, S, SB_HEADS, SB_HEAD_DIM), q_g).astype(f32)
    k = _rmsnorm(k.reshape(Bsz, S, SB_HEADS, SB_HEAD_DIM), k_g).astype(f32)
    v = v.reshape(Bsz, S, SB_HEADS, SB_HEAD_DIM).astype(f32)
    scale = 1.0 / math.sqrt(SB_HEAD_DIM)
    s_idx = jnp.arange(S)[None, :]

    def block(i):
        t0 = i * CHUNK
        qb = lax.dynamic_slice_in_dim(q, t0, CHUNK, axis=1)
        z = jnp.einsum('bthd,bshd->bhts', qb, k) * scale
        t_idx = t0 + jnp.arange(CHUNK)[:, None]
        mask = s_idx < t_idx
        log_beta = jax.nn.log_sigmoid(z)
        log_1m = jnp.where(mask, jax.nn.log_sigmoid(-z), 0.0)
        rev = lax.cumsum(log_1m, axis=3, reverse=True) - log_1m
        att = jnp.where(mask, jnp.exp(log_beta + rev), 0.0)
        return jnp.einsum('bhts,bshd->bthd', att, v)

    out = lax.map(block, jnp.arange(S // CHUNK))
    out = jnp.moveaxis(out, 0, 1).reshape(Bsz, S, W)
    return out


def _ssd_mixer(z, xbc, dt, conv_w, conv_b, dt_bias, a_log, d_skip, norm_g):
    Bsz, S, W = z.shape
    nc = S // CHUNK
    f32 = jnp.float32
    xbc = jax.nn.silu(_causal_dwconv(xbc, conv_w) + conv_b)
    xs, bm, cm = jnp.split(xbc, [W, W + SSM_GROUPS * SSM_STATE], axis=-1)
    xs = xs.astype(f32).reshape(Bsz, nc, CHUNK, SSM_GROUPS, SSM_HPG, SSM_HEAD_DIM)
    bm = bm.astype(f32).reshape(Bsz, nc, CHUNK, SSM_GROUPS, SSM_STATE)
    cm = cm.astype(f32).reshape(Bsz, nc, CHUNK, SSM_GROUPS, SSM_STATE)
    dt = jax.nn.softplus(dt.astype(f32) + dt_bias.astype(f32))
    dt = dt.reshape(Bsz, nc, CHUNK, SSM_GROUPS, SSM_HPG)
    a = -jnp.exp(a_log.astype(f32)).reshape(SSM_GROUPS, SSM_HPG)
    cum = jnp.cumsum(dt * a, axis=2)
    xdt = xs * dt[..., None]
    seg = cum[:, :, :, None] - cum[:, :, None, :]
    causal = jnp.tril(jnp.ones((CHUNK, CHUNK), dtype=bool))[:, :, None, None]
    decay = jnp.exp(jnp.where(causal, seg, -jnp.inf))
    cb = jnp.einsum('bctgn,bcsgn->bctsg', cm, bm)
    y_diag = jnp.einsum('bctsg,bctsgr,bcsgrp->bctgrp', cb, decay, xdt)
    dec_end = jnp.exp(cum[:, :, -1:] - cum)
    states = jnp.einsum('bcsgn,bcsgr,bcsgrp->bcgrpn', bm, dec_end, xdt)
    chunk_dec = jnp.exp(cum[:, :, -1])

    def step(h, inp):
        st, dc = inp
        return h * dc[..., None, None] + st, h

    h0 = jnp.zeros((Bsz, SSM_GROUPS, SSM_HPG, SSM_HEAD_DIM, SSM_STATE), f32)
    _, h_prev = lax.scan(step, h0, (jnp.moveaxis(states, 1, 0), jnp.moveaxis(chunk_dec, 1, 0)))
    h_prev = jnp.moveaxis(h_prev, 0, 1)
    y_off = jnp.einsum('bctgn,bctgr,bcgrpn->bctgrp', cm, jnp.exp(cum), h_prev)
    y = y_diag + y_off + xs * d_skip.astype(f32).reshape(SSM_GROUPS, SSM_HPG)[:, :, None]
    y = y.reshape(Bsz, S, W) * jax.nn.silu(z.astype(f32))
    y = y.reshape(Bsz, S, SSM_GROUPS, W // SSM_GROUPS)
    y = y * lax.rsqrt(jnp.mean(y * y, axis=-1, keepdims=True) + EPS)
    y = y * norm_g.astype(f32).reshape(SSM_GROUPS, W // SSM_GROUPS)
    return y.reshape(Bsz, S, W).astype(z.dtype)


def setup_inputs(seed: int = 0) -> dict:
    key = jax.random.key(seed)
    ks = jax.random.split(key, 24)
    f32 = jnp.float32

    def nrm(k, shape, scale):
        return jax.random.normal(k, shape, f32) * scale

    def gain(k, shape):
        return 1.0 + 0.02 * jax.random.normal(k, shape, f32)

    dt0 = jnp.exp(jax.random.uniform(ks[10], (DEPTH, SSM_HEADS), f32,
                                     minval=math.log(1e-3), maxval=math.log(1e-1)))
    return {
        "x": jax.random.normal(ks[0], (BATCH, SEQ, D_MODEL), f32),
        "norm_mix": gain(ks[1], (DEPTH, D_MODEL)),
        "w_in": nrm(ks[2], (DEPTH, D_MODEL, N_IN), D_MODEL ** -0.5),
        "conv_a": nrm(ks[3], (DEPTH, CONV_A, W_MIX), CONV_A ** -0.5),
        "sgu_norm": gain(ks[4], (DEPTH, W_MIX)),
        "sgu_w": nrm(ks[5], (DEPTH, SGU_GROUPS, CHUNK, CHUNK), CHUNK ** -0.5),
        "sgu_b": gain(ks[6], (DEPTH, SGU_GROUPS, CHUNK)),
        "q_norm": gain(ks[7], (DEPTH, SB_HEAD_DIM)),
        "k_norm": gain(ks[8], (DEPTH, SB_HEAD_DIM)),
        "ssm_conv_w": nrm(ks[9], (DEPTH, SSM_CONV, SSM_CONV_DIM), SSM_CONV ** -0.5),
        "ssm_conv_b": nrm(ks[11], (DEPTH, SSM_CONV_DIM), 0.02),
        "ssm_dt_bias": dt0 + jnp.log(-jnp.expm1(-dt0)),
        "ssm_a_log": jnp.log(jax.random.uniform(ks[12], (DEPTH, SSM_HEADS), f32, minval=1.0, maxval=16.0)),
        "ssm_d": gain(ks[13], (DEPTH, SSM_HEADS)),
        "ssm_norm": gain(ks[14], (DEPTH, W_MIX)),
        "w_branch": nrm(ks[15], (DEPTH, N_BRANCH, W_MIX, D_MODEL), W_MIX ** -0.5),
        "w_out": nrm(ks[16], (DEPTH, D_MODEL, D_MODEL), D_MODEL ** -0.5),
        "norm_ffn": gain(ks[17], (DEPTH, D_MODEL)),
        "w_ffn_gate": nrm(ks[18], (DEPTH, D_MODEL, D_FF), D_MODEL ** -0.5),
        "w_ffn_up": nrm(ks[19], (DEPTH, D_MODEL, D_FF), D_MODEL ** -0.5),
        "w_ffn_down": nrm(ks[20], (DEPTH, D_FF, D_MODEL), D_FF ** -0.5),
    }


def reference(x, norm_mix, w_in, conv_a, sgu_norm, sgu_w, sgu_b, q_norm, k_norm,
              ssm_conv_w, ssm_conv_b, ssm_dt_bias, ssm_a_log, ssm_d, ssm_norm,
              w_branch, w_out, norm_ffn, w_ffn_gate, w_ffn_up, w_ffn_down):
    Bsz, S, D = x.shape
    splits = _split_points()
    for i in range(DEPTH):
        h = _rmsnorm(x, norm_mix[i])
        proj = h @ w_in[i]
        (a_b, a_c, a_x, b_u, b_v, c_q, c_k, c_v,
         d_z, d_xbc, d_dt, gates) = jnp.split(proj, splits, axis=-1)
        y_a = _short_conv_mixer(a_b, a_c, a_x, conv_a[i])
        y_b = _chunked_sgu(b_u, b_v, sgu_norm[i], sgu_w[i], sgu_b[i])
        y_c = _stick_breaking_attention(c_q, c_k, c_v, q_norm[i], k_norm[i]).astype(x.dtype)
        y_d = _ssd_mixer(d_z, d_xbc, d_dt, ssm_conv_w[i], ssm_conv_b[i], ssm_dt_bias[i],
                         ssm_a_log[i], ssm_d[i], ssm_norm[i])
        ys = (y_a, y_b, y_c, y_d)
        gates = jax.nn.sigmoid(gates.astype(jnp.float32)).astype(x.dtype)
        gates = gates.reshape(Bsz, S, N_BRANCH, D)
        merged = sum(gates[:, :, j] * (ys[j] @ w_branch[i, j]) for j in range(N_BRANCH))
        x = x + merged @ w_out[i]
        h = _rmsnorm(x, norm_ffn[i])
        x = x + (jax.nn.silu(h @ w_ffn_gate[i]) * (h @ w_ffn_up[i])) @ w_ffn_down[i]
    return x
```

```python
import functools
import math

import jax
import jax.numpy as jnp
from jax import lax
from jax.experimental import pallas as pl
from jax.experimental.pallas import tpu as pltpu

F32 = jnp.float32
BF16 = jnp.bfloat16

EPS = 1e-6
D_MODEL = 4096
W_MIX = D_MODEL // 4
GROUP = 128
CHUNK = 128
CONV_A = 3
SGU_GROUPS = W_MIX // GROUP
SB_HEAD_DIM = 128
SB_HEADS = W_MIX // SB_HEAD_DIM
SSM_HEAD_DIM = 64
SSM_HEADS = W_MIX // SSM_HEAD_DIM
SSM_GROUPS = 2
SSM_HPG = SSM_HEADS // SSM_GROUPS
SSM_STATE = 128
SSM_CONV = 4
SSM_CONV_DIM = W_MIX + 2 * SSM_GROUPS * SSM_STATE
N_BRANCH = 4
N_MIX = 9 * W_MIX + SSM_CONV_DIM
LANES = 128
BF16_SUBLANES = 16
VMEM_LIMIT = 52 * 1024 * 1024

COL_A_B, COL_A_C, COL_A_X = 0, 1, 2
COL_B_U, COL_B_V = 3, 4
COL_C_Q, COL_C_K, COL_C_V = 5, 6, 7
COL_D_Z = 8
COL_D_XBC = (9 * W_MIX) // SSM_CONV_DIM


def _params(n_axes):
    return pltpu.CompilerParams(
        dimension_semantics=("arbitrary",) * n_axes, vmem_limit_bytes=VMEM_LIMIT)


def _sigmoid(x):
    return 1.0 / (1.0 + jnp.exp(-x))


def _softplus(x):
    return jnp.maximum(x, 0.0) + jnp.log1p(jnp.exp(-jnp.abs(x)))


def _gelu_tanh(x):
    c = math.sqrt(2.0 / math.pi)
    return x * (0.5 * (1.0 + jnp.tanh(c * (x + 0.044715 * (x * x * x)))))


def _rmsnorm_kernel(x_ref, g_ref, o_ref):
    x = x_ref[...]
    ms = jnp.mean(x * x, axis=-1, keepdims=True)
    o_ref[...] = (x * lax.rsqrt(ms + EPS) * g_ref[...]).astype(o_ref.dtype)


def _rmsnorm(x, g, tm=256):
    m, d = x.shape
    return pl.pallas_call(
        _rmsnorm_kernel,
        grid=(m // tm,),
        in_specs=[pl.BlockSpec((tm, d), lambda i: (i, 0)),
                  pl.BlockSpec((1, d), lambda i: (0, 0))],
        out_specs=pl.BlockSpec((tm, d), lambda i: (i, 0)),
        out_shape=jax.ShapeDtypeStruct((m, d), BF16),
        compiler_params=_params(1),
        name="rmsnorm",
    )(x, g.reshape(1, d))


def _mm_kernel(*refs, act, has_res):
    a_ref, b_ref = refs[0], refs[1]
    o_ref = refs[-1]
    acc = jnp.dot(a_ref[...], b_ref[...], preferred_element_type=F32)
    if act == "sigmoid":
        acc = _sigmoid(acc)
    if has_res:
        acc = acc + refs[2][...]
    o_ref[...] = acc.astype(o_ref.dtype)


def _matmul(a, b, *, bm, bn, out_dtype, act=None, res=None, name="matmul"):
    m, k = a.shape
    n = b.shape[1]
    bm = min(bm, m)
    in_specs = [pl.BlockSpec((bm, k), lambda j, i: (i, 0)),
                pl.BlockSpec((k, bn), lambda j, i: (0, j))]
    args = [a, b]
    if res is not None:
        in_specs.append(pl.BlockSpec((bm, bn), lambda j, i: (i, j)))
        args.append(res)
    return pl.pallas_call(
        functools.partial(_mm_kernel, act=act, has_res=res is not None),
        grid=(n // bn, m // bm),
        in_specs=in_specs,
        out_specs=pl.BlockSpec((bm, bn), lambda j, i: (i, j)),
        out_shape=jax.ShapeDtypeStruct((m, n), out_dtype),
        compiler_params=_params(2),
        name=name,
    )(*args)


def _conv_a_kernel(b_ref, c_ref, x_ref, ch_ref, xh_ref, w_ref, o_ref, p_scr, *, blocks_per_seq):
    i = pl.program_id(0)
    tm = b_ref.shape[0]
    h = BF16_SUBLANES
    p = c_ref[...].astype(F32) * x_ref[...].astype(F32)
    ph = ch_ref[...].astype(F32) * xh_ref[...].astype(F32)
    ph = jnp.where(i % blocks_per_seq == 0, 0.0, ph)
    p_scr[0:h, :] = ph
    p_scr[h:h + tm, :] = p
    w = w_ref[...]
    y = w[2:3, :] * p + w[1:2, :] * p_scr[h - 1:h - 1 + tm, :] + w[0:1, :] * p_scr[h - 2:h - 2 + tm, :]
    o_ref[...] = (b_ref[...].astype(F32) * y).astype(o_ref.dtype)


def _mixer_a(proj, conv_w, seq, tm=512):
    m = proj.shape[0]
    tm = min(tm, seq)
    h = BF16_SUBLANES
    halo = lambda col: pl.BlockSpec(
        (h, W_MIX), lambda i: (jnp.maximum(i * (tm // h) - 1, 0), col))
    main = lambda col: pl.BlockSpec((tm, W_MIX), lambda i: (i, col))
    return pl.pallas_call(
        functools.partial(_conv_a_kernel, blocks_per_seq=seq // tm),
        grid=(m // tm,),
        in_specs=[main(COL_A_B), main(COL_A_C), main(COL_A_X), halo(COL_A_C), halo(COL_A_X),
                  pl.BlockSpec((CONV_A, W_MIX), lambda i: (0, 0))],
        out_specs=pl.BlockSpec((tm, W_MIX), lambda i: (i, 0)),
        out_shape=jax.ShapeDtypeStruct((m, W_MIX), BF16),
        scratch_shapes=[pltpu.VMEM((tm + h, W_MIX), F32)],
        compiler_params=_params(1),
        name="mixer_a_conv",
    )(proj, proj, proj, proj, proj, conv_w)


def _sgu_kernel(u_ref, v_ref, ng_ref, w_ref, bexp_ref, o_ref):
    tm = u_ref.shape[0]
    vf = _gelu_tanh(v_ref[...].astype(F32))
    mu = jnp.mean(vf, axis=-1, keepdims=True)
    d = vf - mu
    var = jnp.mean(d * d, axis=-1, keepdims=True)
    vn = (d * lax.rsqrt(var + EPS) * ng_ref[...]).astype(BF16)
    t_idx = lax.broadcasted_iota(jnp.int32, (CHUNK, CHUNK), 0)
    s_idx = lax.broadcasted_iota(jnp.int32, (CHUNK, CHUNK), 1)
    causal = s_idx <= t_idx
    ws = [jnp.where(causal, w_ref[g], 0.0).astype(BF16) for g in range(SGU_GROUPS)]
    bexp = bexp_ref[...]
    for c in range(tm // CHUNK):
        rows = slice(c * CHUNK, (c + 1) * CHUNK)
        vn_c = vn[rows, :]
        mixed = jnp.concatenate(
            [jnp.dot(ws[g], vn_c[:, g * GROUP:(g + 1) * GROUP], preferred_element_type=F32)
             for g in range(SGU_GROUPS)], axis=1)
        u = _gelu_tanh(u_ref[rows, :].astype(F32))
        o_ref[rows, :] = (u * (mixed + bexp)).astype(o_ref.dtype)


def _mixer_b(proj, norm_g, w_s, b_s, tm=512):
    m = proj.shape[0]
    tm = min(tm, m)
    bexp = jnp.repeat(b_s.T, GROUP, axis=1)
    return pl.pallas_call(
        _sgu_kernel,
        grid=(m // tm,),
        in_specs=[pl.BlockSpec((tm, W_MIX), lambda i: (i, COL_B_U)),
                  pl.BlockSpec((tm, W_MIX), lambda i: (i, COL_B_V)),
                  pl.BlockSpec((1, W_MIX), lambda i: (0, 0)),
                  pl.BlockSpec((SGU_GROUPS, CHUNK, CHUNK), lambda i: (0, 0, 0)),
                  pl.BlockSpec((CHUNK, W_MIX), lambda i: (0, 0))],
        out_specs=pl.BlockSpec((tm, W_MIX), lambda i: (i, 0)),
        out_shape=jax.ShapeDtypeStruct((m, W_MIX), BF16),
        compiler_params=_params(1),
        name="mixer_b_sgu",
    )(proj, proj, norm_g.reshape(1, W_MIX), w_s, bexp)


def _qknorm_kernel(q_ref, k_ref, qg_ref, kg_ref, qo_ref, ko_ref):
    for src, g_ref, dst in ((q_ref, qg_ref, qo_ref), (k_ref, kg_ref, ko_ref)):
        g = g_ref[...]
        for h in range(SB_HEADS):
            cols = slice(h * SB_HEAD_DIM, (h + 1) * SB_HEAD_DIM)
            x = src[:, cols].astype(F32)
            ms = jnp.mean(x * x, axis=-1, keepdims=True)
            dst[:, cols] = (x * lax.rsqrt(ms + EPS) * g).astype(dst.dtype)


def _qknorm(proj, q_g, k_g, tm=512):
    m = proj.shape[0]
    tm = min(tm, m)
    spec = lambda col: pl.BlockSpec((tm, W_MIX), lambda i: (i, col))
    gspec = pl.BlockSpec((1, SB_HEAD_DIM), lambda i: (0, 0))
    return pl.pallas_call(
        _qknorm_kernel,
        grid=(m // tm,),
        in_specs=[spec(COL_C_Q), spec(COL_C_K), gspec, gspec],
        out_specs=[spec(0), spec(0)],
        out_shape=[jax.ShapeDtypeStruct((m, W_MIX), BF16)] * 2,
        compiler_params=_params(1),
        name="qk_norm",
    )(proj, proj, q_g.reshape(1, -1), k_g.reshape(1, -1))


def _attn_kernel(q_ref, k_ref, v_ref, o_ref, acc_scr, *, tq, tk):
    i = pl.program_id(2)
    nsub = tq // tk
    q = q_ref[...]
    jj = lax.broadcasted_iota(jnp.int32, (2 * tk, tk), 0)
    ss = lax.broadcasted_iota(jnp.int32, (2 * tk, tk), 1)
    jj = jnp.where(jj >= tk, jj - tk, jj)
    uneg = jnp.where(jj >= ss, -1.0, 0.0).astype(BF16)

    def block(off, r, mask):
        kb = k_ref[pl.ds(off, tk), :]
        vb = v_ref[pl.ds(off, tk), :]
        z2 = lax.dot_general(q, kb, (((1,), (1,)), ((), ())), preferred_element_type=F32)
        e = jnp.exp2(-jnp.abs(z2))
        sp = jnp.maximum(z2, 0.0) + jnp.log(1.0 + e) * (1.0 / math.log(2.0))
        if mask is not None:
            sp = jnp.where(mask, sp, 0.0)
        hi = sp.astype(BF16)
        lo = (sp - hi.astype(F32)).astype(BF16)
        incl = jnp.dot(jnp.concatenate([hi, lo], axis=1), uneg, preferred_element_type=F32)
        att = jnp.exp2(z2 + incl + r)
        if mask is not None:
            att = jnp.where(mask, att, 0.0)
        acc_scr[...] += jnp.dot(att.astype(BF16), vb, preferred_element_type=F32)
        return r - jnp.sum(sp, axis=1, keepdims=True)

    acc_scr[...] = jnp.zeros_like(acc_scr)
    r = jnp.zeros((tq, 1), F32)
    row = lax.broadcasted_iota(jnp.int32, (tq, tk), 0)
    col = lax.broadcasted_iota(jnp.int32, (tq, tk), 1)
    for d in reversed(range(nsub)):
        off = pl.multiple_of((i * nsub + d) * tk, tk)
        r = block(off, r, (col + d * tk) < row)

    def body(step, r):
        off = pl.multiple_of((i * nsub - 1 - step) * tk, tk)
        return block(off, r, None)

    lax.fori_loop(0, i * nsub, body, r)
    o_ref[...] = acc_scr[...].astype(o_ref.dtype)


def _attention(qn, kn, proj, batch, seq, tq=256, tk=128):
    m = qn.shape[0]
    tq = min(tq, seq)
    nq = seq // tq
    v_col0 = COL_C_V * (W_MIX // SB_HEAD_DIM)
    return pl.pallas_call(
        functools.partial(_attn_kernel, tq=tq, tk=tk),
        grid=(batch, SB_HEADS, nq),
        in_specs=[pl.BlockSpec((tq, SB_HEAD_DIM), lambda b, h, i: (b * nq + i, h)),
                  pl.BlockSpec((seq, SB_HEAD_DIM), lambda b, h, i: (b, h)),
                  pl.BlockSpec((seq, SB_HEAD_DIM), lambda b, h, i: (b, v_col0 + h))],
        out_specs=pl.BlockSpec((tq, SB_HEAD_DIM), lambda b, h, i: (b * nq + i, h)),
        out_shape=jax.ShapeDtypeStruct((m, W_MIX), BF16),
        scratch_shapes=[pltpu.VMEM((tq, SB_HEAD_DIM), F32)],
        compiler_params=_params(3),
        name="stick_breaking_attn",
    )(qn, kn, proj)


def _ssd_kernel(z_ref, xbc_ref, halo_ref, dt_ref, cw_ref, cb_ref, dtb_ref, alog_ref, dexp_ref,
                ng_ref, xmat_ref, o_ref, ht_scr, xp_scr):
    c = pl.program_id(1)
    L = CHUNK
    h = BF16_SUBLANES
    hi_prec = lax.Precision.HIGHEST
    gw = W_MIX // SSM_GROUPS

    @pl.when(c == 0)
    def _():
        ht_scr[...] = jnp.zeros_like(ht_scr)

    xp_scr[0:h, :] = jnp.where(c == 0, 0.0, halo_ref[...].astype(F32))
    xp_scr[h:h + L, :] = xbc_ref[...].astype(F32)
    cw = cw_ref[...]
    conv = cb_ref[...]
    for k in range(SSM_CONV):
        s0 = h - (SSM_CONV - 1) + k
        conv = conv + cw[k:k + 1, :] * xp_scr[s0:s0 + L, :]
    xc = conv * _sigmoid(conv)
    xs = xc[:, :W_MIX]
    bmat = xc[:, W_MIX:W_MIX + SSM_GROUPS * SSM_STATE].astype(BF16)
    cmat = xc[:, W_MIX + SSM_GROUPS * SSM_STATE:].astype(BF16)

    dt = _softplus(dt_ref[...] + dtb_ref[...])
    da = dt * (-jnp.exp(alog_ref[...]))
    t_i = lax.broadcasted_iota(jnp.int32, (L, L), 0)
    s_i = lax.broadcasted_iota(jnp.int32, (L, L), 1)
    causal = s_i <= t_i
    cum = jnp.dot(causal.astype(F32), da, precision=hi_prec, preferred_element_type=F32)
    cum_t = cum.T
    xmat = xmat_ref[...]
    dt_e = jnp.dot(dt, xmat, precision=hi_prec, preferred_element_type=F32)
    cum_e = jnp.dot(cum, xmat, precision=hi_prec, preferred_element_type=F32)
    cl_e = cum_e[L - 1:L, :]
    xdt = xs * dt_e
    xdec = (xdt * jnp.exp(cl_e - cum_e)).astype(BF16)
    ecum_e = jnp.exp(cum_e)
    chunk_dec = jnp.exp(cl_e)
    lane = lax.broadcasted_iota(jnp.int32, (L, LANES), 1)
    left = lane < SSM_HEAD_DIM

    ys = []
    for g in range(SSM_GROUPS):
        bg = bmat[:, g * SSM_STATE:(g + 1) * SSM_STATE]
        cg = cmat[:, g * SSM_STATE:(g + 1) * SSM_STATE]
        gcols = slice(g * gw, (g + 1) * gw)
        cb = lax.dot_general(cg, bg, (((1,), (1,)), ((), ())), preferred_element_type=F32)
        ht = ht_scr[g]
        y_off = jnp.dot(cg, ht.astype(BF16), preferred_element_type=F32)
        parts = []
        for pr in range(SSM_HPG // 2):
            ms = []
            for r in range(2):
                idx = g * SSM_HPG + 2 * pr + r
                seg = cum[:, idx:idx + 1] - cum_t[idx:idx + 1, :]
                dec = jnp.exp(jnp.where(causal, seg, -jnp.inf))
                ms.append((cb * dec).astype(BF16))
            c0 = g * gw + pr * LANES
            xpair = xdt[:, c0:c0 + LANES]
            rhs = jnp.concatenate(
                [jnp.where(left, xpair, 0.0), jnp.where(left, 0.0, xpair)], axis=0).astype(BF16)
            parts.append(jnp.dot(jnp.concatenate(ms, axis=1), rhs, preferred_element_type=F32))
        y_diag = jnp.concatenate(parts, axis=1)
        ys.append(y_diag + y_off * ecum_e[:, gcols])
        st = lax.dot_general(bg, xdec[:, gcols], (((0,), (0,)), ((), ())),
                             preferred_element_type=F32)
        ht_scr[g] = ht * chunk_dec[:, gcols] + st

    y = jnp.concatenate(ys, axis=1) + xs * dexp_ref[...]
    z = z_ref[...].astype(F32)
    y = y * (z * _sigmoid(z))
    outs = []
    for g in range(SSM_GROUPS):
        yg = y[:, g * gw:(g + 1) * gw]
        ms = jnp.mean(yg * yg, axis=-1, keepdims=True)
        outs.append(yg * lax.rsqrt(ms + EPS))
    o_ref[...] = (jnp.concatenate(outs, axis=1) * ng_ref[...]).astype(o_ref.dtype)


def _mixer_d(proj, dt_raw, conv_w, conv_b, dt_bias, a_log, d_skip, norm_g, batch, seq):
    m = proj.shape[0]
    nc = seq // CHUNK
    h = BF16_SUBLANES
    pad = LANES - SSM_HEADS
    row = lambda v: v.reshape(1, -1)
    xmat = (jnp.arange(W_MIX)[None, :] // SSM_HEAD_DIM == jnp.arange(LANES)[:, None]).astype(F32)
    const = lambda shape: pl.BlockSpec(shape, lambda b, c: (0,) * len(shape))
    return pl.pallas_call(
        _ssd_kernel,
        grid=(batch, nc),
        in_specs=[pl.BlockSpec((CHUNK, W_MIX), lambda b, c: (b * nc + c, COL_D_Z)),
                  pl.BlockSpec((CHUNK, SSM_CONV_DIM), lambda b, c: (b * nc + c, COL_D_XBC)),
                  pl.BlockSpec((h, SSM_CONV_DIM),
                               lambda b, c: (jnp.maximum((b * nc + c) * (CHUNK // h) - 1, 0), COL_D_XBC)),
                  pl.BlockSpec((CHUNK, LANES), lambda b, c: (b * nc + c, 0)),
                  const((SSM_CONV, SSM_CONV_DIM)), const((1, SSM_CONV_DIM)),
                  const((1, LANES)), const((1, LANES)), const((1, W_MIX)), const((1, W_MIX)),
                  const((LANES, W_MIX))],
        out_specs=pl.BlockSpec((CHUNK, W_MIX), lambda b, c: (b * nc + c, 0)),
        out_shape=jax.ShapeDtypeStruct((m, W_MIX), BF16),
        scratch_shapes=[pltpu.VMEM((SSM_GROUPS, SSM_STATE, W_MIX // SSM_GROUPS), F32),
                        pltpu.VMEM((CHUNK + h, SSM_CONV_DIM), F32)],
        compiler_params=_params(2),
        name="mixer_d_ssd",
    )(proj, proj, proj, dt_raw, conv_w, row(conv_b),
      row(jnp.pad(dt_bias, (0, pad))), row(jnp.pad(a_log, (0, pad))),
      row(jnp.repeat(d_skip, SSM_HEAD_DIM)), row(norm_g), xmat)


def _merge_kernel(ya_ref, yb_ref, yc_ref, yd_ref, wb_ref, g0_ref, g1_ref, g2_ref, g3_ref, o_ref):
    acc = None
    for j, (y_ref, g_ref) in enumerate(
            zip((ya_ref, yb_ref, yc_ref, yd_ref), (g0_ref, g1_ref, g2_ref, g3_ref))):
        t = jnp.dot(y_ref[...], wb_ref[j], preferred_element_type=F32) * g_ref[...].astype(F32)
        acc = t if acc is None else acc + t
    o_ref[...] = acc.astype(o_ref.dtype)


def _merge(ys, w_branch, gates, bm=512, bn=1024):
    m = gates.shape[0]
    bm = min(bm, m)
    nb = D_MODEL // bn
    yspec = pl.BlockSpec((bm, W_MIX), lambda i, n: (i, 0))
    gspec = lambda j: pl.BlockSpec((bm, bn), lambda i, n: (i, j * nb + n))
    return pl.pallas_call(
        _merge_kernel,
        grid=(m // bm, nb),
        in_specs=[yspec] * N_BRANCH
        + [pl.BlockSpec((N_BRANCH, W_MIX, bn), lambda i, n: (0, 0, n))]
        + [gspec(j) for j in range(N_BRANCH)],
        out_specs=pl.BlockSpec((bm, bn), lambda i, n: (i, n)),
        out_shape=jax.ShapeDtypeStruct((m, D_MODEL), BF16),
        compiler_params=_params(2),
        name="branch_merge",
    )(*ys, w_branch, gates, gates, gates, gates)


def _ffn_gu_kernel(h_ref, wg_ref, wu_ref, o_ref):
    h = h_ref[...]
    g = jnp.dot(h, wg_ref[...], preferred_element_type=F32)
    u = jnp.dot(h, wu_ref[...], preferred_element_type=F32)
    o_ref[...] = (g * _sigmoid(g) * u).astype(o_ref.dtype)


def _ffn_gate_up(h, wg, wu, bm=1024, bn=256):
    m, k = h.shape
    f = wg.shape[1]
    bm = min(bm, m)
    return pl.pallas_call(
        _ffn_gu_kernel,
        grid=(m // bm, f // bn),
        in_specs=[pl.BlockSpec((bm, k), lambda i, j: (i, 0)),
                  pl.BlockSpec((k, bn), lambda i, j: (0, j)),
                  pl.BlockSpec((k, bn), lambda i, j: (0, j))],
        out_specs=pl.BlockSpec((bm, bn), lambda i, j: (i, j)),
        out_shape=jax.ShapeDtypeStruct((m, f), BF16),
        compiler_params=_params(2),
        name="ffn_gate_up",
    )(h, wg, wu)


def kernel(x, norm_mix, w_in, conv_a, sgu_norm, sgu_w, sgu_b, q_norm, k_norm, ssm_conv_w, ssm_conv_b, ssm_dt_bias, ssm_a_log, ssm_d, ssm_norm, w_branch, w_out, norm_ffn, w_ffn_gate, w_ffn_up, w_ffn_down):
    batch, seq, d = x.shape
    m = batch * seq
    depth = w_in.shape[0]
    xr = x.reshape(m, d)
    q_fold = (1.0 / math.sqrt(SB_HEAD_DIM)) * (1.0 / math.log(2.0))
    for l in range(depth):
        w_mix = w_in[l, :, :N_MIX].astype(BF16)
        w_dt = jnp.pad(w_in[l, :, N_MIX:N_MIX + SSM_HEADS], ((0, 0), (0, LANES - SSM_HEADS))).astype(BF16)
        w_gate = w_in[l, :, N_MIX + SSM_HEADS:].astype(BF16)

        hn = _rmsnorm(xr, norm_mix[l])
        proj = _matmul(hn, w_mix, bm=512, bn=1536, out_dtype=BF16, name="in_proj_mix")
        dt_raw = _matmul(hn, w_dt, bm=1024, bn=LANES, out_dtype=F32, name="in_proj_dt")
        gates = _matmul(hn, w_gate, bm=512, bn=1024, out_dtype=BF16, act="sigmoid", name="in_proj_gates")

        y_a = _mixer_a(proj, conv_a[l], seq)
        y_b = _mixer_b(proj, sgu_norm[l], sgu_w[l], sgu_b[l])
        qn, kn = _qknorm(proj, q_norm[l] * q_fold, k_norm[l])
        y_c = _attention(qn, kn, proj, batch, seq)
        y_d = _mixer_d(proj, dt_raw, ssm_conv_w[l], ssm_conv_b[l], ssm_dt_bias[l], ssm_a_log[l],
                       ssm_d[l], ssm_norm[l], batch, seq)

        merged = _merge((y_a, y_b, y_c, y_d), w_branch[l].astype(BF16), gates)
        xr = _matmul(merged, w_out[l].astype(BF16), bm=512, bn=1024, out_dtype=F32, res=xr,
                     name="out_proj")

        hn = _rmsnorm(xr, norm_ffn[l])
        act = _ffn_gate_up(hn, w_ffn_gate[l].astype(BF16), w_ffn_up[l].astype(BF16))
        xr = _matmul(act, w_ffn_down[l].astype(BF16), bm=512, bn=256, out_dtype=F32, res=xr,
                     name="ffn_down")
    return xr.reshape(batch, seq, d)
```

```python
import functools
import math

import jax
import jax.numpy as jnp
from jax import lax
from jax.experimental import pallas as pl
from jax.experimental.pallas import tpu as pltpu

F32 = jnp.float32
BF16 = jnp.bfloat16

EPS = 1e-6
D_MODEL = 4096
W_MIX = D_MODEL // 4
GROUP = 128
CHUNK = 128
CONV_A = 3
SGU_GROUPS = W_MIX // GROUP
SB_HEAD_DIM = 128
SB_HEADS = W_MIX // SB_HEAD_DIM
SSM_HEAD_DIM = 64
SSM_HEADS = W_MIX // SSM_HEAD_DIM
SSM_GROUPS = 2
SSM_HPG = SSM_HEADS // SSM_GROUPS
SSM_STATE = 128
SSM_CONV = 4
SSM_CONV_DIM = W_MIX + 2 * SSM_GROUPS * SSM_STATE
N_BRANCH = 4
N_MIX = 9 * W_MIX + SSM_CONV_DIM
LANES = 128
BF16_SUBLANES = 16
VMEM_LIMIT = 52 * 1024 * 1024

COL_A_B, COL_A_C, COL_A_X = 0, 1, 2
COL_B_U, COL_B_V = 3, 4
COL_C_Q, COL_C_K, COL_C_V = 5, 6, 7
COL_D_Z = 8
COL_D_XBC = (9 * W_MIX) // SSM_CONV_DIM


def _params(n_axes):
    return pltpu.CompilerParams(
        dimension_semantics=("arbitrary",) * n_axes, vmem_limit_bytes=VMEM_LIMIT)


def _sigmoid(x):
    return 1.0 / (1.0 + jnp.exp(-x))


def _softplus(x):
    return jnp.maximum(x, 0.0) + jnp.log1p(jnp.exp(-jnp.abs(x)))


def _gelu_tanh(x):
    c = math.sqrt(2.0 / math.pi)
    return x * (0.5 * (1.0 + jnp.tanh(c * (x + 0.044715 * (x * x * x)))))


def _rmsnorm_kernel(x_ref, g_ref, o_ref):
    x = x_ref[...]
    ms = jnp.mean(x * x, axis=-1, keepdims=True)
    o_ref[...] = (x * lax.rsqrt(ms + EPS) * g_ref[...]).astype(o_ref.dtype)


def _rmsnorm(x, g, tm=256):
    m, d = x.shape
    return pl.pallas_call(
        _rmsnorm_kernel,
        grid=(m // tm,),
        in_specs=[pl.BlockSpec((tm, d), lambda i: (i, 0)),
                  pl.BlockSpec((1, d), lambda i: (0, 0))],
        out_specs=pl.BlockSpec((tm, d), lambda i: (i, 0)),
        out_shape=jax.ShapeDtypeStruct((m, d), BF16),
        compiler_params=_params(1),
        name="rmsnorm",
    )(x, g.reshape(1, d))


def _mm_kernel(*refs, act, has_res):
    a_ref, b_ref = refs[0], refs[1]
    o_ref = refs[-1]
    acc = jnp.dot(a_ref[...], b_ref[...], preferred_element_type=F32)
    if act == "sigmoid":
        acc = _sigmoid(acc)
    if has_res:
        acc = acc + refs[2][...]
    o_ref[...] = acc.astype(o_ref.dtype)


def _matmul(a, b, *, bm, bn, out_dtype, act=None, res=None, m_outer=False, name="matmul"):
    m, k = a.shape
    n = b.shape[1]
    bm = min(bm, m)
    if m_outer:
        grid = (m // bm, n // bn)
        ij = lambda i, j: (i, j)
    else:
        grid = (n // bn, m // bm)
        ij = lambda j, i: (i, j)
    in_specs = [pl.BlockSpec((bm, k), lambda *g: (ij(*g)[0], 0)),
                pl.BlockSpec((k, bn), lambda *g: (0, ij(*g)[1]))]
    args = [a, b]
    if res is not None:
        in_specs.append(pl.BlockSpec((bm, bn), ij))
        args.append(res)
    return pl.pallas_call(
        functools.partial(_mm_kernel, act=act, has_res=res is not None),
        grid=grid,
        in_specs=in_specs,
        out_specs=pl.BlockSpec((bm, bn), ij),
        out_shape=jax.ShapeDtypeStruct((m, n), out_dtype),
        compiler_params=_params(2),
        name=name,
    )(*args)


def _conv_a_kernel(b_ref, c_ref, x_ref, ch_ref, xh_ref, w_ref, o_ref, p_scr, *, blocks_per_seq):
    i = pl.program_id(0)
    tm = b_ref.shape[0]
    h = BF16_SUBLANES
    p = c_ref[...].astype(F32) * x_ref[...].astype(F32)
    ph = ch_ref[...].astype(F32) * xh_ref[...].astype(F32)
    ph = jnp.where(i % blocks_per_seq == 0, 0.0, ph)
    p_scr[0:h, :] = ph
    p_scr[h:h + tm, :] = p
    w = w_ref[...]
    y = w[2:3, :] * p + w[1:2, :] * p_scr[h - 1:h - 1 + tm, :] + w[0:1, :] * p_scr[h - 2:h - 2 + tm, :]
    o_ref[...] = (b_ref[...].astype(F32) * y).astype(o_ref.dtype)


def _mixer_a(proj, conv_w, seq, tm=512):
    m = proj.shape[0]
    tm = min(tm, seq)
    h = BF16_SUBLANES
    halo = lambda col: pl.BlockSpec(
        (h, W_MIX), lambda i: (jnp.maximum(i * (tm // h) - 1, 0), col))
    main = lambda col: pl.BlockSpec((tm, W_MIX), lambda i: (i, col))
    return pl.pallas_call(
        functools.partial(_conv_a_kernel, blocks_per_seq=seq // tm),
        grid=(m // tm,),
        in_specs=[main(COL_A_B), main(COL_A_C), main(COL_A_X), halo(COL_A_C), halo(COL_A_X),
                  pl.BlockSpec((CONV_A, W_MIX), lambda i: (0, 0))],
        out_specs=pl.BlockSpec((tm, W_MIX), lambda i: (i, 0)),
        out_shape=jax.ShapeDtypeStruct((m, W_MIX), BF16),
        scratch_shapes=[pltpu.VMEM((tm + h, W_MIX), F32)],
        compiler_params=_params(1),
        name="mixer_a_conv",
    )(proj, proj, proj, proj, proj, conv_w)


def _sgu_kernel(u_ref, v_ref, ng_ref, w_ref, bexp_ref, o_ref):
    tm = u_ref.shape[0]
    vf = _gelu_tanh(v_ref[...].astype(F32))
    mu = jnp.mean(vf, axis=-1, keepdims=True)
    d = vf - mu
    var = jnp.mean(d * d, axis=-1, keepdims=True)
    vn = (d * lax.rsqrt(var + EPS) * ng_ref[...]).astype(BF16)
    t_idx = lax.broadcasted_iota(jnp.int32, (CHUNK, CHUNK), 0)
    s_idx = lax.broadcasted_iota(jnp.int32, (CHUNK, CHUNK), 1)
    causal = s_idx <= t_idx
    ws = [jnp.where(causal, w_ref[g], 0.0).astype(BF16) for g in range(SGU_GROUPS)]
    bexp = bexp_ref[...]
    for c in range(tm // CHUNK):
        rows = slice(c * CHUNK, (c + 1) * CHUNK)
        vn_c = vn[rows, :]
        mixed = jnp.concatenate(
            [jnp.dot(ws[g], vn_c[:, g * GROUP:(g + 1) * GROUP], preferred_element_type=F32)
             for g in range(SGU_GROUPS)], axis=1)
        u = _gelu_tanh(u_ref[rows, :].astype(F32))
        o_ref[rows, :] = (u * (mixed + bexp)).astype(o_ref.dtype)


def _mixer_b(proj, norm_g, w_s, b_s, tm=512):
    m = proj.shape[0]
    tm = min(tm, m)
    bexp = jnp.repeat(b_s.T, GROUP, axis=1)
    return pl.pallas_call(
        _sgu_kernel,
        grid=(m // tm,),
        in_specs=[pl.BlockSpec((tm, W_MIX), lambda i: (i, COL_B_U)),
                  pl.BlockSpec((tm, W_MIX), lambda i: (i, COL_B_V)),
                  pl.BlockSpec((1, W_MIX), lambda i: (0, 0)),
                  pl.BlockSpec((SGU_GROUPS, CHUNK, CHUNK), lambda i: (0, 0, 0)),
                  pl.BlockSpec((CHUNK, W_MIX), lambda i: (0, 0))],
        out_specs=pl.BlockSpec((tm, W_MIX), lambda i: (i, 0)),
        out_shape=jax.ShapeDtypeStruct((m, W_MIX), BF16),
        compiler_params=_params(1),
        name="mixer_b_sgu",
    )(proj, proj, norm_g.reshape(1, W_MIX), w_s, bexp)


def _qknorm_kernel(q_ref, k_ref, qg_ref, kg_ref, qo_ref, ko_ref):
    for src, g_ref, dst in ((q_ref, qg_ref, qo_ref), (k_ref, kg_ref, ko_ref)):
        g = g_ref[...]
        for h in range(SB_HEADS):
            cols = slice(h * SB_HEAD_DIM, (h + 1) * SB_HEAD_DIM)
            x = src[:, cols].astype(F32)
            ms = jnp.mean(x * x, axis=-1, keepdims=True)
            dst[:, cols] = (x * lax.rsqrt(ms + EPS) * g).astype(dst.dtype)


def _qknorm(proj, q_g, k_g, tm=512):
    m = proj.shape[0]
    tm = min(tm, m)
    spec = lambda col: pl.BlockSpec((tm, W_MIX), lambda i: (i, col))
    gspec = pl.BlockSpec((1, SB_HEAD_DIM), lambda i: (0, 0))
    return pl.pallas_call(
        _qknorm_kernel,
        grid=(m // tm,),
        in_specs=[spec(COL_C_Q), spec(COL_C_K), gspec, gspec],
        out_specs=[spec(0), spec(0)],
        out_shape=[jax.ShapeDtypeStruct((m, W_MIX), BF16)] * 2,
        compiler_params=_params(1),
        name="qk_norm",
    )(proj, proj, q_g.reshape(1, -1), k_g.reshape(1, -1))


def _attn_kernel(q_ref, k_ref, v_ref, o_ref, acc_scr, *, tq, sub):
    i = pl.program_id(2)
    nsb = tq // sub
    q = q_ref[...]
    jj = lax.broadcasted_iota(jnp.int32, (2 * sub, sub), 0)
    ss = lax.broadcasted_iota(jnp.int32, (2 * sub, sub), 1)
    jj = jnp.where(jj >= sub, jj - sub, jj)
    uneg = jnp.where(jj >= ss, -1.0, 0.0).astype(BF16)

    def tile(off, r, mask):
        kb = k_ref[pl.ds(off, tq), :]
        vb = v_ref[pl.ds(off, tq), :]
        z2 = lax.dot_general(q, kb, (((1,), (1,)), ((), ())), preferred_element_type=F32)
        e = jnp.exp2(-jnp.abs(z2))
        sp = jnp.maximum(z2, 0.0) + jnp.log(1.0 + e) * (1.0 / math.log(2.0))
        if mask is not None:
            sp = jnp.where(mask, sp, 0.0)
        hi = sp.astype(BF16)
        lo = (sp - hi.astype(F32)).astype(BF16)
        incl, tot = [], []
        for c in range(nsb):
            cs = slice(c * sub, (c + 1) * sub)
            incl.append(jnp.dot(jnp.concatenate([hi[:, cs], lo[:, cs]], axis=1), uneg,
                                preferred_element_type=F32))
            tot.append(jnp.sum(sp[:, cs], axis=1, keepdims=True))
        att = [None] * nsb
        for c in reversed(range(nsb)):
            cs = slice(c * sub, (c + 1) * sub)
            a = jnp.exp2(z2[:, cs] + incl[c] + r)
            if mask is not None:
                a = jnp.where(mask[:, cs], a, 0.0)
            att[c] = a.astype(BF16)
            r = r - tot[c]
        acc_scr[...] += jnp.dot(jnp.concatenate(att, axis=1), vb, preferred_element_type=F32)
        return r

    acc_scr[...] = jnp.zeros_like(acc_scr)
    row = lax.broadcasted_iota(jnp.int32, (tq, tq), 0)
    col = lax.broadcasted_iota(jnp.int32, (tq, tq), 1)
    r = tile(pl.multiple_of(i * tq, tq), jnp.zeros((tq, 1), F32), col < row)

    def body(step, r):
        return tile(pl.multiple_of((i - 1 - step) * tq, tq), r, None)

    lax.fori_loop(0, i, body, r)
    o_ref[...] = acc_scr[...].astype(o_ref.dtype)


def _attention(qn, kn, proj, batch, seq, tq=512, sub=128):
    m = qn.shape[0]
    tq = min(tq, seq)
    nq = seq // tq
    v_col0 = COL_C_V * (W_MIX // SB_HEAD_DIM)
    return pl.pallas_call(
        functools.partial(_attn_kernel, tq=tq, sub=sub),
        grid=(batch, SB_HEADS, nq),
        in_specs=[pl.BlockSpec((tq, SB_HEAD_DIM), lambda b, h, i: (b * nq + i, h)),
                  pl.BlockSpec((seq, SB_HEAD_DIM), lambda b, h, i: (b, h)),
                  pl.BlockSpec((seq, SB_HEAD_DIM), lambda b, h, i: (b, v_col0 + h))],
        out_specs=pl.BlockSpec((tq, SB_HEAD_DIM), lambda b, h, i: (b * nq + i, h)),
        out_shape=jax.ShapeDtypeStruct((m, W_MIX), BF16),
        scratch_shapes=[pltpu.VMEM((tq, SB_HEAD_DIM), F32)],
        compiler_params=_params(3),
        name="stick_breaking_attn",
    )(qn, kn, proj)


def _ssd_kernel(z_ref, xbc_ref, halo_ref, dt_ref, cw_ref, cb_ref, dtb_ref, alog_ref, dexp_ref,
                ng_ref, xmat_ref, o_ref, ht_scr, xp_scr):
    c = pl.program_id(1)
    L = CHUNK
    h = BF16_SUBLANES
    hi_prec = lax.Precision.HIGHEST
    gw = W_MIX // SSM_GROUPS

    @pl.when(c == 0)
    def _():
        ht_scr[...] = jnp.zeros_like(ht_scr)

    xp_scr[0:h, :] = jnp.where(c == 0, 0.0, halo_ref[...].astype(F32))
    xp_scr[h:h + L, :] = xbc_ref[...].astype(F32)
    cw = cw_ref[...]
    conv = cb_ref[...]
    for k in range(SSM_CONV):
        s0 = h - (SSM_CONV - 1) + k
        conv = conv + cw[k:k + 1, :] * xp_scr[s0:s0 + L, :]
    xc = conv * _sigmoid(conv)
    xs = xc[:, :W_MIX]
    bmat = xc[:, W_MIX:W_MIX + SSM_GROUPS * SSM_STATE].astype(BF16)
    cmat = xc[:, W_MIX + SSM_GROUPS * SSM_STATE:].astype(BF16)

    dt = _softplus(dt_ref[...] + dtb_ref[...])
    da = dt * (-jnp.exp(alog_ref[...]))
    t_i = lax.broadcasted_iota(jnp.int32, (L, L), 0)
    s_i = lax.broadcasted_iota(jnp.int32, (L, L), 1)
    causal = s_i <= t_i
    cum = jnp.dot(causal.astype(F32), da, precision=hi_prec, preferred_element_type=F32)
    cum_t = cum.T
    xmat = xmat_ref[...]
    dt_e = jnp.dot(dt, xmat, precision=hi_prec, preferred_element_type=F32)
    cum_e = jnp.dot(cum, xmat, precision=hi_prec, preferred_element_type=F32)
    cl_e = cum_e[L - 1:L, :]
    xdt = xs * dt_e
    xdec = (xdt * jnp.exp(cl_e - cum_e)).astype(BF16)
    ecum_e = jnp.exp(cum_e)
    chunk_dec = jnp.exp(cl_e)
    lane = lax.broadcasted_iota(jnp.int32, (L, LANES), 1)
    left = lane < SSM_HEAD_DIM

    ys = []
    for g in range(SSM_GROUPS):
        bg = bmat[:, g * SSM_STATE:(g + 1) * SSM_STATE]
        cg = cmat[:, g * SSM_STATE:(g + 1) * SSM_STATE]
        gcols = slice(g * gw, (g + 1) * gw)
        cb = lax.dot_general(cg, bg, (((1,), (1,)), ((), ())), preferred_element_type=F32)
        ht = ht_scr[g]
        y_off = jnp.dot(cg, ht.astype(BF16), preferred_element_type=F32)
        parts = []
        for pr in range(SSM_HPG // 2):
            ms = []
            for r in range(2):
                idx = g * SSM_HPG + 2 * pr + r
                seg = cum[:, idx:idx + 1] - cum_t[idx:idx + 1, :]
                dec = jnp.exp(jnp.where(causal, seg, -jnp.inf))
                ms.append((cb * dec).astype(BF16))
            c0 = g * gw + pr * LANES
            xpair = xdt[:, c0:c0 + LANES]
            rhs = jnp.concatenate(
                [jnp.where(left, xpair, 0.0), jnp.where(left, 0.0, xpair)], axis=0).astype(BF16)
            parts.append(jnp.dot(jnp.concatenate(ms, axis=1), rhs, preferred_element_type=F32))
        y_diag = jnp.concatenate(parts, axis=1)
        ys.append(y_diag + y_off * ecum_e[:, gcols])
        st = lax.dot_general(bg, xdec[:, gcols], (((0,), (0,)), ((), ())),
                             preferred_element_type=F32)
        ht_scr[g] = ht * chunk_dec[:, gcols] + st

    y = jnp.concatenate(ys, axis=1) + xs * dexp_ref[...]
    z = z_ref[...].astype(F32)
    y = y * (z * _sigmoid(z))
    outs = []
    for g in range(SSM_GROUPS):
        yg = y[:, g * gw:(g + 1) * gw]
        ms = jnp.mean(yg * yg, axis=-1, keepdims=True)
        outs.append(yg * lax.rsqrt(ms + EPS))
    o_ref[...] = (jnp.concatenate(outs, axis=1) * ng_ref[...]).astype(o_ref.dtype)


def _mixer_d(proj, dt_raw, conv_w, conv_b, dt_bias, a_log, d_skip, norm_g, batch, seq):
    m = proj.shape[0]
    nc = seq // CHUNK
    h = BF16_SUBLANES
    pad = LANES - SSM_HEADS
    row = lambda v: v.reshape(1, -1)
    xmat = (jnp.arange(W_MIX)[None, :] // SSM_HEAD_DIM == jnp.arange(LANES)[:, None]).astype(F32)
    const = lambda shape: pl.BlockSpec(shape, lambda b, c: (0,) * len(shape))
    return pl.pallas_call(
        _ssd_kernel,
        grid=(batch, nc),
        in_specs=[pl.BlockSpec((CHUNK, W_MIX), lambda b, c: (b * nc + c, COL_D_Z)),
                  pl.BlockSpec((CHUNK, SSM_CONV_DIM), lambda b, c: (b * nc + c, COL_D_XBC)),
                  pl.BlockSpec((h, SSM_CONV_DIM),
                               lambda b, c: (jnp.maximum((b * nc + c) * (CHUNK // h) - 1, 0), COL_D_XBC)),
                  pl.BlockSpec((CHUNK, LANES), lambda b, c: (b * nc + c, 0)),
                  const((SSM_CONV, SSM_CONV_DIM)), const((1, SSM_CONV_DIM)),
                  const((1, LANES)), const((1, LANES)), const((1, W_MIX)), const((1, W_MIX)),
                  const((LANES, W_MIX))],
        out_specs=pl.BlockSpec((CHUNK, W_MIX), lambda b, c: (b * nc + c, 0)),
        out_shape=jax.ShapeDtypeStruct((m, W_MIX), BF16),
        scratch_shapes=[pltpu.VMEM((SSM_GROUPS, SSM_STATE, W_MIX // SSM_GROUPS), F32),
                        pltpu.VMEM((CHUNK + h, SSM_CONV_DIM), F32)],
        compiler_params=_params(2),
        name="mixer_d_ssd",
    )(proj, proj, proj, dt_raw, conv_w, row(conv_b),
      row(jnp.pad(dt_bias, (0, pad))), row(jnp.pad(a_log, (0, pad))),
      row(jnp.repeat(d_skip, SSM_HEAD_DIM)), row(norm_g), xmat)


def _merge_kernel(ya_ref, yb_ref, yc_ref, yd_ref, wb_ref, g0_ref, g1_ref, g2_ref, g3_ref, o_ref):
    acc = None
    for j, (y_ref, g_ref) in enumerate(
            zip((ya_ref, yb_ref, yc_ref, yd_ref), (g0_ref, g1_ref, g2_ref, g3_ref))):
        t = jnp.dot(y_ref[...], wb_ref[j], preferred_element_type=F32) * g_ref[...].astype(F32)
        acc = t if acc is None else acc + t
    o_ref[...] = acc.astype(o_ref.dtype)


def _merge(ys, w_branch, gates, bm=512, bn=1024):
    m = gates.shape[0]
    bm = min(bm, m)
    nb = D_MODEL // bn
    yspec = pl.BlockSpec((bm, W_MIX), lambda i, n: (i, 0))
    gspec = lambda j: pl.BlockSpec((bm, bn), lambda i, n: (i, j * nb + n))
    return pl.pallas_call(
        _merge_kernel,
        grid=(m // bm, nb),
        in_specs=[yspec] * N_BRANCH
        + [pl.BlockSpec((N_BRANCH, W_MIX, bn), lambda i, n: (0, 0, n))]
        + [gspec(j) for j in range(N_BRANCH)],
        out_specs=pl.BlockSpec((bm, bn), lambda i, n: (i, n)),
        out_shape=jax.ShapeDtypeStruct((m, D_MODEL), BF16),
        compiler_params=_params(2),
        name="branch_merge",
    )(*ys, w_branch, gates, gates, gates, gates)


def _ffn_gu_kernel(h_ref, wg_ref, wu_ref, o_ref):
    h = h_ref[...]
    g = jnp.dot(h, wg_ref[...], preferred_element_type=F32)
    u = jnp.dot(h, wu_ref[...], preferred_element_type=F32)
    o_ref[...] = (g * _sigmoid(g) * u).astype(o_ref.dtype)


def _ffn_gate_up(h, wg, wu, bm=1024, bn=256):
    m, k = h.shape
    f = wg.shape[1]
    bm = min(bm, m)
    return pl.pallas_call(
        _ffn_gu_kernel,
        grid=(m // bm, f // bn),
        in_specs=[pl.BlockSpec((bm, k), lambda i, j: (i, 0)),
                  pl.BlockSpec((k, bn), lambda i, j: (0, j)),
                  pl.BlockSpec((k, bn), lambda i, j: (0, j))],
        out_specs=pl.BlockSpec((bm, bn), lambda i, j: (i, j)),
        out_shape=jax.ShapeDtypeStruct((m, f), BF16),
        compiler_params=_params(2),
        name="ffn_gate_up",
    )(h, wg, wu)


def kernel(x, norm_mix, w_in, conv_a, sgu_norm, sgu_w, sgu_b, q_norm, k_norm, ssm_conv_w, ssm_conv_b, ssm_dt_bias, ssm_a_log, ssm_d, ssm_norm, w_branch, w_out, norm_ffn, w_ffn_gate, w_ffn_up, w_ffn_down):
    batch, seq, d = x.shape
    m = batch * seq
    depth = w_in.shape[0]
    xr = x.reshape(m, d)
    q_fold = (1.0 / math.sqrt(SB_HEAD_DIM)) * (1.0 / math.log(2.0))
    for l in range(depth):
        w_mix = w_in[l, :, :N_MIX].astype(BF16)
        w_dt = jnp.pad(w_in[l, :, N_MIX:N_MIX + SSM_HEADS], ((0, 0), (0, LANES - SSM_HEADS))).astype(BF16)
        w_gate = w_in[l, :, N_MIX + SSM_HEADS:].astype(BF16)

        hn = _rmsnorm(xr, norm_mix[l])
        proj = _matmul(hn, w_mix, bm=512, bn=1536, out_dtype=BF16, name="in_proj_mix")
        dt_raw = _matmul(hn, w_dt, bm=1024, bn=LANES, out_dtype=F32, name="in_proj_dt")
        gates = _matmul(hn, w_gate, bm=512, bn=1024, out_dtype=BF16, act="sigmoid", name="in_proj_gates")

        y_a = _mixer_a(proj, conv_a[l], seq)
        y_b = _mixer_b(proj, sgu_norm[l], sgu_w[l], sgu_b[l])
        qn, kn = _qknorm(proj, q_norm[l] * q_fold, k_norm[l])
        y_c = _attention(qn, kn, proj, batch, seq)
        y_d = _mixer_d(proj, dt_raw, ssm_conv_w[l], ssm_conv_b[l], ssm_dt_bias[l], ssm_a_log[l],
                       ssm_d[l], ssm_norm[l], batch, seq)

        merged = _merge((y_a, y_b, y_c, y_d), w_branch[l].astype(BF16), gates)
        xr = _matmul(merged, w_out[l].astype(BF16), bm=512, bn=1024, out_dtype=F32, res=xr,
                     name="out_proj")

        hn = _rmsnorm(xr, norm_ffn[l])
        act = _ffn_gate_up(hn, w_ffn_gate[l].astype(BF16), w_ffn_up[l].astype(BF16))
        xr = _matmul(act, w_ffn_down[l].astype(BF16), bm=512, bn=256, out_dtype=F32, res=xr,
                     m_outer=True, name="ffn_down")
    return xr.reshape(batch, seq, d)
```

```python
import functools
import math

import jax
import jax.numpy as jnp
from jax import lax
from jax.experimental import pallas as pl
from jax.experimental.pallas import tpu as pltpu

F32 = jnp.float32
BF16 = jnp.bfloat16

EPS = 1e-6
D_MODEL = 4096
W_MIX = D_MODEL // 4
GROUP = 128
CHUNK = 128
CONV_A = 3
SGU_GROUPS = W_MIX // GROUP
SB_HEAD_DIM = 128
SB_HEADS = W_MIX // SB_HEAD_DIM
SSM_HEAD_DIM = 64
SSM_HEADS = W_MIX // SSM_HEAD_DIM
SSM_GROUPS = 2
SSM_HPG = SSM_HEADS // SSM_GROUPS
SSM_STATE = 128
SSM_CONV = 4
SSM_CONV_DIM = W_MIX + 2 * SSM_GROUPS * SSM_STATE
N_BRANCH = 4
N_MIX = 9 * W_MIX + SSM_CONV_DIM
GATE_SHIFT = SSM_HEADS
LANES = 128
BF16_SUBLANES = 16
VMEM_LIMIT = 52 * 1024 * 1024

COL_A_B, COL_A_C, COL_A_X = 0, 1, 2
COL_B_U, COL_B_V = 3, 4
COL_C_Q, COL_C_K, COL_C_V = 5, 6, 7
COL_D_Z = 8
COL_D_XBC = (9 * W_MIX) // SSM_CONV_DIM


def _params(n_axes):
    return pltpu.CompilerParams(
        dimension_semantics=("arbitrary",) * n_axes, vmem_limit_bytes=VMEM_LIMIT)


def _sigmoid(x):
    return 1.0 / (1.0 + jnp.exp(-x))


def _softplus(x):
    return jnp.maximum(x, 0.0) + jnp.log1p(jnp.exp(-jnp.abs(x)))


def _gelu_tanh(x):
    c = math.sqrt(2.0 / math.pi)
    return x * (0.5 * (1.0 + jnp.tanh(c * (x + 0.044715 * (x * x * x)))))


def _rmsnorm_kernel(x_ref, g_ref, o_ref):
    x = x_ref[...]
    ms = jnp.mean(x * x, axis=-1, keepdims=True)
    o_ref[...] = (x * lax.rsqrt(ms + EPS) * g_ref[...]).astype(o_ref.dtype)


def _rmsnorm(x, g, tm=256):
    m, d = x.shape
    return pl.pallas_call(
        _rmsnorm_kernel,
        grid=(m // tm,),
        in_specs=[pl.BlockSpec((tm, d), lambda i: (i, 0)),
                  pl.BlockSpec((1, d), lambda i: (0, 0))],
        out_specs=pl.BlockSpec((tm, d), lambda i: (i, 0)),
        out_shape=jax.ShapeDtypeStruct((m, d), BF16),
        compiler_params=_params(1),
        name="rmsnorm",
    )(x, g.reshape(1, d))


CAST_ROWS = 512


def _cast_weight_block(dst_scr, *src_refs, shift=0):
    k, width = dst_scr.shape
    rows_per_step = min(CAST_ROWS, k)

    def body(s, carry):
        rows = pl.ds(pl.multiple_of(s * rows_per_step, rows_per_step), rows_per_step)
        w = [r[rows, :] for r in src_refs]
        w = w[0] if len(w) == 1 else jnp.concatenate(w, axis=1)
        dst_scr[rows, :] = w[:, shift:shift + width].astype(dst_scr.dtype)
        return carry

    lax.fori_loop(0, k // rows_per_step, body, 0)


def _mm_kernel(*refs, act, has_res, cast_w):
    a_ref, w_ref = refs[0], refs[1]
    if cast_w:
        o_ref, w_scr = refs[-2], refs[-1]

        @pl.when(pl.program_id(1) == 0)
        def _():
            _cast_weight_block(w_scr, w_ref)

        w = w_scr[...]
    else:
        o_ref = refs[-1]
        w = w_ref[...]
    acc = jnp.dot(a_ref[...], w, preferred_element_type=F32)
    if act == "sigmoid":
        acc = _sigmoid(acc)
    if has_res:
        acc = acc + refs[2][...]
    o_ref[...] = acc.astype(o_ref.dtype)


def _matmul(a, w, *, layer, n, bm, bn, out_dtype, act=None, res=None, m_outer=False, name="matmul"):
    m, k = a.shape
    bm = min(bm, m)
    cast_w = w.dtype != BF16
    assert not (cast_w and m_outer)
    if m_outer:
        grid = (m // bm, n // bn)
        ij = lambda i, j: (i, j)
    else:
        grid = (n // bn, m // bm)
        ij = lambda j, i: (i, j)
    in_specs = [pl.BlockSpec((bm, k), lambda *g: (ij(*g)[0], 0)),
                pl.BlockSpec((None, k, bn), lambda *g: (layer, 0, ij(*g)[1]))]
    args = [a, w]
    if res is not None:
        in_specs.append(pl.BlockSpec((bm, bn), ij))
        args.append(res)
    return pl.pallas_call(
        functools.partial(_mm_kernel, act=act, has_res=res is not None, cast_w=cast_w),
        grid=grid,
        in_specs=in_specs,
        out_specs=pl.BlockSpec((bm, bn), ij),
        out_shape=jax.ShapeDtypeStruct((m, n), out_dtype),
        scratch_shapes=[pltpu.VMEM((k, bn), BF16)] if cast_w else [],
        compiler_params=_params(2),
        name=name,
    )(*args)


def _gates_kernel(a_ref, wm_ref, wx_ref, o_ref, w_scr, *, shift):
    @pl.when(pl.program_id(1) == 0)
    def _():
        _cast_weight_block(w_scr, wm_ref, wx_ref, shift=shift)

    acc = jnp.dot(a_ref[...], w_scr[...], preferred_element_type=F32)
    o_ref[...] = _sigmoid(acc).astype(o_ref.dtype)


def _gates(a, w_in, *, layer, bm=512, bn=512):
    m, k = a.shape
    bm = min(bm, m)
    n = N_BRANCH * D_MODEL
    main0 = N_MIX // bn
    extra0 = N_MIX // LANES
    return pl.pallas_call(
        functools.partial(_gates_kernel, shift=GATE_SHIFT),
        grid=(n // bn, m // bm),
        in_specs=[pl.BlockSpec((bm, k), lambda j, i: (i, 0)),
                  pl.BlockSpec((None, k, bn), lambda j, i: (layer, 0, main0 + j)),
                  pl.BlockSpec((None, k, LANES), lambda j, i: (layer, 0, extra0 + (j + 1) * (bn // LANES)))],
        out_specs=pl.BlockSpec((bm, bn), lambda j, i: (i, j)),
        out_shape=jax.ShapeDtypeStruct((m, n), BF16),
        scratch_shapes=[pltpu.VMEM((k, bn), BF16)],
        compiler_params=_params(2),
        name="in_proj_gates",
    )(a, w_in, w_in)


def _conv_a_kernel(b_ref, c_ref, x_ref, ch_ref, xh_ref, w_ref, o_ref, p_scr, *, blocks_per_seq):
    i = pl.program_id(0)
    tm = b_ref.shape[0]
    h = BF16_SUBLANES
    p = c_ref[...].astype(F32) * x_ref[...].astype(F32)
    ph = ch_ref[...].astype(F32) * xh_ref[...].astype(F32)
    ph = jnp.where(i % blocks_per_seq == 0, 0.0, ph)
    p_scr[0:h, :] = ph
    p_scr[h:h + tm, :] = p
    w = w_ref[...]
    y = w[2:3, :] * p + w[1:2, :] * p_scr[h - 1:h - 1 + tm, :] + w[0:1, :] * p_scr[h - 2:h - 2 + tm, :]
    o_ref[...] = (b_ref[...].astype(F32) * y).astype(o_ref.dtype)


def _mixer_a(proj, conv_w, seq, tm=512):
    m = proj.shape[0]
    tm = min(tm, seq)
    h = BF16_SUBLANES
    halo = lambda col: pl.BlockSpec(
        (h, W_MIX), lambda i: (jnp.maximum(i * (tm // h) - 1, 0), col))
    main = lambda col: pl.BlockSpec((tm, W_MIX), lambda i: (i, col))
    return pl.pallas_call(
        functools.partial(_conv_a_kernel, blocks_per_seq=seq // tm),
        grid=(m // tm,),
        in_specs=[main(COL_A_B), main(COL_A_C), main(COL_A_X), halo(COL_A_C), halo(COL_A_X),
                  pl.BlockSpec((CONV_A, W_MIX), lambda i: (0, 0))],
        out_specs=pl.BlockSpec((tm, W_MIX), lambda i: (i, 0)),
        out_shape=jax.ShapeDtypeStruct((m, W_MIX), BF16),
        scratch_shapes=[pltpu.VMEM((tm + h, W_MIX), F32)],
        compiler_params=_params(1),
        name="mixer_a_conv",
    )(proj, proj, proj, proj, proj, conv_w)


def _sgu_kernel(u_ref, v_ref, ng_ref, w_ref, bexp_ref, o_ref):
    tm = u_ref.shape[0]
    vf = _gelu_tanh(v_ref[...].astype(F32))
    mu = jnp.mean(vf, axis=-1, keepdims=True)
    d = vf - mu
    var = jnp.mean(d * d, axis=-1, keepdims=True)
    vn = (d * lax.rsqrt(var + EPS) * ng_ref[...]).astype(BF16)
    t_idx = lax.broadcasted_iota(jnp.int32, (CHUNK, CHUNK), 0)
    s_idx = lax.broadcasted_iota(jnp.int32, (CHUNK, CHUNK), 1)
    causal = s_idx <= t_idx
    ws = [jnp.where(causal, w_ref[g], 0.0).astype(BF16) for g in range(SGU_GROUPS)]
    bexp = bexp_ref[...]
    for c in range(tm // CHUNK):
        rows = slice(c * CHUNK, (c + 1) * CHUNK)
        vn_c = vn[rows, :]
        mixed = jnp.concatenate(
            [jnp.dot(ws[g], vn_c[:, g * GROUP:(g + 1) * GROUP], preferred_element_type=F32)
             for g in range(SGU_GROUPS)], axis=1)
        u = _gelu_tanh(u_ref[rows, :].astype(F32))
        o_ref[rows, :] = (u * (mixed + bexp)).astype(o_ref.dtype)


def _mixer_b(proj, norm_g, w_s, b_s, tm=512):
    m = proj.shape[0]
    tm = min(tm, m)
    bexp = jnp.repeat(b_s.T, GROUP, axis=1)
    return pl.pallas_call(
        _sgu_kernel,
        grid=(m // tm,),
        in_specs=[pl.BlockSpec((tm, W_MIX), lambda i: (i, COL_B_U)),
                  pl.BlockSpec((tm, W_MIX), lambda i: (i, COL_B_V)),
                  pl.BlockSpec((1, W_MIX), lambda i: (0, 0)),
                  pl.BlockSpec((SGU_GROUPS, CHUNK, CHUNK), lambda i: (0, 0, 0)),
                  pl.BlockSpec((CHUNK, W_MIX), lambda i: (0, 0))],
        out_specs=pl.BlockSpec((tm, W_MIX), lambda i: (i, 0)),
        out_shape=jax.ShapeDtypeStruct((m, W_MIX), BF16),
        compiler_params=_params(1),
        name="mixer_b_sgu",
    )(proj, proj, norm_g.reshape(1, W_MIX), w_s, bexp)


def _qknorm_kernel(q_ref, k_ref, qg_ref, kg_ref, qo_ref, ko_ref):
    for src, g_ref, dst in ((q_ref, qg_ref, qo_ref), (k_ref, kg_ref, ko_ref)):
        g = g_ref[...]
        for h in range(SB_HEADS):
            cols = slice(h * SB_HEAD_DIM, (h + 1) * SB_HEAD_DIM)
            x = src[:, cols].astype(F32)
            ms = jnp.mean(x * x, axis=-1, keepdims=True)
            dst[:, cols] = (x * lax.rsqrt(ms + EPS) * g).astype(dst.dtype)


def _qknorm(proj, q_g, k_g, tm=512):
    m = proj.shape[0]
    tm = min(tm, m)
    spec = lambda col: pl.BlockSpec((tm, W_MIX), lambda i: (i, col))
    gspec = pl.BlockSpec((1, SB_HEAD_DIM), lambda i: (0, 0))
    return pl.pallas_call(
        _qknorm_kernel,
        grid=(m // tm,),
        in_specs=[spec(COL_C_Q), spec(COL_C_K), gspec, gspec],
        out_specs=[spec(0), spec(0)],
        out_shape=[jax.ShapeDtypeStruct((m, W_MIX), BF16)] * 2,
        compiler_params=_params(1),
        name="qk_norm",
    )(proj, proj, q_g.reshape(1, -1), k_g.reshape(1, -1))


def _attn_kernel(q_ref, k_ref, v_ref, o_ref, acc_scr, *, tq, sub):
    i = pl.program_id(2)
    nsb = tq // sub
    q = q_ref[...]
    jj = lax.broadcasted_iota(jnp.int32, (2 * sub, sub), 0)
    ss = lax.broadcasted_iota(jnp.int32, (2 * sub, sub), 1)
    jj = jnp.where(jj >= sub, jj - sub, jj)
    uneg = jnp.where(jj >= ss, -1.0, 0.0).astype(BF16)

    def tile(off, r, mask):
        kb = k_ref[pl.ds(off, tq), :]
        vb = v_ref[pl.ds(off, tq), :]
        z2 = lax.dot_general(q, kb, (((1,), (1,)), ((), ())), preferred_element_type=F32)
        e = jnp.exp2(-jnp.abs(z2))
        sp = jnp.maximum(z2, 0.0) + jnp.log(1.0 + e) * (1.0 / math.log(2.0))
        if mask is not None:
            sp = jnp.where(mask, sp, 0.0)
        hi = sp.astype(BF16)
        lo = (sp - hi.astype(F32)).astype(BF16)
        incl, tot = [], []
        for c in range(nsb):
            cs = slice(c * sub, (c + 1) * sub)
            incl.append(jnp.dot(jnp.concatenate([hi[:, cs], lo[:, cs]], axis=1), uneg,
                                preferred_element_type=F32))
            tot.append(jnp.sum(sp[:, cs], axis=1, keepdims=True))
        att = [None] * nsb
        for c in reversed(range(nsb)):
            cs = slice(c * sub, (c + 1) * sub)
            a = jnp.exp2(z2[:, cs] + incl[c] + r)
            if mask is not None:
                a = jnp.where(mask[:, cs], a, 0.0)
            att[c] = a.astype(BF16)
            r = r - tot[c]
        acc_scr[...] += jnp.dot(jnp.concatenate(att, axis=1), vb, preferred_element_type=F32)
        return r

    acc_scr[...] = jnp.zeros_like(acc_scr)
    row = lax.broadcasted_iota(jnp.int32, (tq, tq), 0)
    col = lax.broadcasted_iota(jnp.int32, (tq, tq), 1)
    r = tile(pl.multiple_of(i * tq, tq), jnp.zeros((tq, 1), F32), col < row)

    def body(step, r):
        return tile(pl.multiple_of((i - 1 - step) * tq, tq), r, None)

    lax.fori_loop(0, i, body, r)
    o_ref[...] = acc_scr[...].astype(o_ref.dtype)


def _attention(qn, kn, proj, batch, seq, tq=512, sub=128):
    m = qn.shape[0]
    tq = min(tq, seq)
    nq = seq // tq
    v_col0 = COL_C_V * (W_MIX // SB_HEAD_DIM)
    return pl.pallas_call(
        functools.partial(_attn_kernel, tq=tq, sub=sub),
        grid=(batch, SB_HEADS, nq),
        in_specs=[pl.BlockSpec((tq, SB_HEAD_DIM), lambda b, h, i: (b * nq + i, h)),
                  pl.BlockSpec((seq, SB_HEAD_DIM), lambda b, h, i: (b, h)),
                  pl.BlockSpec((seq, SB_HEAD_DIM), lambda b, h, i: (b, v_col0 + h))],
        out_specs=pl.BlockSpec((tq, SB_HEAD_DIM), lambda b, h, i: (b * nq + i, h)),
        out_shape=jax.ShapeDtypeStruct((m, W_MIX), BF16),
        scratch_shapes=[pltpu.VMEM((tq, SB_HEAD_DIM), F32)],
        compiler_params=_params(3),
        name="stick_breaking_attn",
    )(qn, kn, proj)


def _ssd_kernel(z_ref, xbc_ref, halo_ref, dt_ref, cw_ref, cb_ref, dtb_ref, alog_ref, dexp_ref,
                ng_ref, xmat_ref, o_ref, ht_scr, xp_scr):
    c = pl.program_id(1)
    L = CHUNK
    h = BF16_SUBLANES
    hi_prec = lax.Precision.HIGHEST
    gw = W_MIX // SSM_GROUPS

    @pl.when(c == 0)
    def _():
        ht_scr[...] = jnp.zeros_like(ht_scr)

    xp_scr[0:h, :] = jnp.where(c == 0, 0.0, halo_ref[...].astype(F32))
    xp_scr[h:h + L, :] = xbc_ref[...].astype(F32)
    cw = cw_ref[...]
    conv = cb_ref[...]
    for k in range(SSM_CONV):
        s0 = h - (SSM_CONV - 1) + k
        conv = conv + cw[k:k + 1, :] * xp_scr[s0:s0 + L, :]
    xc = conv * _sigmoid(conv)
    xs = xc[:, :W_MIX]
    bmat = xc[:, W_MIX:W_MIX + SSM_GROUPS * SSM_STATE].astype(BF16)
    cmat = xc[:, W_MIX + SSM_GROUPS * SSM_STATE:].astype(BF16)

    dt = _softplus(dt_ref[...] + dtb_ref[...])
    da = dt * (-jnp.exp(alog_ref[...]))
    t_i = lax.broadcasted_iota(jnp.int32, (L, L), 0)
    s_i = lax.broadcasted_iota(jnp.int32, (L, L), 1)
    causal = s_i <= t_i
    cum = jnp.dot(causal.astype(F32), da, precision=hi_prec, preferred_element_type=F32)
    cum_t = cum.T
    xmat = xmat_ref[...]
    dt_e = jnp.dot(dt, xmat, precision=hi_prec, preferred_element_type=F32)
    cum_e = jnp.dot(cum, xmat, precision=hi_prec, preferred_element_type=F32)
    cl_e = cum_e[L - 1:L, :]
    xdt = xs * dt_e
    xdec = (xdt * jnp.exp(cl_e - cum_e)).astype(BF16)
    ecum_e = jnp.exp(cum_e)
    chunk_dec = jnp.exp(cl_e)
    lane = lax.broadcasted_iota(jnp.int32, (L, LANES), 1)
    left = lane < SSM_HEAD_DIM

    ys = []
    for g in range(SSM_GROUPS):
        bg = bmat[:, g * SSM_STATE:(g + 1) * SSM_STATE]
        cg = cmat[:, g * SSM_STATE:(g + 1) * SSM_STATE]
        gcols = slice(g * gw, (g + 1) * gw)
        cb = lax.dot_general(cg, bg, (((1,), (1,)), ((), ())), preferred_element_type=F32)
        ht = ht_scr[g]
        y_off = jnp.dot(cg, ht.astype(BF16), preferred_element_type=F32)
        parts = []
        for pr in range(SSM_HPG // 2):
            ms = []
            for r in range(2):
                idx = g * SSM_HPG + 2 * pr + r
                seg = cum[:, idx:idx + 1] - cum_t[idx:idx + 1, :]
                dec = jnp.exp(jnp.where(causal, seg, -jnp.inf))
                ms.append((cb * dec).astype(BF16))
            c0 = g * gw + pr * LANES
            xpair = xdt[:, c0:c0 + LANES]
            rhs = jnp.concatenate(
                [jnp.where(left, xpair, 0.0), jnp.where(left, 0.0, xpair)], axis=0).astype(BF16)
            parts.append(jnp.dot(jnp.concatenate(ms, axis=1), rhs, preferred_element_type=F32))
        y_diag = jnp.concatenate(parts, axis=1)
        ys.append(y_diag + y_off * ecum_e[:, gcols])
        st = lax.dot_general(bg, xdec[:, gcols], (((0,), (0,)), ((), ())),
                             preferred_element_type=F32)
        ht_scr[g] = ht * chunk_dec[:, gcols] + st

    y = jnp.concatenate(ys, axis=1) + xs * dexp_ref[...]
    z = z_ref[...].astype(F32)
    y = y * (z * _sigmoid(z))
    outs = []
    for g in range(SSM_GROUPS):
        yg = y[:, g * gw:(g + 1) * gw]
        ms = jnp.mean(yg * yg, axis=-1, keepdims=True)
        outs.append(yg * lax.rsqrt(ms + EPS))
    o_ref[...] = (jnp.concatenate(outs, axis=1) * ng_ref[...]).astype(o_ref.dtype)


def _mixer_d(proj, dt_raw, conv_w, conv_b, dt_bias, a_log, d_skip, norm_g, batch, seq):
    m = proj.shape[0]
    nc = seq // CHUNK
    h = BF16_SUBLANES
    pad = LANES - SSM_HEADS
    row = lambda v: v.reshape(1, -1)
    xmat = (jnp.arange(W_MIX)[None, :] // SSM_HEAD_DIM == jnp.arange(LANES)[:, None]).astype(F32)
    const = lambda shape: pl.BlockSpec(shape, lambda b, c: (0,) * len(shape))
    return pl.pallas_call(
        _ssd_kernel,
        grid=(batch, nc),
        in_specs=[pl.BlockSpec((CHUNK, W_MIX), lambda b, c: (b * nc + c, COL_D_Z)),
                  pl.BlockSpec((CHUNK, SSM_CONV_DIM), lambda b, c: (b * nc + c, COL_D_XBC)),
                  pl.BlockSpec((h, SSM_CONV_DIM),
                               lambda b, c: (jnp.maximum((b * nc + c) * (CHUNK // h) - 1, 0), COL_D_XBC)),
                  pl.BlockSpec((CHUNK, LANES), lambda b, c: (b * nc + c, 0)),
                  const((SSM_CONV, SSM_CONV_DIM)), const((1, SSM_CONV_DIM)),
                  const((1, LANES)), const((1, LANES)), const((1, W_MIX)), const((1, W_MIX)),
                  const((LANES, W_MIX))],
        out_specs=pl.BlockSpec((CHUNK, W_MIX), lambda b, c: (b * nc + c, 0)),
        out_shape=jax.ShapeDtypeStruct((m, W_MIX), BF16),
        scratch_shapes=[pltpu.VMEM((SSM_GROUPS, SSM_STATE, W_MIX // SSM_GROUPS), F32),
                        pltpu.VMEM((CHUNK + h, SSM_CONV_DIM), F32)],
        compiler_params=_params(2),
        name="mixer_d_ssd",
    )(proj, proj, proj, dt_raw, conv_w, row(conv_b),
      row(jnp.pad(dt_bias, (0, pad))), row(jnp.pad(a_log, (0, pad))),
      row(jnp.repeat(d_skip, SSM_HEAD_DIM)), row(norm_g), xmat)


def _merge_kernel(ya_ref, yb_ref, yc_ref, yd_ref, wb_ref, g0_ref, g1_ref, g2_ref, g3_ref, o_ref):
    acc = None
    for j, (y_ref, g_ref) in enumerate(
            zip((ya_ref, yb_ref, yc_ref, yd_ref), (g0_ref, g1_ref, g2_ref, g3_ref))):
        t = jnp.dot(y_ref[...], wb_ref[j], preferred_element_type=F32) * g_ref[...].astype(F32)
        acc = t if acc is None else acc + t
    o_ref[...] = acc.astype(o_ref.dtype)


def _merge(ys, w_branch, gates, *, layer, bm=512, bn=1024):
    m = gates.shape[0]
    bm = min(bm, m)
    nb = D_MODEL // bn
    yspec = pl.BlockSpec((bm, W_MIX), lambda i, n: (i, 0))
    gspec = lambda j: pl.BlockSpec((bm, bn), lambda i, n: (i, j * nb + n))
    return pl.pallas_call(
        _merge_kernel,
        grid=(m // bm, nb),
        in_specs=[yspec] * N_BRANCH
        + [pl.BlockSpec((None, N_BRANCH, W_MIX, bn), lambda i, n: (layer, 0, 0, n))]
        + [gspec(j) for j in range(N_BRANCH)],
        out_specs=pl.BlockSpec((bm, bn), lambda i, n: (i, n)),
        out_shape=jax.ShapeDtypeStruct((m, D_MODEL), BF16),
        compiler_params=_params(2),
        name="branch_merge",
    )(*ys, w_branch, gates, gates, gates, gates)


def _ffn_gu_kernel(h_ref, wg_ref, wu_ref, o_ref, wg_scr, wu_scr):
    @pl.when(pl.program_id(1) == 0)
    def _():
        _cast_weight_block(wg_scr, wg_ref)
        _cast_weight_block(wu_scr, wu_ref)

    h = h_ref[...]
    g = jnp.dot(h, wg_scr[...], preferred_element_type=F32)
    u = jnp.dot(h, wu_scr[...], preferred_element_type=F32)
    o_ref[...] = (g * _sigmoid(g) * u).astype(o_ref.dtype)


def _ffn_gate_up(h, wg, wu, *, layer, bm=1024, bn=256):
    m, k = h.shape
    f = wg.shape[2]
    bm = min(bm, m)
    wspec = pl.BlockSpec((None, k, bn), lambda j, i: (layer, 0, j))
    return pl.pallas_call(
        _ffn_gu_kernel,
        grid=(f // bn, m // bm),
        in_specs=[pl.BlockSpec((bm, k), lambda j, i: (i, 0)), wspec, wspec],
        out_specs=pl.BlockSpec((bm, bn), lambda j, i: (i, j)),
        out_shape=jax.ShapeDtypeStruct((m, f), BF16),
        scratch_shapes=[pltpu.VMEM((k, bn), BF16)] * 2,
        compiler_params=_params(2),
        name="ffn_gate_up",
    )(h, wg, wu)


def kernel(x, norm_mix, w_in, conv_a, sgu_norm, sgu_w, sgu_b, q_norm, k_norm, ssm_conv_w, ssm_conv_b, ssm_dt_bias, ssm_a_log, ssm_d, ssm_norm, w_branch, w_out, norm_ffn, w_ffn_gate, w_ffn_up, w_ffn_down):
    batch, seq, d = x.shape
    m = batch * seq
    depth = w_in.shape[0]
    xr = x.reshape(m, d)
    q_fold = (1.0 / math.sqrt(SB_HEAD_DIM)) * (1.0 / math.log(2.0))
    w_dt = jnp.pad(w_in[:, :, N_MIX:N_MIX + SSM_HEADS],
                   ((0, 0), (0, 0), (0, LANES - SSM_HEADS))).astype(BF16)
    w_branch_bf = w_branch.astype(BF16)
    w_out_bf = w_out.astype(BF16)
    w_down_bf = w_ffn_down.astype(BF16)
    for l in range(depth):
        hn = _rmsnorm(xr, norm_mix[l])
        proj = _matmul(hn, w_in, layer=l, n=N_MIX, bm=512, bn=512, out_dtype=BF16, name="in_proj_mix")
        dt_raw = _matmul(hn, w_dt, layer=l, n=LANES, bm=1024, bn=LANES, out_dtype=F32, name="in_proj_dt")
        gates = _gates(hn, w_in, layer=l)

        y_a = _mixer_a(proj, conv_a[l], seq)
        y_b = _mixer_b(proj, sgu_norm[l], sgu_w[l], sgu_b[l])
        qn, kn = _qknorm(proj, q_norm[l] * q_fold, k_norm[l])
        y_c = _attention(qn, kn, proj, batch, seq)
        y_d = _mixer_d(proj, dt_raw, ssm_conv_w[l], ssm_conv_b[l], ssm_dt_bias[l], ssm_a_log[l],
                       ssm_d[l], ssm_norm[l], batch, seq)

        merged = _merge((y_a, y_b, y_c, y_d), w_branch_bf, gates, layer=l)
        xr = _matmul(merged, w_out_bf, layer=l, n=d, bm=512, bn=1024, out_dtype=F32, res=xr,
                     name="out_proj")

        hn = _rmsnorm(xr, norm_ffn[l])
        act = _ffn_gate_up(hn, w_ffn_gate, w_ffn_up, layer=l)
        xr = _matmul(act, w_down_bf, layer=l, n=d, bm=512, bn=256, out_dtype=F32, res=xr,
                     m_outer=True, name="ffn_down")
    return xr.reshape(batch, seq, d)
```

```python
import functools
import math

import jax
import jax.numpy as jnp
from jax import lax
from jax.experimental import pallas as pl
from jax.experimental.pallas import tpu as pltpu

F32 = jnp.float32
BF16 = jnp.bfloat16

EPS = 1e-6
D_MODEL = 4096
W_MIX = D_MODEL // 4
GROUP = 128
CHUNK = 128
CONV_A = 3
SGU_GROUPS = W_MIX // GROUP
SB_HEAD_DIM = 128
SB_HEADS = W_MIX // SB_HEAD_DIM
SSM_HEAD_DIM = 64
SSM_HEADS = W_MIX // SSM_HEAD_DIM
SSM_GROUPS = 2
SSM_HPG = SSM_HEADS // SSM_GROUPS
SSM_STATE = 128
SSM_CONV = 4
SSM_CONV_DIM = W_MIX + 2 * SSM_GROUPS * SSM_STATE
N_BRANCH = 4
N_MIX = 9 * W_MIX + SSM_CONV_DIM
GATE_SHIFT = SSM_HEADS
LANES = 128
BF16_SUBLANES = 16
VMEM_LIMIT = 52 * 1024 * 1024

COL_A_B, COL_A_C, COL_A_X = 0, 1, 2
COL_B_U, COL_B_V = 3, 4
COL_C_Q, COL_C_K, COL_C_V = 5, 6, 7
COL_D_Z = 8
COL_D_XBC = (9 * W_MIX) // SSM_CONV_DIM


def _params(n_axes):
    return pltpu.CompilerParams(
        dimension_semantics=("arbitrary",) * n_axes, vmem_limit_bytes=VMEM_LIMIT)


def _sigmoid(x):
    return 1.0 / (1.0 + jnp.exp(-x))


def _softplus(x):
    return jnp.maximum(x, 0.0) + jnp.log1p(jnp.exp(-jnp.abs(x)))


def _gelu_tanh(x):
    c = math.sqrt(2.0 / math.pi)
    return x * (0.5 * (1.0 + jnp.tanh(c * (x + 0.044715 * (x * x * x)))))


def _rmsnorm_kernel(x_ref, g_ref, o_ref):
    x = x_ref[...]
    ms = jnp.mean(x * x, axis=-1, keepdims=True)
    o_ref[...] = (x * lax.rsqrt(ms + EPS) * g_ref[...]).astype(o_ref.dtype)


def _rmsnorm(x, g, tm=256):
    m, d = x.shape
    return pl.pallas_call(
        _rmsnorm_kernel,
        grid=(m // tm,),
        in_specs=[pl.BlockSpec((tm, d), lambda i: (i, 0)),
                  pl.BlockSpec((1, d), lambda i: (0, 0))],
        out_specs=pl.BlockSpec((tm, d), lambda i: (i, 0)),
        out_shape=jax.ShapeDtypeStruct((m, d), BF16),
        compiler_params=_params(1),
        name="rmsnorm",
    )(x, g.reshape(1, d))


CAST_ROWS = 512


def _cast_weight_block(dst_scr, src_ref):
    k = dst_scr.shape[0]
    rows_per_step = min(CAST_ROWS, k)

    def body(s, carry):
        rows = pl.ds(pl.multiple_of(s * rows_per_step, rows_per_step), rows_per_step)
        dst_scr[rows, :] = src_ref[rows, :].astype(dst_scr.dtype)
        return carry

    lax.fori_loop(0, k // rows_per_step, body, 0)


def _mm_kernel(*refs, has_res):
    a_ref, w_ref, o_ref = refs[0], refs[1], refs[-1]
    acc = jnp.dot(a_ref[...], w_ref[...], preferred_element_type=F32)
    if has_res:
        acc = acc + refs[2][...]
    o_ref[...] = acc.astype(o_ref.dtype)


def _matmul(a, w, *, layer, bm, bn, out_dtype, res=None, m_outer=False, name="matmul"):
    m, k = a.shape
    n = w.shape[2]
    bm = min(bm, m)
    if m_outer:
        grid = (m // bm, n // bn)
        ij = lambda i, j: (i, j)
    else:
        grid = (n // bn, m // bm)
        ij = lambda j, i: (i, j)
    in_specs = [pl.BlockSpec((bm, k), lambda *g: (ij(*g)[0], 0)),
                pl.BlockSpec((None, k, bn), lambda *g: (layer, 0, ij(*g)[1]))]
    args = [a, w]
    if res is not None:
        in_specs.append(pl.BlockSpec((bm, bn), ij))
        args.append(res)
    return pl.pallas_call(
        functools.partial(_mm_kernel, has_res=res is not None),
        grid=grid,
        in_specs=in_specs,
        out_specs=pl.BlockSpec((bm, bn), ij),
        out_shape=jax.ShapeDtypeStruct((m, n), out_dtype),
        compiler_params=_params(2),
        name=name,
    )(*args)


CAST_ROWS_T = 128


def _in_proj_kernel(*refs, shift, act, valid_cols):
    a_ref, wm_ref = refs[0], refs[1]
    o_ref, w_scr = refs[-2], refs[-1]
    rows = w_scr.shape[0]

    @pl.when(pl.program_id(1) == 0)
    def _():
        step = min(CAST_ROWS_T, rows)
        for c in range(rows // step):
            lo = c * step + shift
            if lo + step <= rows:
                src = wm_ref[lo:lo + step, :]
            else:
                src = jnp.concatenate([wm_ref[lo:rows, :], refs[2][0:lo + step - rows, :]], axis=0)
            w_scr[c * step:(c + 1) * step, :] = src.astype(w_scr.dtype)

    acc = lax.dot_general(a_ref[...], w_scr[...], (((1,), (1,)), ((), ())),
                          preferred_element_type=F32)
    if act == "sigmoid":
        acc = _sigmoid(acc)
    if valid_cols is not None:
        lane = lax.broadcasted_iota(jnp.int32, acc.shape, 1)
        acc = jnp.where(lane < valid_cols, acc, 0.0)
    o_ref[...] = acc.astype(o_ref.dtype)


def _in_proj(a, w_t, *, layer, row0, shift, n, bm, bn, out_dtype, act=None, valid_cols=None, name):
    m, k = a.shape
    bm = min(bm, m)
    in_specs = [pl.BlockSpec((bm, k), lambda j, i: (i, 0)),
                pl.BlockSpec((None, bn, k), lambda j, i: (layer, row0 // bn + j, 0))]
    args = [a, w_t]
    if shift:
        in_specs.append(pl.BlockSpec(
            (None, shift, k), lambda j, i: (layer, (row0 + (j + 1) * bn) // shift, 0)))
        args.append(w_t)
    return pl.pallas_call(
        functools.partial(_in_proj_kernel, shift=shift, act=act, valid_cols=valid_cols),
        grid=(n // bn, m // bm),
        in_specs=in_specs,
        out_specs=pl.BlockSpec((bm, bn), lambda j, i: (i, j)),
        out_shape=jax.ShapeDtypeStruct((m, n), out_dtype),
        scratch_shapes=[pltpu.VMEM((bn, k), BF16)],
        compiler_params=_params(2),
        name=name,
    )(*args)


def _conv_a_kernel(b_ref, c_ref, x_ref, ch_ref, xh_ref, w_ref, o_ref, p_scr, *, blocks_per_seq):
    i = pl.program_id(0)
    tm = b_ref.shape[0]
    h = BF16_SUBLANES
    p = c_ref[...].astype(F32) * x_ref[...].astype(F32)
    ph = ch_ref[...].astype(F32) * xh_ref[...].astype(F32)
    ph = jnp.where(i % blocks_per_seq == 0, 0.0, ph)
    p_scr[0:h, :] = ph
    p_scr[h:h + tm, :] = p
    w = w_ref[...]
    y = w[2:3, :] * p + w[1:2, :] * p_scr[h - 1:h - 1 + tm, :] + w[0:1, :] * p_scr[h - 2:h - 2 + tm, :]
    o_ref[...] = (b_ref[...].astype(F32) * y).astype(o_ref.dtype)


def _mixer_a(proj, conv_w, seq, tm=512):
    m = proj.shape[0]
    tm = min(tm, seq)
    h = BF16_SUBLANES
    halo = lambda col: pl.BlockSpec(
        (h, W_MIX), lambda i: (jnp.maximum(i * (tm // h) - 1, 0), col))
    main = lambda col: pl.BlockSpec((tm, W_MIX), lambda i: (i, col))
    return pl.pallas_call(
        functools.partial(_conv_a_kernel, blocks_per_seq=seq // tm),
        grid=(m // tm,),
        in_specs=[main(COL_A_B), main(COL_A_C), main(COL_A_X), halo(COL_A_C), halo(COL_A_X),
                  pl.BlockSpec((CONV_A, W_MIX), lambda i: (0, 0))],
        out_specs=pl.BlockSpec((tm, W_MIX), lambda i: (i, 0)),
        out_shape=jax.ShapeDtypeStruct((m, W_MIX), BF16),
        scratch_shapes=[pltpu.VMEM((tm + h, W_MIX), F32)],
        compiler_params=_params(1),
        name="mixer_a_conv",
    )(proj, proj, proj, proj, proj, conv_w)


def _sgu_kernel(u_ref, v_ref, ng_ref, w_ref, bexp_ref, o_ref):
    tm = u_ref.shape[0]
    vf = _gelu_tanh(v_ref[...].astype(F32))
    mu = jnp.mean(vf, axis=-1, keepdims=True)
    d = vf - mu
    var = jnp.mean(d * d, axis=-1, keepdims=True)
    vn = (d * lax.rsqrt(var + EPS) * ng_ref[...]).astype(BF16)
    t_idx = lax.broadcasted_iota(jnp.int32, (CHUNK, CHUNK), 0)
    s_idx = lax.broadcasted_iota(jnp.int32, (CHUNK, CHUNK), 1)
    causal = s_idx <= t_idx
    ws = [jnp.where(causal, w_ref[g], 0.0).astype(BF16) for g in range(SGU_GROUPS)]
    bexp = bexp_ref[...]
    for c in range(tm // CHUNK):
        rows = slice(c * CHUNK, (c + 1) * CHUNK)
        vn_c = vn[rows, :]
        mixed = jnp.concatenate(
            [jnp.dot(ws[g], vn_c[:, g * GROUP:(g + 1) * GROUP], preferred_element_type=F32)
             for g in range(SGU_GROUPS)], axis=1)
        u = _gelu_tanh(u_ref[rows, :].astype(F32))
        o_ref[rows, :] = (u * (mixed + bexp)).astype(o_ref.dtype)


def _mixer_b(proj, norm_g, w_s, b_s, tm=512):
    m = proj.shape[0]
    tm = min(tm, m)
    bexp = jnp.repeat(b_s.T, GROUP, axis=1)
    return pl.pallas_call(
        _sgu_kernel,
        grid=(m // tm,),
        in_specs=[pl.BlockSpec((tm, W_MIX), lambda i: (i, COL_B_U)),
                  pl.BlockSpec((tm, W_MIX), lambda i: (i, COL_B_V)),
                  pl.BlockSpec((1, W_MIX), lambda i: (0, 0)),
                  pl.BlockSpec((SGU_GROUPS, CHUNK, CHUNK), lambda i: (0, 0, 0)),
                  pl.BlockSpec((CHUNK, W_MIX), lambda i: (0, 0))],
        out_specs=pl.BlockSpec((tm, W_MIX), lambda i: (i, 0)),
        out_shape=jax.ShapeDtypeStruct((m, W_MIX), BF16),
        compiler_params=_params(1),
        name="mixer_b_sgu",
    )(proj, proj, norm_g.reshape(1, W_MIX), w_s, bexp)


def _qknorm_kernel(q_ref, k_ref, qg_ref, kg_ref, qo_ref, ko_ref):
    for src, g_ref, dst in ((q_ref, qg_ref, qo_ref), (k_ref, kg_ref, ko_ref)):
        g = g_ref[...]
        for h in range(SB_HEADS):
            cols = slice(h * SB_HEAD_DIM, (h + 1) * SB_HEAD_DIM)
            x = src[:, cols].astype(F32)
            ms = jnp.mean(x * x, axis=-1, keepdims=True)
            dst[:, cols] = (x * lax.rsqrt(ms + EPS) * g).astype(dst.dtype)


def _qknorm(proj, q_g, k_g, tm=512):
    m = proj.shape[0]
    tm = min(tm, m)
    spec = lambda col: pl.BlockSpec((tm, W_MIX), lambda i: (i, col))
    gspec = pl.BlockSpec((1, SB_HEAD_DIM), lambda i: (0, 0))
    return pl.pallas_call(
        _qknorm_kernel,
        grid=(m // tm,),
        in_specs=[spec(COL_C_Q), spec(COL_C_K), gspec, gspec],
        out_specs=[spec(0), spec(0)],
        out_shape=[jax.ShapeDtypeStruct((m, W_MIX), BF16)] * 2,
        compiler_params=_params(1),
        name="qk_norm",
    )(proj, proj, q_g.reshape(1, -1), k_g.reshape(1, -1))


def _attn_kernel(q_ref, k_ref, v_ref, o_ref, acc_scr, *, tq, sub):
    i = pl.program_id(2)
    nsb = tq // sub
    q = q_ref[...]
    jj = lax.broadcasted_iota(jnp.int32, (2 * sub, sub), 0)
    ss = lax.broadcasted_iota(jnp.int32, (2 * sub, sub), 1)
    jj = jnp.where(jj >= sub, jj - sub, jj)
    uneg = jnp.where(jj >= ss, -1.0, 0.0).astype(BF16)

    def tile(off, r, mask):
        kb = k_ref[pl.ds(off, tq), :]
        vb = v_ref[pl.ds(off, tq), :]
        z2 = lax.dot_general(q, kb, (((1,), (1,)), ((), ())), preferred_element_type=F32)
        neg_abs = pltpu.bitcast(pltpu.bitcast(z2, jnp.int32) | jnp.int32(-2 ** 31), F32)
        e = jnp.exp2(neg_abs)
        sp = jnp.maximum(z2, 0.0) + jnp.log(1.0 + e) * (1.0 / math.log(2.0))
        if mask is not None:
            sp = jnp.where(mask, sp, 0.0)
        hi = sp.astype(BF16)
        lo = (sp - hi.astype(F32)).astype(BF16)
        incl, tot = [], []
        for c in range(nsb):
            cs = slice(c * sub, (c + 1) * sub)
            incl.append(jnp.dot(jnp.concatenate([hi[:, cs], lo[:, cs]], axis=1), uneg,
                                preferred_element_type=F32))
            tot.append(jnp.sum(sp[:, cs], axis=1, keepdims=True))
        att = [None] * nsb
        for c in reversed(range(nsb)):
            cs = slice(c * sub, (c + 1) * sub)
            a = jnp.exp2(z2[:, cs] + incl[c] + r)
            if mask is not None:
                a = jnp.where(mask[:, cs], a, 0.0)
            att[c] = a.astype(BF16)
            r = r - tot[c]
        acc_scr[...] += jnp.dot(jnp.concatenate(att, axis=1), vb, preferred_element_type=F32)
        return r

    acc_scr[...] = jnp.zeros_like(acc_scr)
    row = lax.broadcasted_iota(jnp.int32, (tq, tq), 0)
    col = lax.broadcasted_iota(jnp.int32, (tq, tq), 1)
    r = tile(pl.multiple_of(i * tq, tq), jnp.zeros((tq, 1), F32), col < row)

    def body(step, r):
        return tile(pl.multiple_of((i - 1 - step) * tq, tq), r, None)

    lax.fori_loop(0, i, body, r)
    o_ref[...] = acc_scr[...].astype(o_ref.dtype)


def _attention(qn, kn, proj, batch, seq, tq=512, sub=128):
    m = qn.shape[0]
    tq = min(tq, seq)
    nq = seq // tq
    v_col0 = COL_C_V * (W_MIX // SB_HEAD_DIM)
    return pl.pallas_call(
        functools.partial(_attn_kernel, tq=tq, sub=sub),
        grid=(batch, SB_HEADS, nq),
        in_specs=[pl.BlockSpec((tq, SB_HEAD_DIM), lambda b, h, i: (b * nq + i, h)),
                  pl.BlockSpec((seq, SB_HEAD_DIM), lambda b, h, i: (b, h)),
                  pl.BlockSpec((seq, SB_HEAD_DIM), lambda b, h, i: (b, v_col0 + h))],
        out_specs=pl.BlockSpec((tq, SB_HEAD_DIM), lambda b, h, i: (b * nq + i, h)),
        out_shape=jax.ShapeDtypeStruct((m, W_MIX), BF16),
        scratch_shapes=[pltpu.VMEM((tq, SB_HEAD_DIM), F32)],
        compiler_params=_params(3),
        name="stick_breaking_attn",
    )(qn, kn, proj)


def _ssd_kernel(z_ref, xbc_ref, halo_ref, dt_ref, cw_ref, cb_ref, dtb_ref, alog_ref, dexp_ref,
                ng_ref, xmat_ref, o_ref, ht_scr, xp_scr):
    c = pl.program_id(1)
    L = CHUNK
    h = BF16_SUBLANES
    hi_prec = lax.Precision.HIGHEST
    gw = W_MIX // SSM_GROUPS

    @pl.when(c == 0)
    def _():
        ht_scr[...] = jnp.zeros_like(ht_scr)

    xp_scr[0:h, :] = jnp.where(c == 0, 0.0, halo_ref[...].astype(F32))
    xp_scr[h:h + L, :] = xbc_ref[...].astype(F32)
    cw = cw_ref[...]
    conv = cb_ref[...]
    for k in range(SSM_CONV):
        s0 = h - (SSM_CONV - 1) + k
        conv = conv + cw[k:k + 1, :] * xp_scr[s0:s0 + L, :]
    xc = conv * _sigmoid(conv)
    xs = xc[:, :W_MIX]
    bmat = xc[:, W_MIX:W_MIX + SSM_GROUPS * SSM_STATE].astype(BF16)
    cmat = xc[:, W_MIX + SSM_GROUPS * SSM_STATE:].astype(BF16)

    dt = _softplus(dt_ref[...] + dtb_ref[...])
    da = dt * (-jnp.exp(alog_ref[...]))
    t_i = lax.broadcasted_iota(jnp.int32, (L, L), 0)
    s_i = lax.broadcasted_iota(jnp.int32, (L, L), 1)
    causal = s_i <= t_i
    cum = jnp.dot(causal.astype(F32), da, precision=hi_prec, preferred_element_type=F32)
    cum_t = cum.T
    xmat = xmat_ref[...]
    dt_e = jnp.dot(dt, xmat, precision=hi_prec, preferred_element_type=F32)
    cum_e = jnp.dot(cum, xmat, precision=hi_prec, preferred_element_type=F32)
    cl_e = cum_e[L - 1:L, :]
    xdt = xs * dt_e
    xdec = (xdt * jnp.exp(cl_e - cum_e)).astype(BF16)
    ecum_e = jnp.exp(cum_e)
    chunk_dec = jnp.exp(cl_e)
    lane = lax.broadcasted_iota(jnp.int32, (L, LANES), 1)
    left = lane < SSM_HEAD_DIM

    ys = []
    for g in range(SSM_GROUPS):
        bg = bmat[:, g * SSM_STATE:(g + 1) * SSM_STATE]
        cg = cmat[:, g * SSM_STATE:(g + 1) * SSM_STATE]
        gcols = slice(g * gw, (g + 1) * gw)
        cb = lax.dot_general(cg, bg, (((1,), (1,)), ((), ())), preferred_element_type=F32)
        ht = ht_scr[g]
        y_off = jnp.dot(cg, ht.astype(BF16), preferred_element_type=F32)
        parts = []
        for pr in range(SSM_HPG // 2):
            ms = []
            for r in range(2):
                idx = g * SSM_HPG + 2 * pr + r
                seg = cum[:, idx:idx + 1] - cum_t[idx:idx + 1, :]
                dec = jnp.exp(jnp.where(causal, seg, -jnp.inf))
                ms.append((cb * dec).astype(BF16))
            c0 = g * gw + pr * LANES
            xpair = xdt[:, c0:c0 + LANES]
            rhs = jnp.concatenate(
                [jnp.where(left, xpair, 0.0), jnp.where(left, 0.0, xpair)], axis=0).astype(BF16)
            parts.append(jnp.dot(jnp.concatenate(ms, axis=1), rhs, preferred_element_type=F32))
        y_diag = jnp.concatenate(parts, axis=1)
        ys.append(y_diag + y_off * ecum_e[:, gcols])
        st = lax.dot_general(bg, xdec[:, gcols], (((0,), (0,)), ((), ())),
                             preferred_element_type=F32)
        ht_scr[g] = ht * chunk_dec[:, gcols] + st

    y = jnp.concatenate(ys, axis=1) + xs * dexp_ref[...]
    z = z_ref[...].astype(F32)
    y = y * (z * _sigmoid(z))
    outs = []
    for g in range(SSM_GROUPS):
        yg = y[:, g * gw:(g + 1) * gw]
        ms = jnp.mean(yg * yg, axis=-1, keepdims=True)
        outs.append(yg * lax.rsqrt(ms + EPS))
    o_ref[...] = (jnp.concatenate(outs, axis=1) * ng_ref[...]).astype(o_ref.dtype)


def _mixer_d(proj, dt_raw, conv_w, conv_b, dt_bias, a_log, d_skip, norm_g, batch, seq):
    m = proj.shape[0]
    nc = seq // CHUNK
    h = BF16_SUBLANES
    pad = LANES - SSM_HEADS
    row = lambda v: v.reshape(1, -1)
    xmat = (jnp.arange(W_MIX)[None, :] // SSM_HEAD_DIM == jnp.arange(LANES)[:, None]).astype(F32)
    const = lambda shape: pl.BlockSpec(shape, lambda b, c: (0,) * len(shape))
    return pl.pallas_call(
        _ssd_kernel,
        grid=(batch, nc),
        in_specs=[pl.BlockSpec((CHUNK, W_MIX), lambda b, c: (b * nc + c, COL_D_Z)),
                  pl.BlockSpec((CHUNK, SSM_CONV_DIM), lambda b, c: (b * nc + c, COL_D_XBC)),
                  pl.BlockSpec((h, SSM_CONV_DIM),
                               lambda b, c: (jnp.maximum((b * nc + c) * (CHUNK // h) - 1, 0), COL_D_XBC)),
                  pl.BlockSpec((CHUNK, LANES), lambda b, c: (b * nc + c, 0)),
                  const((SSM_CONV, SSM_CONV_DIM)), const((1, SSM_CONV_DIM)),
                  const((1, LANES)), const((1, LANES)), const((1, W_MIX)), const((1, W_MIX)),
                  const((LANES, W_MIX))],
        out_specs=pl.BlockSpec((CHUNK, W_MIX), lambda b, c: (b * nc + c, 0)),
        out_shape=jax.ShapeDtypeStruct((m, W_MIX), BF16),
        scratch_shapes=[pltpu.VMEM((SSM_GROUPS, SSM_STATE, W_MIX // SSM_GROUPS), F32),
                        pltpu.VMEM((CHUNK + h, SSM_CONV_DIM), F32)],
        compiler_params=_params(2),
        name="mixer_d_ssd",
    )(proj, proj, proj, dt_raw, conv_w, row(conv_b),
      row(jnp.pad(dt_bias, (0, pad))), row(jnp.pad(a_log, (0, pad))),
      row(jnp.repeat(d_skip, SSM_HEAD_DIM)), row(norm_g), xmat)


def _merge_kernel(ya_ref, yb_ref, yc_ref, yd_ref, wb_ref, g0_ref, g1_ref, g2_ref, g3_ref, o_ref):
    acc = None
    for j, (y_ref, g_ref) in enumerate(
            zip((ya_ref, yb_ref, yc_ref, yd_ref), (g0_ref, g1_ref, g2_ref, g3_ref))):
        t = jnp.dot(y_ref[...], wb_ref[j], preferred_element_type=F32) * g_ref[...].astype(F32)
        acc = t if acc is None else acc + t
    o_ref[...] = acc.astype(o_ref.dtype)


def _merge(ys, w_branch, gates, *, layer, bm=512, bn=1024):
    m = gates.shape[0]
    bm = min(bm, m)
    nb = D_MODEL // bn
    yspec = pl.BlockSpec((bm, W_MIX), lambda i, n: (i, 0))
    gspec = lambda j: pl.BlockSpec((bm, bn), lambda i, n: (i, j * nb + n))
    return pl.pallas_call(
        _merge_kernel,
        grid=(m // bm, nb),
        in_specs=[yspec] * N_BRANCH
        + [pl.BlockSpec((None, N_BRANCH, W_MIX, bn), lambda i, n: (layer, 0, 0, n))]
        + [gspec(j) for j in range(N_BRANCH)],
        out_specs=pl.BlockSpec((bm, bn), lambda i, n: (i, n)),
        out_shape=jax.ShapeDtypeStruct((m, D_MODEL), BF16),
        compiler_params=_params(2),
        name="branch_merge",
    )(*ys, w_branch, gates, gates, gates, gates)


def _ffn_gu_kernel(h_ref, wg_ref, wu_ref, o_ref, wg_scr, wu_scr):
    @pl.when(pl.program_id(1) == 0)
    def _():
        _cast_weight_block(wg_scr, wg_ref)
        _cast_weight_block(wu_scr, wu_ref)

    h = h_ref[...]
    g = jnp.dot(h, wg_scr[...], preferred_element_type=F32)
    u = jnp.dot(h, wu_scr[...], preferred_element_type=F32)
    o_ref[...] = (g * _sigmoid(g) * u).astype(o_ref.dtype)


def _ffn_gate_up(h, wg, wu, *, layer, bm=1024, bn=256):
    m, k = h.shape
    f = wg.shape[2]
    bm = min(bm, m)
    wspec = pl.BlockSpec((None, k, bn), lambda j, i: (layer, 0, j))
    return pl.pallas_call(
        _ffn_gu_kernel,
        grid=(f // bn, m // bm),
        in_specs=[pl.BlockSpec((bm, k), lambda j, i: (i, 0)), wspec, wspec],
        out_specs=pl.BlockSpec((bm, bn), lambda j, i: (i, j)),
        out_shape=jax.ShapeDtypeStruct((m, f), BF16),
        scratch_shapes=[pltpu.VMEM((k, bn), BF16)] * 2,
        compiler_params=_params(2),
        name="ffn_gate_up",
    )(h, wg, wu)


def kernel(x, norm_mix, w_in, conv_a, sgu_norm, sgu_w, sgu_b, q_norm, k_norm, ssm_conv_w, ssm_conv_b, ssm_dt_bias, ssm_a_log, ssm_d, ssm_norm, w_branch, w_out, norm_ffn, w_ffn_gate, w_ffn_up, w_ffn_down):
    batch, seq, d = x.shape
    m = batch * seq
    depth = w_in.shape[0]
    xr = x.reshape(m, d)
    q_fold = (1.0 / math.sqrt(SB_HEAD_DIM)) * (1.0 / math.log(2.0))
    w_in_t = jnp.swapaxes(w_in, 1, 2)
    w_branch_bf = w_branch.astype(BF16)
    w_out_bf = w_out.astype(BF16)
    w_down_bf = w_ffn_down.astype(BF16)
    for l in range(depth):
        hn = _rmsnorm(xr, norm_mix[l])
        proj = _in_proj(hn, w_in_t, layer=l, row0=0, shift=0, n=N_MIX, bm=1024, bn=512,
                        out_dtype=BF16, name="in_proj_mix")
        dt_raw = _in_proj(hn, w_in_t, layer=l, row0=N_MIX, shift=0, n=LANES, bm=1024, bn=LANES,
                          out_dtype=F32, valid_cols=SSM_HEADS, name="in_proj_dt")
        gates = _in_proj(hn, w_in_t, layer=l, row0=N_MIX, shift=GATE_SHIFT, n=N_BRANCH * d, bm=1024,
                         bn=512, out_dtype=BF16, act="sigmoid", name="in_proj_gates")

        y_a = _mixer_a(proj, conv_a[l], seq)
        y_b = _mixer_b(proj, sgu_norm[l], sgu_w[l], sgu_b[l])
        qn, kn = _qknorm(proj, q_norm[l] * q_fold, k_norm[l])
        y_c = _attention(qn, kn, proj, batch, seq)
        y_d = _mixer_d(proj, dt_raw, ssm_conv_w[l], ssm_conv_b[l], ssm_dt_bias[l], ssm_a_log[l],
                       ssm_d[l], ssm_norm[l], batch, seq)

        merged = _merge((y_a, y_b, y_c, y_d), w_branch_bf, gates, layer=l)
        xr = _matmul(merged, w_out_bf, layer=l, bm=512, bn=1024, out_dtype=F32, res=xr,
                     name="out_proj")

        hn = _rmsnorm(xr, norm_ffn[l])
        act = _ffn_gate_up(hn, w_ffn_gate, w_ffn_up, layer=l)
        xr = _matmul(act, w_down_bf, layer=l, bm=512, bn=256, out_dtype=F32, res=xr,
                     m_outer=True, name="ffn_down")
    return xr.reshape(batch, seq, d)
```

```python
import functools
import math

import jax
import jax.numpy as jnp
from jax import lax
from jax.experimental import pallas as pl
from jax.experimental.pallas import tpu as pltpu

F32 = jnp.float32
BF16 = jnp.bfloat16

EPS = 1e-6
D_MODEL = 4096
W_MIX = D_MODEL // 4
GROUP = 128
CHUNK = 128
CONV_A = 3
SGU_GROUPS = W_MIX // GROUP
SB_HEAD_DIM = 128
SB_HEADS = W_MIX // SB_HEAD_DIM
SSM_HEAD_DIM = 64
SSM_HEADS = W_MIX // SSM_HEAD_DIM
SSM_GROUPS = 2
SSM_HPG = SSM_HEADS // SSM_GROUPS
SSM_STATE = 128
SSM_CONV = 4
SSM_CONV_DIM = W_MIX + 2 * SSM_GROUPS * SSM_STATE
N_BRANCH = 4
N_MIX = 9 * W_MIX + SSM_CONV_DIM
GATE_SHIFT = SSM_HEADS
LANES = 128
BF16_SUBLANES = 16
VMEM_LIMIT = 52 * 1024 * 1024

COL_A_B, COL_A_C, COL_A_X = 0, 1, 2
COL_B_U, COL_B_V = 3, 4
COL_C_Q, COL_C_K, COL_C_V = 5, 6, 7
COL_D_Z = 8
COL_D_XBC = (9 * W_MIX) // SSM_CONV_DIM


def _params(n_axes):
    return pltpu.CompilerParams(
        dimension_semantics=("arbitrary",) * n_axes, vmem_limit_bytes=VMEM_LIMIT)


def _sigmoid(x):
    return 1.0 / (1.0 + jnp.exp(-x))


def _softplus(x):
    return jnp.maximum(x, 0.0) + jnp.log1p(jnp.exp(-jnp.abs(x)))


def _gelu_tanh(x):
    c = math.sqrt(2.0 / math.pi)
    return x * (0.5 * (1.0 + jnp.tanh(c * (x + 0.044715 * (x * x * x)))))


def _rmsnorm_kernel(x_ref, g_ref, o_ref):
    x = x_ref[...]
    ms = jnp.mean(x * x, axis=-1, keepdims=True)
    o_ref[...] = (x * lax.rsqrt(ms + EPS) * g_ref[...]).astype(o_ref.dtype)


def _rmsnorm(x, g, tm=256):
    m, d = x.shape
    return pl.pallas_call(
        _rmsnorm_kernel,
        grid=(m // tm,),
        in_specs=[pl.BlockSpec((tm, d), lambda i: (i, 0)),
                  pl.BlockSpec((1, d), lambda i: (0, 0))],
        out_specs=pl.BlockSpec((tm, d), lambda i: (i, 0)),
        out_shape=jax.ShapeDtypeStruct((m, d), BF16),
        compiler_params=_params(1),
        name="rmsnorm",
    )(x, g.reshape(1, d))


CAST_ROWS = 512


def _cast_weight_block(dst_scr, src_ref):
    k = dst_scr.shape[0]
    rows_per_step = min(CAST_ROWS, k)

    def body(s, carry):
        rows = pl.ds(pl.multiple_of(s * rows_per_step, rows_per_step), rows_per_step)
        dst_scr[rows, :] = src_ref[rows, :].astype(dst_scr.dtype)
        return carry

    lax.fori_loop(0, k // rows_per_step, body, 0)


def _mm_kernel(*refs, has_res):
    a_ref, w_ref, o_ref = refs[0], refs[1], refs[-1]
    acc = jnp.dot(a_ref[...], w_ref[...], preferred_element_type=F32)
    if has_res:
        acc = acc + refs[2][...]
    o_ref[...] = acc.astype(o_ref.dtype)


def _matmul(a, w, *, layer, bm, bn, out_dtype, res=None, m_outer=False, name="matmul"):
    m, k = a.shape
    n = w.shape[2]
    bm = min(bm, m)
    if m_outer:
        grid = (m // bm, n // bn)
        ij = lambda i, j: (i, j)
    else:
        grid = (n // bn, m // bm)
        ij = lambda j, i: (i, j)
    in_specs = [pl.BlockSpec((bm, k), lambda *g: (ij(*g)[0], 0)),
                pl.BlockSpec((None, k, bn), lambda *g: (layer, 0, ij(*g)[1]))]
    args = [a, w]
    if res is not None:
        in_specs.append(pl.BlockSpec((bm, bn), ij))
        args.append(res)
    return pl.pallas_call(
        functools.partial(_mm_kernel, has_res=res is not None),
        grid=grid,
        in_specs=in_specs,
        out_specs=pl.BlockSpec((bm, bn), ij),
        out_shape=jax.ShapeDtypeStruct((m, n), out_dtype),
        compiler_params=_params(2),
        name=name,
    )(*args)


CAST_ROWS_T = 128


def _in_proj_kernel(*refs, shift, act, valid_cols):
    a_ref, wm_ref = refs[0], refs[1]
    o_ref, w_scr = refs[-2], refs[-1]
    rows = w_scr.shape[0]

    @pl.when(pl.program_id(1) == 0)
    def _():
        step = min(CAST_ROWS_T, rows)
        for c in range(rows // step):
            lo = c * step + shift
            if lo + step <= rows:
                src = wm_ref[lo:lo + step, :]
            else:
                src = jnp.concatenate([wm_ref[lo:rows, :], refs[2][0:lo + step - rows, :]], axis=0)
            w_scr[c * step:(c + 1) * step, :] = src.astype(w_scr.dtype)

    acc = lax.dot_general(a_ref[...], w_scr[...], (((1,), (1,)), ((), ())),
                          preferred_element_type=F32)
    if act == "sigmoid":
        acc = _sigmoid(acc)
    if valid_cols is not None:
        lane = lax.broadcasted_iota(jnp.int32, acc.shape, 1)
        acc = jnp.where(lane < valid_cols, acc, 0.0)
    o_ref[...] = acc.astype(o_ref.dtype)


def _in_proj(a, w_t, *, layer, row0, shift, n, bm, bn, out_dtype, act=None, valid_cols=None, name):
    m, k = a.shape
    bm = min(bm, m)
    in_specs = [pl.BlockSpec((bm, k), lambda j, i: (i, 0)),
                pl.BlockSpec((None, bn, k), lambda j, i: (layer, row0 // bn + j, 0))]
    args = [a, w_t]
    if shift:
        in_specs.append(pl.BlockSpec(
            (None, shift, k), lambda j, i: (layer, (row0 + (j + 1) * bn) // shift, 0)))
        args.append(w_t)
    return pl.pallas_call(
        functools.partial(_in_proj_kernel, shift=shift, act=act, valid_cols=valid_cols),
        grid=(n // bn, m // bm),
        in_specs=in_specs,
        out_specs=pl.BlockSpec((bm, bn), lambda j, i: (i, j)),
        out_shape=jax.ShapeDtypeStruct((m, n), out_dtype),
        scratch_shapes=[pltpu.VMEM((bn, k), BF16)],
        compiler_params=_params(2),
        name=name,
    )(*args)


def _conv_a_kernel(b_ref, c_ref, x_ref, ch_ref, xh_ref, w_ref, o_ref, p_scr, *, blocks_per_seq):
    i = pl.program_id(0)
    tm = b_ref.shape[0]
    h = BF16_SUBLANES
    p = c_ref[...].astype(F32) * x_ref[...].astype(F32)
    ph = ch_ref[...].astype(F32) * xh_ref[...].astype(F32)
    ph = jnp.where(i % blocks_per_seq == 0, 0.0, ph)
    p_scr[0:h, :] = ph
    p_scr[h:h + tm, :] = p
    w = w_ref[...]
    y = w[2:3, :] * p + w[1:2, :] * p_scr[h - 1:h - 1 + tm, :] + w[0:1, :] * p_scr[h - 2:h - 2 + tm, :]
    o_ref[...] = (b_ref[...].astype(F32) * y).astype(o_ref.dtype)


def _mixer_a(proj, conv_w, seq, tm=512):
    m = proj.shape[0]
    tm = min(tm, seq)
    h = BF16_SUBLANES
    halo = lambda col: pl.BlockSpec(
        (h, W_MIX), lambda i: (jnp.maximum(i * (tm // h) - 1, 0), col))
    main = lambda col: pl.BlockSpec((tm, W_MIX), lambda i: (i, col))
    return pl.pallas_call(
        functools.partial(_conv_a_kernel, blocks_per_seq=seq // tm),
        grid=(m // tm,),
        in_specs=[main(COL_A_B), main(COL_A_C), main(COL_A_X), halo(COL_A_C), halo(COL_A_X),
                  pl.BlockSpec((CONV_A, W_MIX), lambda i: (0, 0))],
        out_specs=pl.BlockSpec((tm, W_MIX), lambda i: (i, 0)),
        out_shape=jax.ShapeDtypeStruct((m, W_MIX), BF16),
        scratch_shapes=[pltpu.VMEM((tm + h, W_MIX), F32)],
        compiler_params=_params(1),
        name="mixer_a_conv",
    )(proj, proj, proj, proj, proj, conv_w)


def _sgu_kernel(u_ref, v_ref, ng_ref, w_ref, bexp_ref, o_ref):
    tm = u_ref.shape[0]
    vf = _gelu_tanh(v_ref[...].astype(F32))
    mu = jnp.mean(vf, axis=-1, keepdims=True)
    d = vf - mu
    var = jnp.mean(d * d, axis=-1, keepdims=True)
    vn = (d * lax.rsqrt(var + EPS) * ng_ref[...]).astype(BF16)
    t_idx = lax.broadcasted_iota(jnp.int32, (CHUNK, CHUNK), 0)
    s_idx = lax.broadcasted_iota(jnp.int32, (CHUNK, CHUNK), 1)
    causal = s_idx <= t_idx
    ws = [jnp.where(causal, w_ref[g], 0.0).astype(BF16) for g in range(SGU_GROUPS)]
    bexp = bexp_ref[...]
    for c in range(tm // CHUNK):
        rows = slice(c * CHUNK, (c + 1) * CHUNK)
        vn_c = vn[rows, :]
        mixed = jnp.concatenate(
            [jnp.dot(ws[g], vn_c[:, g * GROUP:(g + 1) * GROUP], preferred_element_type=F32)
             for g in range(SGU_GROUPS)], axis=1)
        u = _gelu_tanh(u_ref[rows, :].astype(F32))
        o_ref[rows, :] = (u * (mixed + bexp)).astype(o_ref.dtype)


def _mixer_b(proj, norm_g, w_s, b_s, tm=512):
    m = proj.shape[0]
    tm = min(tm, m)
    bexp = jnp.repeat(b_s.T, GROUP, axis=1)
    return pl.pallas_call(
        _sgu_kernel,
        grid=(m // tm,),
        in_specs=[pl.BlockSpec((tm, W_MIX), lambda i: (i, COL_B_U)),
                  pl.BlockSpec((tm, W_MIX), lambda i: (i, COL_B_V)),
                  pl.BlockSpec((1, W_MIX), lambda i: (0, 0)),
                  pl.BlockSpec((SGU_GROUPS, CHUNK, CHUNK), lambda i: (0, 0, 0)),
                  pl.BlockSpec((CHUNK, W_MIX), lambda i: (0, 0))],
        out_specs=pl.BlockSpec((tm, W_MIX), lambda i: (i, 0)),
        out_shape=jax.ShapeDtypeStruct((m, W_MIX), BF16),
        compiler_params=_params(1),
        name="mixer_b_sgu",
    )(proj, proj, norm_g.reshape(1, W_MIX), w_s, bexp)


def _qknorm_kernel(q_ref, k_ref, qg_ref, kg_ref, qo_ref, ko_ref):
    for src, g_ref, dst in ((q_ref, qg_ref, qo_ref), (k_ref, kg_ref, ko_ref)):
        g = g_ref[...]
        for h in range(SB_HEADS):
            cols = slice(h * SB_HEAD_DIM, (h + 1) * SB_HEAD_DIM)
            x = src[:, cols].astype(F32)
            ms = jnp.mean(x * x, axis=-1, keepdims=True)
            dst[:, cols] = (x * lax.rsqrt(ms + EPS) * g).astype(dst.dtype)


def _qknorm(proj, q_g, k_g, tm=512):
    m = proj.shape[0]
    tm = min(tm, m)
    spec = lambda col: pl.BlockSpec((tm, W_MIX), lambda i: (i, col))
    gspec = pl.BlockSpec((1, SB_HEAD_DIM), lambda i: (0, 0))
    return pl.pallas_call(
        _qknorm_kernel,
        grid=(m // tm,),
        in_specs=[spec(COL_C_Q), spec(COL_C_K), gspec, gspec],
        out_specs=[spec(0), spec(0)],
        out_shape=[jax.ShapeDtypeStruct((m, W_MIX), BF16)] * 2,
        compiler_params=_params(1),
        name="qk_norm",
    )(proj, proj, q_g.reshape(1, -1), k_g.reshape(1, -1))


EXP2_ZERO_BELOW = -160.0


def _attn_kernel(q_ref, k_ref, v_ref, o_ref, acc_scr, *, tq, sub):
    i = pl.program_id(2)
    q = q_ref[...]
    jj = lax.broadcasted_iota(jnp.int32, (sub, sub), 0)
    ss = lax.broadcasted_iota(jnp.int32, (sub, sub), 1)
    uneg = jnp.where(jj > ss, -1.0, 0.0).astype(BF16)

    def tile(off, width, r, mask):
        nsb = width // sub
        pv = None
        for c in reversed(range(nsb)):
            rows = pl.ds(off + c * sub, sub)
            z2 = lax.dot_general(q, k_ref[rows, :], (((1,), (1,)), ((), ())),
                                 preferred_element_type=F32)
            neg_abs = pltpu.bitcast(pltpu.bitcast(z2, jnp.int32) | jnp.int32(-2 ** 31), F32)
            l2 = jnp.log(1.0 + jnp.exp2(neg_abs)) * (1.0 / math.log(2.0))
            sp = jnp.maximum(z2, 0.0) + l2
            lb = jnp.minimum(z2, 0.0) - l2
            if mask is not None:
                sp = jnp.where(mask[:, c * sub:(c + 1) * sub], sp, 0.0)
            rev = jnp.dot(sp.astype(BF16), uneg, preferred_element_type=F32)
            a = jnp.exp2(lb + rev + r)
            if mask is not None:
                a = jnp.where(mask[:, c * sub:(c + 1) * sub], a, 0.0)
            t = jnp.dot(a.astype(BF16), v_ref[rows, :], preferred_element_type=F32)
            pv = t if pv is None else pv + t
            r = r - jnp.sum(sp, axis=1, keepdims=True)
        acc_scr[...] += pv
        return r

    acc_scr[...] = jnp.zeros_like(acc_scr)
    row = lax.broadcasted_iota(jnp.int32, (tq, tq), 0)
    col = lax.broadcasted_iota(jnp.int32, (tq, tq), 1)
    r = tile(pl.multiple_of(i * tq, tq), tq, jnp.zeros((tq, 1), F32), col < row)

    def cond(carry):
        step, _, r_max = carry
        return jnp.logical_and(step < i, r_max > EXP2_ZERO_BELOW)

    def body(carry):
        step, r, _ = carry
        r = tile(pl.multiple_of((i - 1 - step) * tq, tq), tq, r, None)
        return step + 1, r, jnp.max(r)

    lax.while_loop(cond, body, (jnp.int32(0), r, jnp.max(r)))
    o_ref[...] = acc_scr[...].astype(o_ref.dtype)


def _attention(qn, kn, proj, batch, seq, tq=512, sub=256):
    m = qn.shape[0]
    tq = min(tq, seq)
    nq = seq // tq
    v_col0 = COL_C_V * (W_MIX // SB_HEAD_DIM)
    return pl.pallas_call(
        functools.partial(_attn_kernel, tq=tq, sub=sub),
        grid=(batch, SB_HEADS, nq),
        in_specs=[pl.BlockSpec((tq, SB_HEAD_DIM), lambda b, h, i: (b * nq + i, h)),
                  pl.BlockSpec((seq, SB_HEAD_DIM), lambda b, h, i: (b, h)),
                  pl.BlockSpec((seq, SB_HEAD_DIM), lambda b, h, i: (b, v_col0 + h))],
        out_specs=pl.BlockSpec((tq, SB_HEAD_DIM), lambda b, h, i: (b * nq + i, h)),
        out_shape=jax.ShapeDtypeStruct((m, W_MIX), BF16),
        scratch_shapes=[pltpu.VMEM((tq, SB_HEAD_DIM), F32)],
        compiler_params=_params(3),
        name="stick_breaking_attn",
    )(qn, kn, proj)


def _ssd_kernel(z_ref, xbc_ref, halo_ref, dt_ref, cw_ref, cb_ref, dtb_ref, alog_ref, dexp_ref,
                ng_ref, xmat_ref, o_ref, ht_scr, xp_scr):
    c = pl.program_id(1)
    L = CHUNK
    h = BF16_SUBLANES
    hi_prec = lax.Precision.HIGHEST
    gw = W_MIX // SSM_GROUPS

    @pl.when(c == 0)
    def _():
        ht_scr[...] = jnp.zeros_like(ht_scr)

    xp_scr[0:h, :] = jnp.where(c == 0, 0.0, halo_ref[...].astype(F32))
    xp_scr[h:h + L, :] = xbc_ref[...].astype(F32)
    cw = cw_ref[...]
    conv = cb_ref[...]
    for k in range(SSM_CONV):
        s0 = h - (SSM_CONV - 1) + k
        conv = conv + cw[k:k + 1, :] * xp_scr[s0:s0 + L, :]
    xc = conv * _sigmoid(conv)
    xs = xc[:, :W_MIX]
    bmat = xc[:, W_MIX:W_MIX + SSM_GROUPS * SSM_STATE].astype(BF16)
    cmat = xc[:, W_MIX + SSM_GROUPS * SSM_STATE:].astype(BF16)

    dt = _softplus(dt_ref[...] + dtb_ref[...])
    da = dt * (-jnp.exp(alog_ref[...]))
    t_i = lax.broadcasted_iota(jnp.int32, (L, L), 0)
    s_i = lax.broadcasted_iota(jnp.int32, (L, L), 1)
    causal = s_i <= t_i
    cum = jnp.dot(causal.astype(F32), da, precision=hi_prec, preferred_element_type=F32)
    cum_t = cum.T
    xmat = xmat_ref[...]
    dt_e = jnp.dot(dt, xmat, precision=hi_prec, preferred_element_type=F32)
    cum_e = jnp.dot(cum, xmat, precision=hi_prec, preferred_element_type=F32)
    cl_e = cum_e[L - 1:L, :]
    xdt = xs * dt_e
    xdec = (xdt * jnp.exp(cl_e - cum_e)).astype(BF16)
    ecum_e = jnp.exp(cum_e)
    chunk_dec = jnp.exp(cl_e)
    lane = lax.broadcasted_iota(jnp.int32, (L, LANES), 1)
    left = lane < SSM_HEAD_DIM

    ys = []
    for g in range(SSM_GROUPS):
        bg = bmat[:, g * SSM_STATE:(g + 1) * SSM_STATE]
        cg = cmat[:, g * SSM_STATE:(g + 1) * SSM_STATE]
        gcols = slice(g * gw, (g + 1) * gw)
        cb = lax.dot_general(cg, bg, (((1,), (1,)), ((), ())), preferred_element_type=F32)
        ht = ht_scr[g]
        y_off = jnp.dot(cg, ht.astype(BF16), preferred_element_type=F32)
        parts = []
        for pr in range(SSM_HPG // 2):
            ms = []
            for r in range(2):
                idx = g * SSM_HPG + 2 * pr + r
                seg = cum[:, idx:idx + 1] - cum_t[idx:idx + 1, :]
                dec = jnp.exp(jnp.where(causal, seg, -jnp.inf))
                ms.append((cb * dec).astype(BF16))
            c0 = g * gw + pr * LANES
            xpair = xdt[:, c0:c0 + LANES]
            rhs = jnp.concatenate(
                [jnp.where(left, xpair, 0.0), jnp.where(left, 0.0, xpair)], axis=0).astype(BF16)
            parts.append(jnp.dot(jnp.concatenate(ms, axis=1), rhs, preferred_element_type=F32))
        y_diag = jnp.concatenate(parts, axis=1)
        ys.append(y_diag + y_off * ecum_e[:, gcols])
        st = lax.dot_general(bg, xdec[:, gcols], (((0,), (0,)), ((), ())),
                             preferred_element_type=F32)
        ht_scr[g] = ht * chunk_dec[:, gcols] + st

    y = jnp.concatenate(ys, axis=1) + xs * dexp_ref[...]
    z = z_ref[...].astype(F32)
    y = y * (z * _sigmoid(z))
    outs = []
    for g in range(SSM_GROUPS):
        yg = y[:, g * gw:(g + 1) * gw]
        ms = jnp.mean(yg * yg, axis=-1, keepdims=True)
        outs.append(yg * lax.rsqrt(ms + EPS))
    o_ref[...] = (jnp.concatenate(outs, axis=1) * ng_ref[...]).astype(o_ref.dtype)


def _mixer_d(proj, dt_raw, conv_w, conv_b, dt_bias, a_log, d_skip, norm_g, batch, seq):
    m = proj.shape[0]
    nc = seq // CHUNK
    h = BF16_SUBLANES
    pad = LANES - SSM_HEADS
    row = lambda v: v.reshape(1, -1)
    xmat = (jnp.arange(W_MIX)[None, :] // SSM_HEAD_DIM == jnp.arange(LANES)[:, None]).astype(F32)
    const = lambda shape: pl.BlockSpec(shape, lambda b, c: (0,) * len(shape))
    return pl.pallas_call(
        _ssd_kernel,
        grid=(batch, nc),
        in_specs=[pl.BlockSpec((CHUNK, W_MIX), lambda b, c: (b * nc + c, COL_D_Z)),
                  pl.BlockSpec((CHUNK, SSM_CONV_DIM), lambda b, c: (b * nc + c, COL_D_XBC)),
                  pl.BlockSpec((h, SSM_CONV_DIM),
                               lambda b, c: (jnp.maximum((b * nc + c) * (CHUNK // h) - 1, 0), COL_D_XBC)),
                  pl.BlockSpec((CHUNK, LANES), lambda b, c: (b * nc + c, 0)),
                  const((SSM_CONV, SSM_CONV_DIM)), const((1, SSM_CONV_DIM)),
                  const((1, LANES)), const((1, LANES)), const((1, W_MIX)), const((1, W_MIX)),
                  const((LANES, W_MIX))],
        out_specs=pl.BlockSpec((CHUNK, W_MIX), lambda b, c: (b * nc + c, 0)),
        out_shape=jax.ShapeDtypeStruct((m, W_MIX), BF16),
        scratch_shapes=[pltpu.VMEM((SSM_GROUPS, SSM_STATE, W_MIX // SSM_GROUPS), F32),
                        pltpu.VMEM((CHUNK + h, SSM_CONV_DIM), F32)],
        compiler_params=_params(2),
        name="mixer_d_ssd",
    )(proj, proj, proj, dt_raw, conv_w, row(conv_b),
      row(jnp.pad(dt_bias, (0, pad))), row(jnp.pad(a_log, (0, pad))),
      row(jnp.repeat(d_skip, SSM_HEAD_DIM)), row(norm_g), xmat)


def _merge_kernel(ya_ref, yb_ref, yc_ref, yd_ref, wb_ref, g0_ref, g1_ref, g2_ref, g3_ref, o_ref):
    acc = None
    for j, (y_ref, g_ref) in enumerate(
            zip((ya_ref, yb_ref, yc_ref, yd_ref), (g0_ref, g1_ref, g2_ref, g3_ref))):
        t = jnp.dot(y_ref[...], wb_ref[j], preferred_element_type=F32) * g_ref[...].astype(F32)
        acc = t if acc is None else acc + t
    o_ref[...] = acc.astype(o_ref.dtype)


def _merge(ys, w_branch, gates, *, layer, bm=512, bn=1024):
    m = gates.shape[0]
    bm = min(bm, m)
    nb = D_MODEL // bn
    yspec = pl.BlockSpec((bm, W_MIX), lambda i, n: (i, 0))
    gspec = lambda j: pl.BlockSpec((bm, bn), lambda i, n: (i, j * nb + n))
    return pl.pallas_call(
        _merge_kernel,
        grid=(m // bm, nb),
        in_specs=[yspec] * N_BRANCH
        + [pl.BlockSpec((None, N_BRANCH, W_MIX, bn), lambda i, n: (layer, 0, 0, n))]
        + [gspec(j) for j in range(N_BRANCH)],
        out_specs=pl.BlockSpec((bm, bn), lambda i, n: (i, n)),
        out_shape=jax.ShapeDtypeStruct((m, D_MODEL), BF16),
        compiler_params=_params(2),
        name="branch_merge",
    )(*ys, w_branch, gates, gates, gates, gates)


def _ffn_gu_kernel(h_ref, wg_ref, wu_ref, o_ref, wg_scr, wu_scr):
    @pl.when(pl.program_id(1) == 0)
    def _():
        _cast_weight_block(wg_scr, wg_ref)
        _cast_weight_block(wu_scr, wu_ref)

    h = h_ref[...]
    g = jnp.dot(h, wg_scr[...], preferred_element_type=F32)
    u = jnp.dot(h, wu_scr[...], preferred_element_type=F32)
    o_ref[...] = (g * _sigmoid(g) * u).astype(o_ref.dtype)


def _ffn_gate_up(h, wg, wu, *, layer, bm=1024, bn=256):
    m, k = h.shape
    f = wg.shape[2]
    bm = min(bm, m)
    wspec = pl.BlockSpec((None, k, bn), lambda j, i: (layer, 0, j))
    return pl.pallas_call(
        _ffn_gu_kernel,
        grid=(f // bn, m // bm),
        in_specs=[pl.BlockSpec((bm, k), lambda j, i: (i, 0)), wspec, wspec],
        out_specs=pl.BlockSpec((bm, bn), lambda j, i: (i, j)),
        out_shape=jax.ShapeDtypeStruct((m, f), BF16),
        scratch_shapes=[pltpu.VMEM((k, bn), BF16)] * 2,
        compiler_params=_params(2),
        name="ffn_gate_up",
    )(h, wg, wu)


def kernel(x, norm_mix, w_in, conv_a, sgu_norm, sgu_w, sgu_b, q_norm, k_norm, ssm_conv_w, ssm_conv_b, ssm_dt_bias, ssm_a_log, ssm_d, ssm_norm, w_branch, w_out, norm_ffn, w_ffn_gate, w_ffn_up, w_ffn_down):
    batch, seq, d = x.shape
    m = batch * seq
    depth = w_in.shape[0]
    xr = x.reshape(m, d)
    q_fold = (1.0 / math.sqrt(SB_HEAD_DIM)) * (1.0 / math.log(2.0))
    w_in_t = jnp.swapaxes(w_in, 1, 2)
    w_branch_bf = w_branch.astype(BF16)
    w_out_bf = w_out.astype(BF16)
    w_down_bf = w_ffn_down.astype(BF16)
    for l in range(depth):
        hn = _rmsnorm(xr, norm_mix[l])
        proj = _in_proj(hn, w_in_t, layer=l, row0=0, shift=0, n=N_MIX, bm=1024, bn=512,
                        out_dtype=BF16, name="in_proj_mix")
        dt_raw = _in_proj(hn, w_in_t, layer=l, row0=N_MIX, shift=0, n=LANES, bm=1024, bn=LANES,
                          out_dtype=F32, valid_cols=SSM_HEADS, name="in_proj_dt")
        gates = _in_proj(hn, w_in_t, layer=l, row0=N_MIX, shift=GATE_SHIFT, n=N_BRANCH * d, bm=1024,
                         bn=512, out_dtype=BF16, act="sigmoid", name="in_proj_gates")

        y_a = _mixer_a(proj, conv_a[l], seq)
        y_b = _mixer_b(proj, sgu_norm[l], sgu_w[l], sgu_b[l])
        qn, kn = _qknorm(proj, q_norm[l] * q_fold, k_norm[l])
        y_c = _attention(qn, kn, proj, batch, seq)
        y_d = _mixer_d(proj, dt_raw, ssm_conv_w[l], ssm_conv_b[l], ssm_dt_bias[l], ssm_a_log[l],
                       ssm_d[l], ssm_norm[l], batch, seq)

        merged = _merge((y_a, y_b, y_c, y_d), w_branch_bf, gates, layer=l)
        xr = _matmul(merged, w_out_bf, layer=l, bm=512, bn=1024, out_dtype=F32, res=xr,
                     name="out_proj")

        hn = _rmsnorm(xr, norm_ffn[l])
        act = _ffn_gate_up(hn, w_ffn_gate, w_ffn_up, layer=l)
        xr = _matmul(act, w_down_bf, layer=l, bm=512, bn=256, out_dtype=F32, res=xr,
                     m_outer=True, name="ffn_down")
    return xr.reshape(batch, seq, d)
```

```python
import functools
import math

import jax
import jax.numpy as jnp
from jax import lax
from jax.experimental import pallas as pl
from jax.experimental.pallas import tpu as pltpu

F32 = jnp.float32
BF16 = jnp.bfloat16

EPS = 1e-6
D_MODEL = 4096
W_MIX = D_MODEL // 4
GROUP = 128
CHUNK = 128
CONV_A = 3
SGU_GROUPS = W_MIX // GROUP
SB_HEAD_DIM = 128
SB_HEADS = W_MIX // SB_HEAD_DIM
SSM_HEAD_DIM = 64
SSM_HEADS = W_MIX // SSM_HEAD_DIM
SSM_GROUPS = 2
SSM_HPG = SSM_HEADS // SSM_GROUPS
SSM_STATE = 128
SSM_CONV = 4
SSM_CONV_DIM = W_MIX + 2 * SSM_GROUPS * SSM_STATE
N_BRANCH = 4
N_MIX = 9 * W_MIX + SSM_CONV_DIM
GATE_SHIFT = SSM_HEADS
LANES = 128
BF16_SUBLANES = 16
VMEM_LIMIT = 52 * 1024 * 1024

COL_A_B, COL_A_C, COL_A_X = 0, 1, 2
COL_B_U, COL_B_V = 3, 4
COL_C_Q, COL_C_K, COL_C_V = 5, 6, 7
COL_D_Z = 8
COL_D_XBC = (9 * W_MIX) // SSM_CONV_DIM


def _params(n_axes):
    return pltpu.CompilerParams(
        dimension_semantics=("arbitrary",) * n_axes, vmem_limit_bytes=VMEM_LIMIT)


def _sigmoid(x):
    return 1.0 / (1.0 + jnp.exp(-x))


def _softplus(x):
    return jnp.maximum(x, 0.0) + jnp.log1p(jnp.exp(-jnp.abs(x)))


def _gelu_tanh(x):
    c = math.sqrt(2.0 / math.pi)
    return x * (0.5 * (1.0 + jnp.tanh(c * (x + 0.044715 * (x * x * x)))))


def _rmsnorm_kernel(x_ref, g_ref, o_ref):
    x = x_ref[...]
    ms = jnp.mean(x * x, axis=-1, keepdims=True)
    o_ref[...] = (x * lax.rsqrt(ms + EPS) * g_ref[...]).astype(o_ref.dtype)


def _rmsnorm(x, g, tm=256):
    m, d = x.shape
    return pl.pallas_call(
        _rmsnorm_kernel,
        grid=(m // tm,),
        in_specs=[pl.BlockSpec((tm, d), lambda i: (i, 0)),
                  pl.BlockSpec((1, d), lambda i: (0, 0))],
        out_specs=pl.BlockSpec((tm, d), lambda i: (i, 0)),
        out_shape=jax.ShapeDtypeStruct((m, d), BF16),
        compiler_params=_params(1),
        name="rmsnorm",
    )(x, g.reshape(1, d))


CAST_ROWS = 512


def _cast_weight_block(dst_scr, src_ref):
    k = dst_scr.shape[0]
    rows_per_step = min(CAST_ROWS, k)

    def body(s, carry):
        rows = pl.ds(pl.multiple_of(s * rows_per_step, rows_per_step), rows_per_step)
        dst_scr[rows, :] = src_ref[rows, :].astype(dst_scr.dtype)
        return carry

    lax.fori_loop(0, k // rows_per_step, body, 0)


def _mm_kernel(*refs, has_res):
    a_ref, w_ref, o_ref = refs[0], refs[1], refs[-1]
    acc = jnp.dot(a_ref[...], w_ref[...], preferred_element_type=F32)
    if has_res:
        acc = acc + refs[2][...]
    o_ref[...] = acc.astype(o_ref.dtype)


def _matmul(a, w, *, layer, bm, bn, out_dtype, res=None, m_outer=False, name="matmul"):
    m, k = a.shape
    n = w.shape[2]
    bm = min(bm, m)
    if m_outer:
        grid = (m // bm, n // bn)
        ij = lambda i, j: (i, j)
    else:
        grid = (n // bn, m // bm)
        ij = lambda j, i: (i, j)
    in_specs = [pl.BlockSpec((bm, k), lambda *g: (ij(*g)[0], 0)),
                pl.BlockSpec((None, k, bn), lambda *g: (layer, 0, ij(*g)[1]))]
    args = [a, w]
    if res is not None:
        in_specs.append(pl.BlockSpec((bm, bn), ij))
        args.append(res)
    return pl.pallas_call(
        functools.partial(_mm_kernel, has_res=res is not None),
        grid=grid,
        in_specs=in_specs,
        out_specs=pl.BlockSpec((bm, bn), ij),
        out_shape=jax.ShapeDtypeStruct((m, n), out_dtype),
        compiler_params=_params(2),
        name=name,
    )(*args)


CAST_ROWS_T = 128


def _in_proj_kernel(*refs, shift, act, valid_cols):
    a_ref, wm_ref = refs[0], refs[1]
    o_ref, w_scr = refs[-2], refs[-1]
    rows = w_scr.shape[0]

    @pl.when(pl.program_id(1) == 0)
    def _():
        step = min(CAST_ROWS_T, rows)
        for c in range(rows // step):
            lo = c * step + shift
            if lo + step <= rows:
                src = wm_ref[lo:lo + step, :]
            else:
                src = jnp.concatenate([wm_ref[lo:rows, :], refs[2][0:lo + step - rows, :]], axis=0)
            w_scr[c * step:(c + 1) * step, :] = src.astype(w_scr.dtype)

    acc = lax.dot_general(a_ref[...], w_scr[...], (((1,), (1,)), ((), ())),
                          preferred_element_type=F32)
    if act == "sigmoid":
        acc = _sigmoid(acc)
    if valid_cols is not None:
        lane = lax.broadcasted_iota(jnp.int32, acc.shape, 1)
        acc = jnp.where(lane < valid_cols, acc, 0.0)
    o_ref[...] = acc.astype(o_ref.dtype)


def _in_proj(a, w_t, *, layer, row0, shift, n, bm, bn, out_dtype, act=None, valid_cols=None, name):
    m, k = a.shape
    bm = min(bm, m)
    in_specs = [pl.BlockSpec((bm, k), lambda j, i: (i, 0)),
                pl.BlockSpec((None, bn, k), lambda j, i: (layer, row0 // bn + j, 0))]
    args = [a, w_t]
    if shift:
        in_specs.append(pl.BlockSpec(
            (None, shift, k), lambda j, i: (layer, (row0 + (j + 1) * bn) // shift, 0)))
        args.append(w_t)
    return pl.pallas_call(
        functools.partial(_in_proj_kernel, shift=shift, act=act, valid_cols=valid_cols),
        grid=(n // bn, m // bm),
        in_specs=in_specs,
        out_specs=pl.BlockSpec((bm, bn), lambda j, i: (i, j)),
        out_shape=jax.ShapeDtypeStruct((m, n), out_dtype),
        scratch_shapes=[pltpu.VMEM((bn, k), BF16)],
        compiler_params=_params(2),
        name=name,
    )(*args)


def _conv_a_kernel(b_ref, c_ref, x_ref, ch_ref, xh_ref, w_ref, o_ref, p_scr, *, blocks_per_seq):
    i = pl.program_id(0)
    tm = b_ref.shape[0]
    h = BF16_SUBLANES
    p = c_ref[...].astype(F32) * x_ref[...].astype(F32)
    ph = ch_ref[...].astype(F32) * xh_ref[...].astype(F32)
    ph = jnp.where(i % blocks_per_seq == 0, 0.0, ph)
    p_scr[0:h, :] = ph
    p_scr[h:h + tm, :] = p
    w = w_ref[...]
    y = w[2:3, :] * p + w[1:2, :] * p_scr[h - 1:h - 1 + tm, :] + w[0:1, :] * p_scr[h - 2:h - 2 + tm, :]
    o_ref[...] = (b_ref[...].astype(F32) * y).astype(o_ref.dtype)


def _mixer_a(proj, conv_w, seq, tm=512):
    m = proj.shape[0]
    tm = min(tm, seq)
    h = BF16_SUBLANES
    halo = lambda col: pl.BlockSpec(
        (h, W_MIX), lambda i: (jnp.maximum(i * (tm // h) - 1, 0), col))
    main = lambda col: pl.BlockSpec((tm, W_MIX), lambda i: (i, col))
    return pl.pallas_call(
        functools.partial(_conv_a_kernel, blocks_per_seq=seq // tm),
        grid=(m // tm,),
        in_specs=[main(COL_A_B), main(COL_A_C), main(COL_A_X), halo(COL_A_C), halo(COL_A_X),
                  pl.BlockSpec((CONV_A, W_MIX), lambda i: (0, 0))],
        out_specs=pl.BlockSpec((tm, W_MIX), lambda i: (i, 0)),
        out_shape=jax.ShapeDtypeStruct((m, W_MIX), BF16),
        scratch_shapes=[pltpu.VMEM((tm + h, W_MIX), F32)],
        compiler_params=_params(1),
        name="mixer_a_conv",
    )(proj, proj, proj, proj, proj, conv_w)


def _sgu_kernel(u_ref, v_ref, ng_ref, w_ref, bexp_ref, o_ref):
    tm = u_ref.shape[0]
    vf = _gelu_tanh(v_ref[...].astype(F32))
    mu = jnp.mean(vf, axis=-1, keepdims=True)
    d = vf - mu
    var = jnp.mean(d * d, axis=-1, keepdims=True)
    vn = (d * lax.rsqrt(var + EPS) * ng_ref[...]).astype(BF16)
    t_idx = lax.broadcasted_iota(jnp.int32, (CHUNK, CHUNK), 0)
    s_idx = lax.broadcasted_iota(jnp.int32, (CHUNK, CHUNK), 1)
    causal = s_idx <= t_idx
    ws = [jnp.where(causal, w_ref[g], 0.0).astype(BF16) for g in range(SGU_GROUPS)]
    bexp = bexp_ref[...]
    for c in range(tm // CHUNK):
        rows = slice(c * CHUNK, (c + 1) * CHUNK)
        vn_c = vn[rows, :]
        mixed = jnp.concatenate(
            [jnp.dot(ws[g], vn_c[:, g * GROUP:(g + 1) * GROUP], preferred_element_type=F32)
             for g in range(SGU_GROUPS)], axis=1)
        u = _gelu_tanh(u_ref[rows, :].astype(F32))
        o_ref[rows, :] = (u * (mixed + bexp)).astype(o_ref.dtype)


def _mixer_b(proj, norm_g, w_s, b_s, tm=512):
    m = proj.shape[0]
    tm = min(tm, m)
    bexp = jnp.repeat(b_s.T, GROUP, axis=1)
    return pl.pallas_call(
        _sgu_kernel,
        grid=(m // tm,),
        in_specs=[pl.BlockSpec((tm, W_MIX), lambda i: (i, COL_B_U)),
                  pl.BlockSpec((tm, W_MIX), lambda i: (i, COL_B_V)),
                  pl.BlockSpec((1, W_MIX), lambda i: (0, 0)),
                  pl.BlockSpec((SGU_GROUPS, CHUNK, CHUNK), lambda i: (0, 0, 0)),
                  pl.BlockSpec((CHUNK, W_MIX), lambda i: (0, 0))],
        out_specs=pl.BlockSpec((tm, W_MIX), lambda i: (i, 0)),
        out_shape=jax.ShapeDtypeStruct((m, W_MIX), BF16),
        compiler_params=_params(1),
        name="mixer_b_sgu",
    )(proj, proj, norm_g.reshape(1, W_MIX), w_s, bexp)


def _qknorm_kernel(q_ref, k_ref, qg_ref, kg_ref, qo_ref, ko_ref):
    for src, g_ref, dst in ((q_ref, qg_ref, qo_ref), (k_ref, kg_ref, ko_ref)):
        g = g_ref[...]
        for h in range(SB_HEADS):
            cols = slice(h * SB_HEAD_DIM, (h + 1) * SB_HEAD_DIM)
            x = src[:, cols].astype(F32)
            ms = jnp.mean(x * x, axis=-1, keepdims=True)
            dst[:, cols] = (x * lax.rsqrt(ms + EPS) * g).astype(dst.dtype)


def _qknorm(proj, q_g, k_g, tm=512):
    m = proj.shape[0]
    tm = min(tm, m)
    spec = lambda col: pl.BlockSpec((tm, W_MIX), lambda i: (i, col))
    gspec = pl.BlockSpec((1, SB_HEAD_DIM), lambda i: (0, 0))
    return pl.pallas_call(
        _qknorm_kernel,
        grid=(m // tm,),
        in_specs=[spec(COL_C_Q), spec(COL_C_K), gspec, gspec],
        out_specs=[spec(0), spec(0)],
        out_shape=[jax.ShapeDtypeStruct((m, W_MIX), BF16)] * 2,
        compiler_params=_params(1),
        name="qk_norm",
    )(proj, proj, q_g.reshape(1, -1), k_g.reshape(1, -1))


EXP2_ZERO_BELOW = -160.0


def _attn_kernel(q_ref, k_ref, v_ref, o_ref, acc_scr, *, tq, sub):
    i = pl.program_id(2)
    q = q_ref[...]
    jj = lax.broadcasted_iota(jnp.int32, (sub, sub), 0)
    ss = lax.broadcasted_iota(jnp.int32, (sub, sub), 1)
    uneg = jnp.where(jj > ss, -1.0, 0.0).astype(BF16)

    def tile(off, width, r, mask):
        nsb = width // sub
        pv = None
        for c in reversed(range(nsb)):
            rows = pl.ds(off + c * sub, sub)
            z2 = lax.dot_general(q, k_ref[rows, :], (((1,), (1,)), ((), ())),
                                 preferred_element_type=F32)
            neg_abs = pltpu.bitcast(pltpu.bitcast(z2, jnp.int32) | jnp.int32(-2 ** 31), F32)
            l2 = jnp.log(1.0 + jnp.exp2(neg_abs)) * (1.0 / math.log(2.0))
            sp = jnp.maximum(z2, 0.0) + l2
            lb = jnp.minimum(z2, 0.0) - l2
            if mask is not None:
                sp = jnp.where(mask[:, c * sub:(c + 1) * sub], sp, 0.0)
            rev = jnp.dot(sp.astype(BF16), uneg, preferred_element_type=F32)
            a = jnp.exp2(lb + rev + r)
            if mask is not None:
                a = jnp.where(mask[:, c * sub:(c + 1) * sub], a, 0.0)
            t = jnp.dot(a.astype(BF16), v_ref[rows, :], preferred_element_type=F32)
            pv = t if pv is None else pv + t
            r = r - jnp.sum(sp, axis=1, keepdims=True)
        acc_scr[...] += pv
        return r

    acc_scr[...] = jnp.zeros_like(acc_scr)
    row = lax.broadcasted_iota(jnp.int32, (tq, tq), 0)
    col = lax.broadcasted_iota(jnp.int32, (tq, tq), 1)
    r = tile(pl.multiple_of(i * tq, tq), tq, jnp.zeros((tq, 1), F32), col < row)

    def cond(carry):
        step, _, r_max = carry
        return jnp.logical_and(step < i, r_max > EXP2_ZERO_BELOW)

    def body(carry):
        step, r, _ = carry
        r = tile(pl.multiple_of((i - 1 - step) * tq, tq), tq, r, None)
        return step + 1, r, jnp.max(r)

    lax.while_loop(cond, body, (jnp.int32(0), r, jnp.max(r)))
    o_ref[...] = acc_scr[...].astype(o_ref.dtype)


def _attention(qn, kn, proj, batch, seq, tq=512, sub=256):
    m = qn.shape[0]
    tq = min(tq, seq)
    nq = seq // tq
    v_col0 = COL_C_V * (W_MIX // SB_HEAD_DIM)
    return pl.pallas_call(
        functools.partial(_attn_kernel, tq=tq, sub=sub),
        grid=(batch, SB_HEADS, nq),
        in_specs=[pl.BlockSpec((tq, SB_HEAD_DIM), lambda b, h, i: (b * nq + i, h)),
                  pl.BlockSpec((seq, SB_HEAD_DIM), lambda b, h, i: (b, h)),
                  pl.BlockSpec((seq, SB_HEAD_DIM), lambda b, h, i: (b, v_col0 + h))],
        out_specs=pl.BlockSpec((tq, SB_HEAD_DIM), lambda b, h, i: (b * nq + i, h)),
        out_shape=jax.ShapeDtypeStruct((m, W_MIX), BF16),
        scratch_shapes=[pltpu.VMEM((tq, SB_HEAD_DIM), F32)],
        compiler_params=_params(3),
        name="stick_breaking_attn",
    )(qn, kn, proj)


def _split3(x):
    h1 = x.astype(BF16)
    r1 = x - h1.astype(F32)
    h2 = r1.astype(BF16)
    h3 = (r1 - h2.astype(F32)).astype(BF16)
    return [h1, h2, h3]


def _ssd_kernel(z_ref, xbc_ref, halo_ref, dt_ref, cw_ref, cb_ref, dtb_ref, alog_ref, dexp_ref,
                ng_ref, xmat_ref, o_ref, ht_scr, xp_scr):
    c = pl.program_id(1)
    L = CHUNK
    h = BF16_SUBLANES
    gw = W_MIX // SSM_GROUPS

    @pl.when(c == 0)
    def _():
        ht_scr[...] = jnp.zeros_like(ht_scr)

    xp_scr[0:h, :] = jnp.where(c == 0, 0.0, halo_ref[...].astype(F32))
    xp_scr[h:h + L, :] = xbc_ref[...].astype(F32)
    cw = cw_ref[...]
    conv = cb_ref[...]
    for k in range(SSM_CONV):
        s0 = h - (SSM_CONV - 1) + k
        conv = conv + cw[k:k + 1, :] * xp_scr[s0:s0 + L, :]
    xc = conv * _sigmoid(conv)
    xs = xc[:, :W_MIX]
    bmat = xc[:, W_MIX:W_MIX + SSM_GROUPS * SSM_STATE].astype(BF16)
    cmat = xc[:, W_MIX + SSM_GROUPS * SSM_STATE:].astype(BF16)

    dt = _softplus(dt_ref[...] + dtb_ref[...])
    da = dt * (-jnp.exp(alog_ref[...]))
    t_i = lax.broadcasted_iota(jnp.int32, (L, L), 0)
    s_i = lax.broadcasted_iota(jnp.int32, (L, L), 1)
    causal = s_i <= t_i
    tri3 = jnp.concatenate([causal.astype(BF16)] * 3, axis=1)
    cum = jnp.dot(tri3, jnp.concatenate(_split3(da), axis=0), preferred_element_type=F32)
    cum_t = cum.T
    xmat3 = xmat_ref[...]
    dt_e = jnp.dot(jnp.concatenate(_split3(dt), axis=1), xmat3, preferred_element_type=F32)
    cum_e = jnp.dot(jnp.concatenate(_split3(cum), axis=1), xmat3, preferred_element_type=F32)
    cl_e = cum_e[L - 1:L, :]
    xdt = xs * dt_e
    xdec = (xdt * jnp.exp(cl_e - cum_e)).astype(BF16)
    ecum_e = jnp.exp(cum_e)
    chunk_dec = jnp.exp(cl_e)
    lane = lax.broadcasted_iota(jnp.int32, (L, LANES), 1)
    left = lane < SSM_HEAD_DIM

    ys = []
    for g in range(SSM_GROUPS):
        bg = bmat[:, g * SSM_STATE:(g + 1) * SSM_STATE]
        cg = cmat[:, g * SSM_STATE:(g + 1) * SSM_STATE]
        gcols = slice(g * gw, (g + 1) * gw)
        cb = lax.dot_general(cg, bg, (((1,), (1,)), ((), ())), preferred_element_type=F32)
        ht = ht_scr[g]
        y_off = jnp.dot(cg, ht.astype(BF16), preferred_element_type=F32)
        parts = []
        for pr in range(SSM_HPG // 2):
            ms = []
            for r in range(2):
                idx = g * SSM_HPG + 2 * pr + r
                seg = cum[:, idx:idx + 1] - cum_t[idx:idx + 1, :]
                dec = jnp.exp(jnp.where(causal, seg, -jnp.inf))
                ms.append((cb * dec).astype(BF16))
            c0 = g * gw + pr * LANES
            xpair = xdt[:, c0:c0 + LANES]
            rhs = jnp.concatenate(
                [jnp.where(left, xpair, 0.0), jnp.where(left, 0.0, xpair)], axis=0).astype(BF16)
            parts.append(jnp.dot(jnp.concatenate(ms, axis=1), rhs, preferred_element_type=F32))
        y_diag = jnp.concatenate(parts, axis=1)
        ys.append(y_diag + y_off * ecum_e[:, gcols])
        st = lax.dot_general(bg, xdec[:, gcols], (((0,), (0,)), ((), ())),
                             preferred_element_type=F32)
        ht_scr[g] = ht * chunk_dec[:, gcols] + st

    y = jnp.concatenate(ys, axis=1) + xs * dexp_ref[...]
    z = z_ref[...].astype(F32)
    y = y * (z * _sigmoid(z))
    outs = []
    for g in range(SSM_GROUPS):
        yg = y[:, g * gw:(g + 1) * gw]
        ms = jnp.mean(yg * yg, axis=-1, keepdims=True)
        outs.append(yg * lax.rsqrt(ms + EPS))
    o_ref[...] = (jnp.concatenate(outs, axis=1) * ng_ref[...]).astype(o_ref.dtype)


def _mixer_d(proj, dt_raw, conv_w, conv_b, dt_bias, a_log, d_skip, norm_g, batch, seq):
    m = proj.shape[0]
    nc = seq // CHUNK
    h = BF16_SUBLANES
    pad = LANES - SSM_HEADS
    row = lambda v: v.reshape(1, -1)
    xmat = (jnp.arange(W_MIX)[None, :] // SSM_HEAD_DIM == jnp.arange(LANES)[:, None]).astype(BF16)
    xmat = jnp.concatenate([xmat] * 3, axis=0)
    const = lambda shape: pl.BlockSpec(shape, lambda b, c: (0,) * len(shape))
    return pl.pallas_call(
        _ssd_kernel,
        grid=(batch, nc),
        in_specs=[pl.BlockSpec((CHUNK, W_MIX), lambda b, c: (b * nc + c, COL_D_Z)),
                  pl.BlockSpec((CHUNK, SSM_CONV_DIM), lambda b, c: (b * nc + c, COL_D_XBC)),
                  pl.BlockSpec((h, SSM_CONV_DIM),
                               lambda b, c: (jnp.maximum((b * nc + c) * (CHUNK // h) - 1, 0), COL_D_XBC)),
                  pl.BlockSpec((CHUNK, LANES), lambda b, c: (b * nc + c, 0)),
                  const((SSM_CONV, SSM_CONV_DIM)), const((1, SSM_CONV_DIM)),
                  const((1, LANES)), const((1, LANES)), const((1, W_MIX)), const((1, W_MIX)),
                  const((3 * LANES, W_MIX))],
        out_specs=pl.BlockSpec((CHUNK, W_MIX), lambda b, c: (b * nc + c, 0)),
        out_shape=jax.ShapeDtypeStruct((m, W_MIX), BF16),
        scratch_shapes=[pltpu.VMEM((SSM_GROUPS, SSM_STATE, W_MIX // SSM_GROUPS), F32),
                        pltpu.VMEM((CHUNK + h, SSM_CONV_DIM), F32)],
        compiler_params=_params(2),
        name="mixer_d_ssd",
    )(proj, proj, proj, dt_raw, conv_w, row(conv_b),
      row(jnp.pad(dt_bias, (0, pad))), row(jnp.pad(a_log, (0, pad))),
      row(jnp.repeat(d_skip, SSM_HEAD_DIM)), row(norm_g), xmat)


def _merge_kernel(ya_ref, yb_ref, yc_ref, yd_ref, wb_ref, g0_ref, g1_ref, g2_ref, g3_ref, o_ref):
    acc = None
    for j, (y_ref, g_ref) in enumerate(
            zip((ya_ref, yb_ref, yc_ref, yd_ref), (g0_ref, g1_ref, g2_ref, g3_ref))):
        t = jnp.dot(y_ref[...], wb_ref[j], preferred_element_type=F32) * g_ref[...].astype(F32)
        acc = t if acc is None else acc + t
    o_ref[...] = acc.astype(o_ref.dtype)


def _merge(ys, w_branch, gates, *, layer, bm=512, bn=1024):
    m = gates.shape[0]
    bm = min(bm, m)
    nb = D_MODEL // bn
    yspec = pl.BlockSpec((bm, W_MIX), lambda n, i: (i, 0))
    gspec = lambda j: pl.BlockSpec((bm, bn), lambda n, i: (i, j * nb + n))
    return pl.pallas_call(
        _merge_kernel,
        grid=(nb, m // bm),
        in_specs=[yspec] * N_BRANCH
        + [pl.BlockSpec((None, N_BRANCH, W_MIX, bn), lambda n, i: (layer, 0, 0, n))]
        + [gspec(j) for j in range(N_BRANCH)],
        out_specs=pl.BlockSpec((bm, bn), lambda n, i: (i, n)),
        out_shape=jax.ShapeDtypeStruct((m, D_MODEL), BF16),
        compiler_params=_params(2),
        name="branch_merge",
    )(*ys, w_branch, gates, gates, gates, gates)


def _ffn_gu_kernel(h_ref, wg_ref, wu_ref, o_ref, wg_scr, wu_scr):
    @pl.when(pl.program_id(1) == 0)
    def _():
        _cast_weight_block(wg_scr, wg_ref)
        _cast_weight_block(wu_scr, wu_ref)

    h = h_ref[...]
    g = jnp.dot(h, wg_scr[...], preferred_element_type=F32)
    u = jnp.dot(h, wu_scr[...], preferred_element_type=F32)
    o_ref[...] = (g * _sigmoid(g) * u).astype(o_ref.dtype)


def _ffn_gate_up(h, wg, wu, *, layer, bm=1024, bn=256):
    m, k = h.shape
    f = wg.shape[2]
    bm = min(bm, m)
    wspec = pl.BlockSpec((None, k, bn), lambda j, i: (layer, 0, j))
    return pl.pallas_call(
        _ffn_gu_kernel,
        grid=(f // bn, m // bm),
        in_specs=[pl.BlockSpec((bm, k), lambda j, i: (i, 0)), wspec, wspec],
        out_specs=pl.BlockSpec((bm, bn), lambda j, i: (i, j)),
        out_shape=jax.ShapeDtypeStruct((m, f), BF16),
        scratch_shapes=[pltpu.VMEM((k, bn), BF16)] * 2,
        compiler_params=_params(2),
        name="ffn_gate_up",
    )(h, wg, wu)


def kernel(x, norm_mix, w_in, conv_a, sgu_norm, sgu_w, sgu_b, q_norm, k_norm, ssm_conv_w, ssm_conv_b, ssm_dt_bias, ssm_a_log, ssm_d, ssm_norm, w_branch, w_out, norm_ffn, w_ffn_gate, w_ffn_up, w_ffn_down):
    batch, seq, d = x.shape
    m = batch * seq
    depth = w_in.shape[0]
    xr = x.reshape(m, d)
    q_fold = (1.0 / math.sqrt(SB_HEAD_DIM)) * (1.0 / math.log(2.0))
    w_in_t = jnp.swapaxes(w_in, 1, 2)
    w_branch_bf = w_branch.astype(BF16)
    w_out_bf = w_out.astype(BF16)
    w_down_bf = w_ffn_down.astype(BF16)
    for l in range(depth):
        hn = _rmsnorm(xr, norm_mix[l])
        proj = _in_proj(hn, w_in_t, layer=l, row0=0, shift=0, n=N_MIX, bm=1024, bn=512,
                        out_dtype=BF16, name="in_proj_mix")
        dt_raw = _in_proj(hn, w_in_t, layer=l, row0=N_MIX, shift=0, n=LANES, bm=1024, bn=LANES,
                          out_dtype=F32, valid_cols=SSM_HEADS, name="in_proj_dt")
        gates = _in_proj(hn, w_in_t, layer=l, row0=N_MIX, shift=GATE_SHIFT, n=N_BRANCH * d, bm=1024,
                         bn=512, out_dtype=BF16, act="sigmoid", name="in_proj_gates")

        y_a = _mixer_a(proj, conv_a[l], seq)
        y_b = _mixer_b(proj, sgu_norm[l], sgu_w[l], sgu_b[l])
        qn, kn = _qknorm(proj, q_norm[l] * q_fold, k_norm[l])
        y_c = _attention(qn, kn, proj, batch, seq)
        y_d = _mixer_d(proj, dt_raw, ssm_conv_w[l], ssm_conv_b[l], ssm_dt_bias[l], ssm_a_log[l],
                       ssm_d[l], ssm_norm[l], batch, seq)

        merged = _merge((y_a, y_b, y_c, y_d), w_branch_bf, gates, layer=l)
        xr = _matmul(merged, w_out_bf, layer=l, bm=512, bn=1024, out_dtype=F32, res=xr,
                     name="out_proj")

        hn = _rmsnorm(xr, norm_ffn[l])
        act = _ffn_gate_up(hn, w_ffn_gate, w_ffn_up, layer=l)
        xr = _matmul(act, w_down_bf, layer=l, bm=512, bn=512, out_dtype=F32, res=xr,
                     m_outer=True, name="ffn_down")
    return xr.reshape(batch, seq, d)
```

```python
import functools
import math

import jax
import jax.numpy as jnp
from jax import lax
from jax.experimental import pallas as pl
from jax.experimental.pallas import tpu as pltpu

F32 = jnp.float32
BF16 = jnp.bfloat16

EPS = 1e-6
D_MODEL = 4096
W_MIX = D_MODEL // 4
GROUP = 128
CHUNK = 128
CONV_A = 3
SGU_GROUPS = W_MIX // GROUP
SB_HEAD_DIM = 128
SB_HEADS = W_MIX // SB_HEAD_DIM
SSM_HEAD_DIM = 64
SSM_HEADS = W_MIX // SSM_HEAD_DIM
SSM_GROUPS = 2
SSM_HPG = SSM_HEADS // SSM_GROUPS
SSM_STATE = 128
SSM_CONV = 4
SSM_CONV_DIM = W_MIX + 2 * SSM_GROUPS * SSM_STATE
N_BRANCH = 4
N_MIX = 9 * W_MIX + SSM_CONV_DIM
GATE_SHIFT = SSM_HEADS
LANES = 128
BF16_SUBLANES = 16
VMEM_LIMIT = 52 * 1024 * 1024

COL_A_B, COL_A_C, COL_A_X = 0, 1, 2
COL_B_U, COL_B_V = 3, 4
COL_C_Q, COL_C_K, COL_C_V = 5, 6, 7
COL_D_Z = 8
COL_D_XBC = (9 * W_MIX) // SSM_CONV_DIM


def _params(n_axes):
    return pltpu.CompilerParams(
        dimension_semantics=("arbitrary",) * n_axes, vmem_limit_bytes=VMEM_LIMIT)


def _sigmoid(x):
    return 1.0 / (1.0 + jnp.exp(-x))


def _softplus(x):
    return jnp.maximum(x, 0.0) + jnp.log1p(jnp.exp(-jnp.abs(x)))


def _gelu_tanh(x):
    c = math.sqrt(2.0 / math.pi)
    return x * (0.5 * (1.0 + jnp.tanh(c * (x + 0.044715 * (x * x * x)))))


def _norm_prep_kernel(x_ref, g_ref, xb_ref, s_ref):
    x = x_ref[...]
    ms = jnp.mean(x * x, axis=-1, keepdims=True)
    xb_ref[...] = (x * g_ref[...]).astype(xb_ref.dtype)
    s_ref[...] = jnp.broadcast_to(lax.rsqrt(ms + EPS), s_ref.shape)


def _norm_prep(x, g, tm=256):
    m, d = x.shape
    return pl.pallas_call(
        _norm_prep_kernel,
        grid=(m // tm,),
        in_specs=[pl.BlockSpec((tm, d), lambda i: (i, 0)),
                  pl.BlockSpec((1, d), lambda i: (0, 0))],
        out_specs=[pl.BlockSpec((tm, d), lambda i: (i, 0)),
                   pl.BlockSpec((tm, LANES), lambda i: (i, 0))],
        out_shape=[jax.ShapeDtypeStruct((m, d), BF16), jax.ShapeDtypeStruct((m, LANES), F32)],
        compiler_params=_params(1),
        name="norm_prep",
    )(x, g.reshape(1, d))


def _row_scale(acc, s):
    reps = acc.shape[1] // LANES
    return acc * (s if reps == 1 else jnp.concatenate([s] * reps, axis=1))


CAST_ROWS = 512


def _cast_weight_block(dst_scr, src_ref):
    k = dst_scr.shape[0]
    rows_per_step = min(CAST_ROWS, k)

    def body(s, carry):
        rows = pl.ds(pl.multiple_of(s * rows_per_step, rows_per_step), rows_per_step)
        dst_scr[rows, :] = src_ref[rows, :].astype(dst_scr.dtype)
        return carry

    lax.fori_loop(0, k // rows_per_step, body, 0)


def _resid_mm_kernel(*refs, norm, n_steps, n_cols):
    a_ref, w_ref, r_ref = refs[0], refs[1], refs[2]
    acc = jnp.dot(a_ref[...], w_ref[...], preferred_element_type=F32) + r_ref[...]
    if not norm:
        refs[3][...] = acc
        return
    g_ref, o_ref, xb_ref, s_ref = refs[3], refs[4], refs[5], refs[6]
    j = pl.program_id(1)
    o_ref[...] = acc
    xb_ref[...] = (acc * g_ref[...]).astype(xb_ref.dtype)
    sq = acc * acc
    part = sq[:, 0:LANES]
    for c in range(1, acc.shape[1] // LANES):
        part = part + sq[:, c * LANES:(c + 1) * LANES]

    @pl.when(j == 0)
    def _():
        s_ref[...] = part

    @pl.when(j != 0)
    def _():
        s_ref[...] += part

    @pl.when(j == n_steps - 1)
    def _():
        tot = jnp.sum(s_ref[...], axis=1, keepdims=True)
        s_ref[...] = jnp.broadcast_to(lax.rsqrt(tot * (1.0 / n_cols) + EPS), s_ref.shape)


def _resid_matmul(a, w, res, g_next, *, layer, bm, bn, name):
    m, k = a.shape
    n = w.shape[2]
    bm = min(bm, m)
    norm = g_next is not None
    ij = lambda i, j: (i, j)
    in_specs = [pl.BlockSpec((bm, k), lambda i, j: (i, 0)),
                pl.BlockSpec((None, k, bn), lambda i, j: (layer, 0, j)),
                pl.BlockSpec((bm, bn), ij)]
    args = [a, w, res]
    out_specs = [pl.BlockSpec((bm, bn), ij)]
    out_shape = [jax.ShapeDtypeStruct((m, n), F32)]
    if norm:
        in_specs.append(pl.BlockSpec((1, bn), lambda i, j: (0, j)))
        args.append(g_next.reshape(1, n))
        out_specs += [pl.BlockSpec((bm, bn), ij), pl.BlockSpec((bm, LANES), lambda i, j: (i, 0))]
        out_shape += [jax.ShapeDtypeStruct((m, n), BF16), jax.ShapeDtypeStruct((m, LANES), F32)]
    out = pl.pallas_call(
        functools.partial(_resid_mm_kernel, norm=norm, n_steps=n // bn, n_cols=n),
        grid=(m // bm, n // bn),
        in_specs=in_specs,
        out_specs=out_specs,
        out_shape=out_shape,
        compiler_params=_params(2),
        name=name,
    )(*args)
    return out if norm else out[0]


CAST_ROWS_T = 128


def _in_proj_kernel(*refs, shift, act, valid_cols):
    a_ref, s_ref, wm_ref = refs[0], refs[1], refs[2]
    o_ref, w_scr = refs[-2], refs[-1]
    rows = w_scr.shape[0]

    @pl.when(pl.program_id(1) == 0)
    def _():
        step = min(CAST_ROWS_T, rows)
        for c in range(rows // step):
            lo = c * step + shift
            if lo + step <= rows:
                src = wm_ref[lo:lo + step, :]
            else:
                src = jnp.concatenate([wm_ref[lo:rows, :], refs[3][0:lo + step - rows, :]], axis=0)
            w_scr[c * step:(c + 1) * step, :] = src.astype(w_scr.dtype)

    acc = lax.dot_general(a_ref[...], w_scr[...], (((1,), (1,)), ((), ())),
                          preferred_element_type=F32)
    acc = _row_scale(acc, s_ref[...])
    if act == "sigmoid":
        acc = _sigmoid(acc)
    if valid_cols is not None:
        lane = lax.broadcasted_iota(jnp.int32, acc.shape, 1)
        acc = jnp.where(lane < valid_cols, acc, 0.0)
    o_ref[...] = acc.astype(o_ref.dtype)


def _in_proj(a, s, w_t, *, layer, row0, shift, n, bm, bn, out_dtype, act=None, valid_cols=None, name):
    m, k = a.shape
    bm = min(bm, m)
    in_specs = [pl.BlockSpec((bm, k), lambda j, i: (i, 0)),
                pl.BlockSpec((bm, LANES), lambda j, i: (i, 0)),
                pl.BlockSpec((None, bn, k), lambda j, i: (layer, row0 // bn + j, 0))]
    args = [a, s, w_t]
    if shift:
        in_specs.append(pl.BlockSpec(
            (None, shift, k), lambda j, i: (layer, (row0 + (j + 1) * bn) // shift, 0)))
        args.append(w_t)
    return pl.pallas_call(
        functools.partial(_in_proj_kernel, shift=shift, act=act, valid_cols=valid_cols),
        grid=(n // bn, m // bm),
        in_specs=in_specs,
        out_specs=pl.BlockSpec((bm, bn), lambda j, i: (i, j)),
        out_shape=jax.ShapeDtypeStruct((m, n), out_dtype),
        scratch_shapes=[pltpu.VMEM((bn, k), BF16)],
        compiler_params=_params(2),
        name=name,
    )(*args)


def _conv_a_kernel(b_ref, c_ref, x_ref, ch_ref, xh_ref, w_ref, o_ref, p_scr, *, blocks_per_seq):
    i = pl.program_id(0)
    tm = b_ref.shape[0]
    h = BF16_SUBLANES
    p = c_ref[...].astype(F32) * x_ref[...].astype(F32)
    ph = ch_ref[...].astype(F32) * xh_ref[...].astype(F32)
    ph = jnp.where(i % blocks_per_seq == 0, 0.0, ph)
    p_scr[0:h, :] = ph
    p_scr[h:h + tm, :] = p
    w = w_ref[...]
    y = w[2:3, :] * p + w[1:2, :] * p_scr[h - 1:h - 1 + tm, :] + w[0:1, :] * p_scr[h - 2:h - 2 + tm, :]
    o_ref[...] = (b_ref[...].astype(F32) * y).astype(o_ref.dtype)


def _mixer_a(proj, conv_w, seq, tm=512):
    m = proj.shape[0]
    tm = min(tm, seq)
    h = BF16_SUBLANES
    halo = lambda col: pl.BlockSpec(
        (h, W_MIX), lambda i: (jnp.maximum(i * (tm // h) - 1, 0), col))
    main = lambda col: pl.BlockSpec((tm, W_MIX), lambda i: (i, col))
    return pl.pallas_call(
        functools.partial(_conv_a_kernel, blocks_per_seq=seq // tm),
        grid=(m // tm,),
        in_specs=[main(COL_A_B), main(COL_A_C), main(COL_A_X), halo(COL_A_C), halo(COL_A_X),
                  pl.BlockSpec((CONV_A, W_MIX), lambda i: (0, 0))],
        out_specs=pl.BlockSpec((tm, W_MIX), lambda i: (i, 0)),
        out_shape=jax.ShapeDtypeStruct((m, W_MIX), BF16),
        scratch_shapes=[pltpu.VMEM((tm + h, W_MIX), F32)],
        compiler_params=_params(1),
        name="mixer_a_conv",
    )(proj, proj, proj, proj, proj, conv_w)


def _sgu_kernel(u_ref, v_ref, ng_ref, w_ref, bexp_ref, o_ref):
    tm = u_ref.shape[0]
    vf = _gelu_tanh(v_ref[...].astype(F32))
    mu = jnp.mean(vf, axis=-1, keepdims=True)
    d = vf - mu
    var = jnp.mean(d * d, axis=-1, keepdims=True)
    vn = (d * lax.rsqrt(var + EPS) * ng_ref[...]).astype(BF16)
    t_idx = lax.broadcasted_iota(jnp.int32, (CHUNK, CHUNK), 0)
    s_idx = lax.broadcasted_iota(jnp.int32, (CHUNK, CHUNK), 1)
    causal = s_idx <= t_idx
    ws = [jnp.where(causal, w_ref[g], 0.0).astype(BF16) for g in range(SGU_GROUPS)]
    bexp = bexp_ref[...]
    for c in range(tm // CHUNK):
        rows = slice(c * CHUNK, (c + 1) * CHUNK)
        vn_c = vn[rows, :]
        mixed = jnp.concatenate(
            [jnp.dot(ws[g], vn_c[:, g * GROUP:(g + 1) * GROUP], preferred_element_type=F32)
             for g in range(SGU_GROUPS)], axis=1)
        u = _gelu_tanh(u_ref[rows, :].astype(F32))
        o_ref[rows, :] = (u * (mixed + bexp)).astype(o_ref.dtype)


def _mixer_b(proj, norm_g, w_s, b_s, tm=512):
    m = proj.shape[0]
    tm = min(tm, m)
    bexp = jnp.repeat(b_s.T, GROUP, axis=1)
    return pl.pallas_call(
        _sgu_kernel,
        grid=(m // tm,),
        in_specs=[pl.BlockSpec((tm, W_MIX), lambda i: (i, COL_B_U)),
                  pl.BlockSpec((tm, W_MIX), lambda i: (i, COL_B_V)),
                  pl.BlockSpec((1, W_MIX), lambda i: (0, 0)),
                  pl.BlockSpec((SGU_GROUPS, CHUNK, CHUNK), lambda i: (0, 0, 0)),
                  pl.BlockSpec((CHUNK, W_MIX), lambda i: (0, 0))],
        out_specs=pl.BlockSpec((tm, W_MIX), lambda i: (i, 0)),
        out_shape=jax.ShapeDtypeStruct((m, W_MIX), BF16),
        compiler_params=_params(1),
        name="mixer_b_sgu",
    )(proj, proj, norm_g.reshape(1, W_MIX), w_s, bexp)


def _qknorm_kernel(q_ref, k_ref, qg_ref, kg_ref, qo_ref, ko_ref):
    for src, g_ref, dst in ((q_ref, qg_ref, qo_ref), (k_ref, kg_ref, ko_ref)):
        g = g_ref[...]
        for h in range(SB_HEADS):
            cols = slice(h * SB_HEAD_DIM, (h + 1) * SB_HEAD_DIM)
            x = src[:, cols].astype(F32)
            ms = jnp.mean(x * x, axis=-1, keepdims=True)
            dst[:, cols] = (x * lax.rsqrt(ms + EPS) * g).astype(dst.dtype)


def _qknorm(proj, q_g, k_g, tm=512):
    m = proj.shape[0]
    tm = min(tm, m)
    spec = lambda col: pl.BlockSpec((tm, W_MIX), lambda i: (i, col))
    gspec = pl.BlockSpec((1, SB_HEAD_DIM), lambda i: (0, 0))
    return pl.pallas_call(
        _qknorm_kernel,
        grid=(m // tm,),
        in_specs=[spec(COL_C_Q), spec(COL_C_K), gspec, gspec],
        out_specs=[spec(0), spec(0)],
        out_shape=[jax.ShapeDtypeStruct((m, W_MIX), BF16)] * 2,
        compiler_params=_params(1),
        name="qk_norm",
    )(proj, proj, q_g.reshape(1, -1), k_g.reshape(1, -1))


EXP2_ZERO_BELOW = -160.0


def _attn_kernel(q_ref, k_ref, v_ref, o_ref, acc_scr, *, tq, sub):
    i = pl.program_id(2)
    q = q_ref[...]
    jj = lax.broadcasted_iota(jnp.int32, (sub, sub), 0)
    ss = lax.broadcasted_iota(jnp.int32, (sub, sub), 1)
    uneg = jnp.where(jj > ss, -1.0, 0.0).astype(BF16)

    def tile(off, width, r, mask):
        nsb = width // sub
        pv = None
        for c in reversed(range(nsb)):
            rows = pl.ds(off + c * sub, sub)
            z2 = lax.dot_general(q, k_ref[rows, :], (((1,), (1,)), ((), ())),
                                 preferred_element_type=F32)
            neg_abs = pltpu.bitcast(pltpu.bitcast(z2, jnp.int32) | jnp.int32(-2 ** 31), F32)
            l2 = jnp.log(1.0 + jnp.exp2(neg_abs)) * (1.0 / math.log(2.0))
            sp = jnp.maximum(z2, 0.0) + l2
            lb = jnp.minimum(z2, 0.0) - l2
            if mask is not None:
                sp = jnp.where(mask[:, c * sub:(c + 1) * sub], sp, 0.0)
            rev = jnp.dot(sp.astype(BF16), uneg, preferred_element_type=F32)
            a = jnp.exp2(lb + rev + r)
            if mask is not None:
                a = jnp.where(mask[:, c * sub:(c + 1) * sub], a, 0.0)
            t = jnp.dot(a.astype(BF16), v_ref[rows, :], preferred_element_type=F32)
            pv = t if pv is None else pv + t
            r = r - jnp.sum(sp, axis=1, keepdims=True)
        acc_scr[...] += pv
        return r

    acc_scr[...] = jnp.zeros_like(acc_scr)
    row = lax.broadcasted_iota(jnp.int32, (tq, tq), 0)
    col = lax.broadcasted_iota(jnp.int32, (tq, tq), 1)
    r = tile(pl.multiple_of(i * tq, tq), tq, jnp.zeros((tq, 1), F32), col < row)

    def cond(carry):
        step, _, r_max = carry
        return jnp.logical_and(step < i, r_max > EXP2_ZERO_BELOW)

    def body(carry):
        step, r, _ = carry
        r = tile(pl.multiple_of((i - 1 - step) * tq, tq), tq, r, None)
        return step + 1, r, jnp.max(r)

    lax.while_loop(cond, body, (jnp.int32(0), r, jnp.max(r)))
    o_ref[...] = acc_scr[...].astype(o_ref.dtype)


def _attention(qn, kn, proj, batch, seq, tq=512, sub=256):
    m = qn.shape[0]
    tq = min(tq, seq)
    nq = seq // tq
    v_col0 = COL_C_V * (W_MIX // SB_HEAD_DIM)
    return pl.pallas_call(
        functools.partial(_attn_kernel, tq=tq, sub=sub),
        grid=(batch, SB_HEADS, nq),
        in_specs=[pl.BlockSpec((tq, SB_HEAD_DIM), lambda b, h, i: (b * nq + i, h)),
                  pl.BlockSpec((seq, SB_HEAD_DIM), lambda b, h, i: (b, h)),
                  pl.BlockSpec((seq, SB_HEAD_DIM), lambda b, h, i: (b, v_col0 + h))],
        out_specs=pl.BlockSpec((tq, SB_HEAD_DIM), lambda b, h, i: (b * nq + i, h)),
        out_shape=jax.ShapeDtypeStruct((m, W_MIX), BF16),
        scratch_shapes=[pltpu.VMEM((tq, SB_HEAD_DIM), F32)],
        compiler_params=_params(3),
        name="stick_breaking_attn",
    )(qn, kn, proj)


def _split3(x):
    h1 = x.astype(BF16)
    r1 = x - h1.astype(F32)
    h2 = r1.astype(BF16)
    h3 = (r1 - h2.astype(F32)).astype(BF16)
    return [h1, h2, h3]


def _ssd_kernel(z_ref, xbc_ref, halo_ref, dt_ref, cw_ref, cb_ref, dtb_ref, alog_ref, dexp_ref,
                ng_ref, xmat_ref, o_ref, ht_scr, xp_scr):
    c = pl.program_id(1)
    L = CHUNK
    h = BF16_SUBLANES
    gw = W_MIX // SSM_GROUPS

    @pl.when(c == 0)
    def _():
        ht_scr[...] = jnp.zeros_like(ht_scr)

    xp_scr[0:h, :] = jnp.where(c == 0, 0.0, halo_ref[...].astype(F32))
    xp_scr[h:h + L, :] = xbc_ref[...].astype(F32)
    cw = cw_ref[...]
    conv = cb_ref[...]
    for k in range(SSM_CONV):
        s0 = h - (SSM_CONV - 1) + k
        conv = conv + cw[k:k + 1, :] * xp_scr[s0:s0 + L, :]
    xc = conv * _sigmoid(conv)
    xs = xc[:, :W_MIX]
    bmat = xc[:, W_MIX:W_MIX + SSM_GROUPS * SSM_STATE].astype(BF16)
    cmat = xc[:, W_MIX + SSM_GROUPS * SSM_STATE:].astype(BF16)

    dt = _softplus(dt_ref[...] + dtb_ref[...])
    da = dt * (-jnp.exp(alog_ref[...]))
    t_i = lax.broadcasted_iota(jnp.int32, (L, L), 0)
    s_i = lax.broadcasted_iota(jnp.int32, (L, L), 1)
    causal = s_i <= t_i
    tri3 = jnp.concatenate([causal.astype(BF16)] * 3, axis=1)
    cum = jnp.dot(tri3, jnp.concatenate(_split3(da), axis=0), preferred_element_type=F32)
    cum_t = cum.T
    xmat3 = xmat_ref[...]
    dt_e = jnp.dot(jnp.concatenate(_split3(dt), axis=1), xmat3, preferred_element_type=F32)
    cum_e = jnp.dot(jnp.concatenate(_split3(cum), axis=1), xmat3, preferred_element_type=F32)
    cl_e = cum_e[L - 1:L, :]
    xdt = xs * dt_e
    xdec = (xdt * jnp.exp(cl_e - cum_e)).astype(BF16)
    ecum_e = jnp.exp(cum_e)
    chunk_dec = jnp.exp(cl_e)
    lane = lax.broadcasted_iota(jnp.int32, (L, LANES), 1)
    left = lane < SSM_HEAD_DIM

    ys = []
    for g in range(SSM_GROUPS):
        bg = bmat[:, g * SSM_STATE:(g + 1) * SSM_STATE]
        cg = cmat[:, g * SSM_STATE:(g + 1) * SSM_STATE]
        gcols = slice(g * gw, (g + 1) * gw)
        cb = lax.dot_general(cg, bg, (((1,), (1,)), ((), ())), preferred_element_type=F32)
        ht = ht_scr[g]
        y_off = jnp.dot(cg, ht.astype(BF16), preferred_element_type=F32)
        parts = []
        for pr in range(SSM_HPG // 2):
            ms = []
            for r in range(2):
                idx = g * SSM_HPG + 2 * pr + r
                seg = cum[:, idx:idx + 1] - cum_t[idx:idx + 1, :]
                dec = jnp.exp(jnp.where(causal, seg, -jnp.inf))
                ms.append((cb * dec).astype(BF16))
            c0 = g * gw + pr * LANES
            xpair = xdt[:, c0:c0 + LANES]
            rhs = jnp.concatenate(
                [jnp.where(left, xpair, 0.0), jnp.where(left, 0.0, xpair)], axis=0).astype(BF16)
            parts.append(jnp.dot(jnp.concatenate(ms, axis=1), rhs, preferred_element_type=F32))
        y_diag = jnp.concatenate(parts, axis=1)
        ys.append(y_diag + y_off * ecum_e[:, gcols])
        st = lax.dot_general(bg, xdec[:, gcols], (((0,), (0,)), ((), ())),
                             preferred_element_type=F32)
        ht_scr[g] = ht * chunk_dec[:, gcols] + st

    y = jnp.concatenate(ys, axis=1) + xs * dexp_ref[...]
    z = z_ref[...].astype(F32)
    y = y * (z * _sigmoid(z))
    outs = []
    for g in range(SSM_GROUPS):
        yg = y[:, g * gw:(g + 1) * gw]
        ms = jnp.mean(yg * yg, axis=-1, keepdims=True)
        outs.append(yg * lax.rsqrt(ms + EPS))
    o_ref[...] = (jnp.concatenate(outs, axis=1) * ng_ref[...]).astype(o_ref.dtype)


def _mixer_d(proj, dt_raw, conv_w, conv_b, dt_bias, a_log, d_skip, norm_g, batch, seq):
    m = proj.shape[0]
    nc = seq // CHUNK
    h = BF16_SUBLANES
    pad = LANES - SSM_HEADS
    row = lambda v: v.reshape(1, -1)
    xmat = (jnp.arange(W_MIX)[None, :] // SSM_HEAD_DIM == jnp.arange(LANES)[:, None]).astype(BF16)
    xmat = jnp.concatenate([xmat] * 3, axis=0)
    const = lambda shape: pl.BlockSpec(shape, lambda b, c: (0,) * len(shape))
    return pl.pallas_call(
        _ssd_kernel,
        grid=(batch, nc),
        in_specs=[pl.BlockSpec((CHUNK, W_MIX), lambda b, c: (b * nc + c, COL_D_Z)),
                  pl.BlockSpec((CHUNK, SSM_CONV_DIM), lambda b, c: (b * nc + c, COL_D_XBC)),
                  pl.BlockSpec((h, SSM_CONV_DIM),
                               lambda b, c: (jnp.maximum((b * nc + c) * (CHUNK // h) - 1, 0), COL_D_XBC)),
                  pl.BlockSpec((CHUNK, LANES), lambda b, c: (b * nc + c, 0)),
                  const((SSM_CONV, SSM_CONV_DIM)), const((1, SSM_CONV_DIM)),
                  const((1, LANES)), const((1, LANES)), const((1, W_MIX)), const((1, W_MIX)),
                  const((3 * LANES, W_MIX))],
        out_specs=pl.BlockSpec((CHUNK, W_MIX), lambda b, c: (b * nc + c, 0)),
        out_shape=jax.ShapeDtypeStruct((m, W_MIX), BF16),
        scratch_shapes=[pltpu.VMEM((SSM_GROUPS, SSM_STATE, W_MIX // SSM_GROUPS), F32),
                        pltpu.VMEM((CHUNK + h, SSM_CONV_DIM), F32)],
        compiler_params=_params(2),
        name="mixer_d_ssd",
    )(proj, proj, proj, dt_raw, conv_w, row(conv_b),
      row(jnp.pad(dt_bias, (0, pad))), row(jnp.pad(a_log, (0, pad))),
      row(jnp.repeat(d_skip, SSM_HEAD_DIM)), row(norm_g), xmat)


def _merge_kernel(ya_ref, yb_ref, yc_ref, yd_ref, wb_ref, g0_ref, g1_ref, g2_ref, g3_ref, o_ref):
    acc = None
    for j, (y_ref, g_ref) in enumerate(
            zip((ya_ref, yb_ref, yc_ref, yd_ref), (g0_ref, g1_ref, g2_ref, g3_ref))):
        t = jnp.dot(y_ref[...], wb_ref[j], preferred_element_type=F32) * g_ref[...].astype(F32)
        acc = t if acc is None else acc + t
    o_ref[...] = acc.astype(o_ref.dtype)


def _merge(ys, w_branch, gates, *, layer, bm=512, bn=1024):
    m = gates.shape[0]
    bm = min(bm, m)
    nb = D_MODEL // bn
    yspec = pl.BlockSpec((bm, W_MIX), lambda n, i: (i, 0))
    gspec = lambda j: pl.BlockSpec((bm, bn), lambda n, i: (i, j * nb + n))
    return pl.pallas_call(
        _merge_kernel,
        grid=(nb, m // bm),
        in_specs=[yspec] * N_BRANCH
        + [pl.BlockSpec((None, N_BRANCH, W_MIX, bn), lambda n, i: (layer, 0, 0, n))]
        + [gspec(j) for j in range(N_BRANCH)],
        out_specs=pl.BlockSpec((bm, bn), lambda n, i: (i, n)),
        out_shape=jax.ShapeDtypeStruct((m, D_MODEL), BF16),
        compiler_params=_params(2),
        name="branch_merge",
    )(*ys, w_branch, gates, gates, gates, gates)


def _ffn_gu_kernel(h_ref, s_ref, wg_ref, wu_ref, o_ref, wg_scr, wu_scr):
    @pl.when(pl.program_id(1) == 0)
    def _():
        _cast_weight_block(wg_scr, wg_ref)
        _cast_weight_block(wu_scr, wu_ref)

    h = h_ref[...]
    s = s_ref[...]
    g = _row_scale(jnp.dot(h, wg_scr[...], preferred_element_type=F32), s)
    u = _row_scale(jnp.dot(h, wu_scr[...], preferred_element_type=F32), s)
    o_ref[...] = (g * _sigmoid(g) * u).astype(o_ref.dtype)


def _ffn_gate_up(h, s, wg, wu, *, layer, bm=1024, bn=256):
    m, k = h.shape
    f = wg.shape[2]
    bm = min(bm, m)
    wspec = pl.BlockSpec((None, k, bn), lambda j, i: (layer, 0, j))
    return pl.pallas_call(
        _ffn_gu_kernel,
        grid=(f // bn, m // bm),
        in_specs=[pl.BlockSpec((bm, k), lambda j, i: (i, 0)),
                  pl.BlockSpec((bm, LANES), lambda j, i: (i, 0)), wspec, wspec],
        out_specs=pl.BlockSpec((bm, bn), lambda j, i: (i, j)),
        out_shape=jax.ShapeDtypeStruct((m, f), BF16),
        scratch_shapes=[pltpu.VMEM((k, bn), BF16)] * 2,
        compiler_params=_params(2),
        name="ffn_gate_up",
    )(h, s, wg, wu)


def kernel(x, norm_mix, w_in, conv_a, sgu_norm, sgu_w, sgu_b, q_norm, k_norm, ssm_conv_w, ssm_conv_b, ssm_dt_bias, ssm_a_log, ssm_d, ssm_norm, w_branch, w_out, norm_ffn, w_ffn_gate, w_ffn_up, w_ffn_down):
    batch, seq, d = x.shape
    m = batch * seq
    depth = w_in.shape[0]
    xr = x.reshape(m, d)
    q_fold = (1.0 / math.sqrt(SB_HEAD_DIM)) * (1.0 / math.log(2.0))
    w_in_t = jnp.swapaxes(w_in, 1, 2)
    w_branch_bf = w_branch.astype(BF16)
    w_out_bf = w_out.astype(BF16)
    w_down_bf = w_ffn_down.astype(BF16)
    xb, s = _norm_prep(xr, norm_mix[0])
    for l in range(depth):
        proj = _in_proj(xb, s, w_in_t, layer=l, row0=0, shift=0, n=N_MIX, bm=1024, bn=512,
                        out_dtype=BF16, name="in_proj_mix")
        dt_raw = _in_proj(xb, s, w_in_t, layer=l, row0=N_MIX, shift=0, n=LANES, bm=1024, bn=LANES,
                          out_dtype=F32, valid_cols=SSM_HEADS, name="in_proj_dt")
        gates = _in_proj(xb, s, w_in_t, layer=l, row0=N_MIX, shift=GATE_SHIFT, n=N_BRANCH * d, bm=1024,
                         bn=512, out_dtype=BF16, act="sigmoid", name="in_proj_gates")

        y_a = _mixer_a(proj, conv_a[l], seq)
        y_b = _mixer_b(proj, sgu_norm[l], sgu_w[l], sgu_b[l])
        qn, kn = _qknorm(proj, q_norm[l] * q_fold, k_norm[l])
        y_c = _attention(qn, kn, proj, batch, seq)
        y_d = _mixer_d(proj, dt_raw, ssm_conv_w[l], ssm_conv_b[l], ssm_dt_bias[l], ssm_a_log[l],
                       ssm_d[l], ssm_norm[l], batch, seq)

        merged = _merge((y_a, y_b, y_c, y_d), w_branch_bf, gates, layer=l)
        xr, xb, s = _resid_matmul(merged, w_out_bf, xr, norm_ffn[l], layer=l, bm=1024, bn=512,
                                  name="out_proj")

        act = _ffn_gate_up(xb, s, w_ffn_gate, w_ffn_up, layer=l)
        if l + 1 < depth:
            xr, xb, s = _resid_matmul(act, w_down_bf, xr, norm_mix[l + 1], layer=l, bm=512, bn=512,
                                      name="ffn_down")
        else:
            xr = _resid_matmul(act, w_down_bf, xr, None, layer=l, bm=512, bn=512, name="ffn_down")
    return xr.reshape(batch, seq, d)
```

```python
import functools
import math

import jax
import jax.numpy as jnp
from jax import lax
from jax.experimental import pallas as pl
from jax.experimental.pallas import tpu as pltpu

F32 = jnp.float32
BF16 = jnp.bfloat16

EPS = 1e-6
D_MODEL = 4096
W_MIX = D_MODEL // 4
GROUP = 128
CHUNK = 128
CONV_A = 3
SGU_GROUPS = W_MIX // GROUP
SB_HEAD_DIM = 128
SB_HEADS = W_MIX // SB_HEAD_DIM
SSM_HEAD_DIM = 64
SSM_HEADS = W_MIX // SSM_HEAD_DIM
SSM_GROUPS = 2
SSM_HPG = SSM_HEADS // SSM_GROUPS
SSM_STATE = 128
SSM_CONV = 4
SSM_CONV_DIM = W_MIX + 2 * SSM_GROUPS * SSM_STATE
N_BRANCH = 4
N_MIX = 9 * W_MIX + SSM_CONV_DIM
GATE_SHIFT = SSM_HEADS
LANES = 128
BF16_SUBLANES = 16
VMEM_LIMIT = 52 * 1024 * 1024

COL_A_B, COL_A_C, COL_A_X = 0, 1, 2
COL_B_U, COL_B_V = 3, 4
COL_C_Q, COL_C_K, COL_C_V = 5, 6, 7
COL_D_Z = 8
COL_D_XBC = (9 * W_MIX) // SSM_CONV_DIM


def _params(n_axes):
    return pltpu.CompilerParams(
        dimension_semantics=("arbitrary",) * n_axes, vmem_limit_bytes=VMEM_LIMIT)


def _sigmoid(x):
    return 1.0 / (1.0 + jnp.exp(-x))


def _softplus(x):
    return jnp.maximum(x, 0.0) + jnp.log1p(jnp.exp(-jnp.abs(x)))


def _gelu_tanh(x):
    c = math.sqrt(2.0 / math.pi)
    return x * (0.5 * (1.0 + jnp.tanh(c * (x + 0.044715 * (x * x * x)))))


def _norm_prep_kernel(x_ref, g_ref, xb_ref, s_ref):
    x = x_ref[...]
    ms = jnp.mean(x * x, axis=-1, keepdims=True)
    xb_ref[...] = (x * g_ref[...]).astype(xb_ref.dtype)
    s_ref[...] = jnp.broadcast_to(lax.rsqrt(ms + EPS), s_ref.shape)


def _norm_prep(x, g, tm=256):
    m, d = x.shape
    return pl.pallas_call(
        _norm_prep_kernel,
        grid=(m // tm,),
        in_specs=[pl.BlockSpec((tm, d), lambda i: (i, 0)),
                  pl.BlockSpec((1, d), lambda i: (0, 0))],
        out_specs=[pl.BlockSpec((tm, d), lambda i: (i, 0)),
                   pl.BlockSpec((tm, LANES), lambda i: (i, 0))],
        out_shape=[jax.ShapeDtypeStruct((m, d), BF16), jax.ShapeDtypeStruct((m, LANES), F32)],
        compiler_params=_params(1),
        name="norm_prep",
    )(x, g.reshape(1, d))


ROW_CHUNKS = 4


def _row_scale(acc, s):
    reps = acc.shape[1] // LANES
    return acc * (s if reps == 1 else jnp.concatenate([s] * reps, axis=1))


CAST_ROWS = 512


def _cast_weight_block(dst_scr, src_ref):
    k = dst_scr.shape[0]
    rows_per_step = min(CAST_ROWS, k)

    def body(s, carry):
        rows = pl.ds(pl.multiple_of(s * rows_per_step, rows_per_step), rows_per_step)
        dst_scr[rows, :] = src_ref[rows, :].astype(dst_scr.dtype)
        return carry

    lax.fori_loop(0, k // rows_per_step, body, 0)


def _resid_mm_kernel(*refs, norm, n_steps, n_cols):
    a_ref, w_ref, r_ref = refs[0], refs[1], refs[2]
    acc = jnp.dot(a_ref[...], w_ref[...], preferred_element_type=F32) + r_ref[...]
    if not norm:
        refs[3][...] = acc
        return
    g_ref, o_ref, xb_ref, s_ref = refs[3], refs[4], refs[5], refs[6]
    j = pl.program_id(1)
    o_ref[...] = acc
    xb_ref[...] = (acc * g_ref[...]).astype(xb_ref.dtype)
    sq = acc * acc
    part = sq[:, 0:LANES]
    for c in range(1, acc.shape[1] // LANES):
        part = part + sq[:, c * LANES:(c + 1) * LANES]

    @pl.when(j == 0)
    def _():
        s_ref[...] = part

    @pl.when(j != 0)
    def _():
        s_ref[...] += part

    @pl.when(j == n_steps - 1)
    def _():
        tot = jnp.sum(s_ref[...], axis=1, keepdims=True)
        s_ref[...] = jnp.broadcast_to(lax.rsqrt(tot * (1.0 / n_cols) + EPS), s_ref.shape)


def _resid_matmul(a, w, res, g_next, *, layer, bm, bn, name):
    m, k = a.shape
    n = w.shape[2]
    bm = min(bm, m)
    norm = g_next is not None
    ij = lambda i, j: (i, j)
    in_specs = [pl.BlockSpec((bm, k), lambda i, j: (i, 0)),
                pl.BlockSpec((None, k, bn), lambda i, j: (layer, 0, j)),
                pl.BlockSpec((bm, bn), ij)]
    args = [a, w, res]
    out_specs = [pl.BlockSpec((bm, bn), ij)]
    out_shape = [jax.ShapeDtypeStruct((m, n), F32)]
    if norm:
        in_specs.append(pl.BlockSpec((1, bn), lambda i, j: (0, j)))
        args.append(g_next.reshape(1, n))
        out_specs += [pl.BlockSpec((bm, bn), ij), pl.BlockSpec((bm, LANES), lambda i, j: (i, 0))]
        out_shape += [jax.ShapeDtypeStruct((m, n), BF16), jax.ShapeDtypeStruct((m, LANES), F32)]
    out = pl.pallas_call(
        functools.partial(_resid_mm_kernel, norm=norm, n_steps=n // bn, n_cols=n),
        grid=(m // bm, n // bn),
        in_specs=in_specs,
        out_specs=out_specs,
        out_shape=out_shape,
        compiler_params=_params(2),
        name=name,
    )(*args)
    return out if norm else out[0]


CAST_ROWS_T = 128


def _in_proj_kernel(*refs, shift, act, valid_cols):
    a_ref, s_ref, wm_ref = refs[0], refs[1], refs[2]
    o_ref, w_scr = refs[-2], refs[-1]
    rows = w_scr.shape[0]

    @pl.when(pl.program_id(1) == 0)
    def _():
        step = min(CAST_ROWS_T, rows)
        for c in range(rows // step):
            lo = c * step + shift
            if lo + step <= rows:
                src = wm_ref[lo:lo + step, :]
            else:
                src = jnp.concatenate([wm_ref[lo:rows, :], refs[3][0:lo + step - rows, :]], axis=0)
            w_scr[c * step:(c + 1) * step, :] = src.astype(w_scr.dtype)

    acc = lax.dot_general(a_ref[...], w_scr[...], (((1,), (1,)), ((), ())),
                          preferred_element_type=F32)
    acc = _row_scale(acc, s_ref[...])
    if act == "sigmoid":
        acc = _sigmoid(acc)
    if valid_cols is not None:
        lane = lax.broadcasted_iota(jnp.int32, acc.shape, 1)
        acc = jnp.where(lane < valid_cols, acc, 0.0)
    o_ref[...] = acc.astype(o_ref.dtype)


def _in_proj(a, s, w_t, *, layer, row0, shift, n, bm, bn, out_dtype, act=None, valid_cols=None, name):
    m, k = a.shape
    bm = min(bm, m)
    in_specs = [pl.BlockSpec((bm, k), lambda j, i: (i, 0)),
                pl.BlockSpec((bm, LANES), lambda j, i: (i, 0)),
                pl.BlockSpec((None, bn, k), lambda j, i: (layer, row0 // bn + j, 0))]
    args = [a, s, w_t]
    if shift:
        in_specs.append(pl.BlockSpec(
            (None, shift, k), lambda j, i: (layer, (row0 + (j + 1) * bn) // shift, 0)))
        args.append(w_t)
    return pl.pallas_call(
        functools.partial(_in_proj_kernel, shift=shift, act=act, valid_cols=valid_cols),
        grid=(n // bn, m // bm),
        in_specs=in_specs,
        out_specs=pl.BlockSpec((bm, bn), lambda j, i: (i, j)),
        out_shape=jax.ShapeDtypeStruct((m, n), out_dtype),
        scratch_shapes=[pltpu.VMEM((bn, k), BF16)],
        compiler_params=_params(2),
        name=name,
    )(*args)


def _WARMUP_OUT_INDEX(j, i):
    return jnp.where(j == 0, 0, i), jnp.maximum(j - 1, 0)


def _in_proj_stream_kernel(*refs, shift, act, n_blocks):
    a_ref, s_ref, wc_ref = refs[0], refs[1], refs[2]
    o_ref, w_scr = refs[-2], refs[-1]
    j, i = pl.program_id(0), pl.program_id(1)
    chunk = wc_ref.shape[0]
    slot = j % 2

    @pl.when(j < n_blocks)
    def _():
        src = wc_ref[...]
        if shift:
            src = jnp.concatenate([src[shift:, :], refs[3][...]], axis=0)
        w_scr[slot, pl.ds(pl.multiple_of(i * chunk, chunk), chunk), :] = src.astype(w_scr.dtype)

    @pl.when(j == 0)
    def _():
        o_ref[...] = jnp.zeros_like(o_ref)

    @pl.when(j > 0)
    def _():
        rows_per = a_ref.shape[0] // ROW_CHUNKS
        w = w_scr[1 - slot]
        for c in range(ROW_CHUNKS):
            rows = slice(c * rows_per, (c + 1) * rows_per)
            acc = lax.dot_general(a_ref[rows, :], w, (((1,), (1,)), ((), ())),
                                  preferred_element_type=F32)
            acc = _row_scale(acc, s_ref[rows, :])
            if act == "sigmoid":
                acc = _sigmoid(acc)
            o_ref[rows, :] = acc.astype(o_ref.dtype)


def _in_proj_stream(a, s, w_t, *, layer, row0, shift, n, bm, bn, out_dtype, act=None, name):
    m, k = a.shape
    bm = min(bm, m)
    nm = m // bm
    nb = n // bn
    chunk = bn // nm
    assert chunk * nm == bn and chunk % BF16_SUBLANES == 0 and row0 % chunk == 0
    wblk = lambda j: jnp.minimum(j, nb - 1)
    in_specs = [pl.BlockSpec((bm, k), lambda j, i: (i, 0)),
                pl.BlockSpec((bm, LANES), lambda j, i: (i, 0)),
                pl.BlockSpec((None, chunk, k), lambda j, i: (layer, row0 // chunk + wblk(j) * nm + i, 0))]
    args = [a, s, w_t]
    if shift:
        in_specs.append(pl.BlockSpec(
            (None, shift, k),
            lambda j, i: (layer, (row0 + wblk(j) * bn + (i + 1) * chunk) // shift, 0)))
        args.append(w_t)
    return pl.pallas_call(
        functools.partial(_in_proj_stream_kernel, shift=shift, act=act, n_blocks=nb),
        grid=(nb + 1, nm),
        in_specs=in_specs,
        out_specs=pl.BlockSpec((bm, bn), _WARMUP_OUT_INDEX),
        out_shape=jax.ShapeDtypeStruct((m, n), out_dtype),
        scratch_shapes=[pltpu.VMEM((2, bn, k), BF16)],
        compiler_params=_params(2),
        name=name,
    )(*args)


def _conv_a_kernel(b_ref, c_ref, x_ref, ch_ref, xh_ref, w_ref, o_ref, p_scr, *, blocks_per_seq):
    i = pl.program_id(0)
    tm = b_ref.shape[0]
    h = BF16_SUBLANES
    p = c_ref[...].astype(F32) * x_ref[...].astype(F32)
    ph = ch_ref[...].astype(F32) * xh_ref[...].astype(F32)
    ph = jnp.where(i % blocks_per_seq == 0, 0.0, ph)
    p_scr[0:h, :] = ph
    p_scr[h:h + tm, :] = p
    w = w_ref[...]
    y = w[2:3, :] * p + w[1:2, :] * p_scr[h - 1:h - 1 + tm, :] + w[0:1, :] * p_scr[h - 2:h - 2 + tm, :]
    o_ref[...] = (b_ref[...].astype(F32) * y).astype(o_ref.dtype)


def _mixer_a(proj, conv_w, seq, tm=512):
    m = proj.shape[0]
    tm = min(tm, seq)
    h = BF16_SUBLANES
    halo = lambda col: pl.BlockSpec(
        (h, W_MIX), lambda i: (jnp.maximum(i * (tm // h) - 1, 0), col))
    main = lambda col: pl.BlockSpec((tm, W_MIX), lambda i: (i, col))
    return pl.pallas_call(
        functools.partial(_conv_a_kernel, blocks_per_seq=seq // tm),
        grid=(m // tm,),
        in_specs=[main(COL_A_B), main(COL_A_C), main(COL_A_X), halo(COL_A_C), halo(COL_A_X),
                  pl.BlockSpec((CONV_A, W_MIX), lambda i: (0, 0))],
        out_specs=pl.BlockSpec((tm, W_MIX), lambda i: (i, 0)),
        out_shape=jax.ShapeDtypeStruct((m, W_MIX), BF16),
        scratch_shapes=[pltpu.VMEM((tm + h, W_MIX), F32)],
        compiler_params=_params(1),
        name="mixer_a_conv",
    )(proj, proj, proj, proj, proj, conv_w)


def _sgu_kernel(u_ref, v_ref, ng_ref, w_ref, bexp_ref, o_ref):
    tm = u_ref.shape[0]
    vf = _gelu_tanh(v_ref[...].astype(F32))
    mu = jnp.mean(vf, axis=-1, keepdims=True)
    d = vf - mu
    var = jnp.mean(d * d, axis=-1, keepdims=True)
    vn = (d * lax.rsqrt(var + EPS) * ng_ref[...]).astype(BF16)
    t_idx = lax.broadcasted_iota(jnp.int32, (CHUNK, CHUNK), 0)
    s_idx = lax.broadcasted_iota(jnp.int32, (CHUNK, CHUNK), 1)
    causal = s_idx <= t_idx
    ws = [jnp.where(causal, w_ref[g], 0.0).astype(BF16) for g in range(SGU_GROUPS)]
    bexp = bexp_ref[...]
    for c in range(tm // CHUNK):
        rows = slice(c * CHUNK, (c + 1) * CHUNK)
        vn_c = vn[rows, :]
        mixed = jnp.concatenate(
            [jnp.dot(ws[g], vn_c[:, g * GROUP:(g + 1) * GROUP], preferred_element_type=F32)
             for g in range(SGU_GROUPS)], axis=1)
        u = _gelu_tanh(u_ref[rows, :].astype(F32))
        o_ref[rows, :] = (u * (mixed + bexp)).astype(o_ref.dtype)


def _mixer_b(proj, norm_g, w_s, b_s, tm=512):
    m = proj.shape[0]
    tm = min(tm, m)
    bexp = jnp.repeat(b_s.T, GROUP, axis=1)
    return pl.pallas_call(
        _sgu_kernel,
        grid=(m // tm,),
        in_specs=[pl.BlockSpec((tm, W_MIX), lambda i: (i, COL_B_U)),
                  pl.BlockSpec((tm, W_MIX), lambda i: (i, COL_B_V)),
                  pl.BlockSpec((1, W_MIX), lambda i: (0, 0)),
                  pl.BlockSpec((SGU_GROUPS, CHUNK, CHUNK), lambda i: (0, 0, 0)),
                  pl.BlockSpec((CHUNK, W_MIX), lambda i: (0, 0))],
        out_specs=pl.BlockSpec((tm, W_MIX), lambda i: (i, 0)),
        out_shape=jax.ShapeDtypeStruct((m, W_MIX), BF16),
        compiler_params=_params(1),
        name="mixer_b_sgu",
    )(proj, proj, norm_g.reshape(1, W_MIX), w_s, bexp)


def _qknorm_kernel(q_ref, k_ref, qg_ref, kg_ref, qo_ref, ko_ref):
    for src, g_ref, dst in ((q_ref, qg_ref, qo_ref), (k_ref, kg_ref, ko_ref)):
        g = g_ref[...]
        for h in range(SB_HEADS):
            cols = slice(h * SB_HEAD_DIM, (h + 1) * SB_HEAD_DIM)
            x = src[:, cols].astype(F32)
            ms = jnp.mean(x * x, axis=-1, keepdims=True)
            dst[:, cols] = (x * lax.rsqrt(ms + EPS) * g).astype(dst.dtype)


def _qknorm(proj, q_g, k_g, tm=512):
    m = proj.shape[0]
    tm = min(tm, m)
    spec = lambda col: pl.BlockSpec((tm, W_MIX), lambda i: (i, col))
    gspec = pl.BlockSpec((1, SB_HEAD_DIM), lambda i: (0, 0))
    return pl.pallas_call(
        _qknorm_kernel,
        grid=(m // tm,),
        in_specs=[spec(COL_C_Q), spec(COL_C_K), gspec, gspec],
        out_specs=[spec(0), spec(0)],
        out_shape=[jax.ShapeDtypeStruct((m, W_MIX), BF16)] * 2,
        compiler_params=_params(1),
        name="qk_norm",
    )(proj, proj, q_g.reshape(1, -1), k_g.reshape(1, -1))


EXP2_ZERO_BELOW = -160.0


def _attn_kernel(q_ref, k_ref, v_ref, o_ref, acc_scr, *, tq, sub):
    i = pl.program_id(2)
    q = q_ref[...]
    jj = lax.broadcasted_iota(jnp.int32, (sub, sub), 0)
    ss = lax.broadcasted_iota(jnp.int32, (sub, sub), 1)
    uneg = jnp.where(jj > ss, -1.0, 0.0).astype(BF16)

    def tile(off, width, r, mask):
        nsb = width // sub
        pv = None
        for c in reversed(range(nsb)):
            rows = pl.ds(off + c * sub, sub)
            z2 = lax.dot_general(q, k_ref[rows, :], (((1,), (1,)), ((), ())),
                                 preferred_element_type=F32)
            neg_abs = pltpu.bitcast(pltpu.bitcast(z2, jnp.int32) | jnp.int32(-2 ** 31), F32)
            l2 = jnp.log(1.0 + jnp.exp2(neg_abs)) * (1.0 / math.log(2.0))
            sp = jnp.maximum(z2, 0.0) + l2
            lb = jnp.minimum(z2, 0.0) - l2
            if mask is not None:
                sp = jnp.where(mask[:, c * sub:(c + 1) * sub], sp, 0.0)
            rev = jnp.dot(sp.astype(BF16), uneg, preferred_element_type=F32)
            a = jnp.exp2(lb + rev + r)
            if mask is not None:
                a = jnp.where(mask[:, c * sub:(c + 1) * sub], a, 0.0)
            t = jnp.dot(a.astype(BF16), v_ref[rows, :], preferred_element_type=F32)
            pv = t if pv is None else pv + t
            r = r - jnp.sum(sp, axis=1, keepdims=True)
        acc_scr[...] += pv
        return r

    acc_scr[...] = jnp.zeros_like(acc_scr)
    row = lax.broadcasted_iota(jnp.int32, (tq, tq), 0)
    col = lax.broadcasted_iota(jnp.int32, (tq, tq), 1)
    r = tile(pl.multiple_of(i * tq, tq), tq, jnp.zeros((tq, 1), F32), col < row)

    def cond(carry):
        step, _, r_max = carry
        return jnp.logical_and(step < i, r_max > EXP2_ZERO_BELOW)

    def body(carry):
        step, r, _ = carry
        r = tile(pl.multiple_of((i - 1 - step) * tq, tq), tq, r, None)
        return step + 1, r, jnp.max(r)

    lax.while_loop(cond, body, (jnp.int32(0), r, jnp.max(r)))
    o_ref[...] = acc_scr[...].astype(o_ref.dtype)


def _attention(qn, kn, proj, batch, seq, tq=512, sub=256):
    m = qn.shape[0]
    tq = min(tq, seq)
    nq = seq // tq
    v_col0 = COL_C_V * (W_MIX // SB_HEAD_DIM)
    return pl.pallas_call(
        functools.partial(_attn_kernel, tq=tq, sub=sub),
        grid=(batch, SB_HEADS, nq),
        in_specs=[pl.BlockSpec((tq, SB_HEAD_DIM), lambda b, h, i: (b * nq + i, h)),
                  pl.BlockSpec((seq, SB_HEAD_DIM), lambda b, h, i: (b, h)),
                  pl.BlockSpec((seq, SB_HEAD_DIM), lambda b, h, i: (b, v_col0 + h))],
        out_specs=pl.BlockSpec((tq, SB_HEAD_DIM), lambda b, h, i: (b * nq + i, h)),
        out_shape=jax.ShapeDtypeStruct((m, W_MIX), BF16),
        scratch_shapes=[pltpu.VMEM((tq, SB_HEAD_DIM), F32)],
        compiler_params=_params(3),
        name="stick_breaking_attn",
    )(qn, kn, proj)


def _split3(x):
    h1 = x.astype(BF16)
    r1 = x - h1.astype(F32)
    h2 = r1.astype(BF16)
    h3 = (r1 - h2.astype(F32)).astype(BF16)
    return [h1, h2, h3]


def _ssd_kernel(z_ref, xbc_ref, halo_ref, dt_ref, cw_ref, cb_ref, dtb_ref, alog_ref, dexp_ref,
                ng_ref, xmat_ref, o_ref, ht_scr, xp_scr):
    c = pl.program_id(1)
    L = CHUNK
    h = BF16_SUBLANES
    gw = W_MIX // SSM_GROUPS

    @pl.when(c == 0)
    def _():
        ht_scr[...] = jnp.zeros_like(ht_scr)

    xp_scr[0:h, :] = jnp.where(c == 0, 0.0, halo_ref[...].astype(F32))
    xp_scr[h:h + L, :] = xbc_ref[...].astype(F32)
    cw = cw_ref[...]
    conv = cb_ref[...]
    for k in range(SSM_CONV):
        s0 = h - (SSM_CONV - 1) + k
        conv = conv + cw[k:k + 1, :] * xp_scr[s0:s0 + L, :]
    xc = conv * _sigmoid(conv)
    xs = xc[:, :W_MIX]
    bmat = xc[:, W_MIX:W_MIX + SSM_GROUPS * SSM_STATE].astype(BF16)
    cmat = xc[:, W_MIX + SSM_GROUPS * SSM_STATE:].astype(BF16)

    dt = _softplus(dt_ref[...] + dtb_ref[...])
    da = dt * (-jnp.exp(alog_ref[...]))
    t_i = lax.broadcasted_iota(jnp.int32, (L, L), 0)
    s_i = lax.broadcasted_iota(jnp.int32, (L, L), 1)
    causal = s_i <= t_i
    tri3 = jnp.concatenate([causal.astype(BF16)] * 3, axis=1)
    cum = jnp.dot(tri3, jnp.concatenate(_split3(da), axis=0), preferred_element_type=F32)
    cum_t = cum.T
    xmat3 = xmat_ref[...]
    dt_e = jnp.dot(jnp.concatenate(_split3(dt), axis=1), xmat3, preferred_element_type=F32)
    cum_e = jnp.dot(jnp.concatenate(_split3(cum), axis=1), xmat3, preferred_element_type=F32)
    cl_e = cum_e[L - 1:L, :]
    xdt = xs * dt_e
    xdec = (xdt * jnp.exp(cl_e - cum_e)).astype(BF16)
    ecum_e = jnp.exp(cum_e)
    chunk_dec = jnp.exp(cl_e)
    lane = lax.broadcasted_iota(jnp.int32, (L, LANES), 1)
    left = lane < SSM_HEAD_DIM

    ys = []
    for g in range(SSM_GROUPS):
        bg = bmat[:, g * SSM_STATE:(g + 1) * SSM_STATE]
        cg = cmat[:, g * SSM_STATE:(g + 1) * SSM_STATE]
        gcols = slice(g * gw, (g + 1) * gw)
        cb = lax.dot_general(cg, bg, (((1,), (1,)), ((), ())), preferred_element_type=F32)
        ht = ht_scr[g]
        y_off = jnp.dot(cg, ht.astype(BF16), preferred_element_type=F32)
        parts = []
        for pr in range(SSM_HPG // 2):
            ms = []
            for r in range(2):
                idx = g * SSM_HPG + 2 * pr + r
                seg = cum[:, idx:idx + 1] - cum_t[idx:idx + 1, :]
                dec = jnp.exp(jnp.where(causal, seg, -jnp.inf))
                ms.append((cb * dec).astype(BF16))
            c0 = g * gw + pr * LANES
            xpair = xdt[:, c0:c0 + LANES]
            rhs = jnp.concatenate(
                [jnp.where(left, xpair, 0.0), jnp.where(left, 0.0, xpair)], axis=0).astype(BF16)
            parts.append(jnp.dot(jnp.concatenate(ms, axis=1), rhs, preferred_element_type=F32))
        y_diag = jnp.concatenate(parts, axis=1)
        ys.append(y_diag + y_off * ecum_e[:, gcols])
        st = lax.dot_general(bg, xdec[:, gcols], (((0,), (0,)), ((), ())),
                             preferred_element_type=F32)
        ht_scr[g] = ht * chunk_dec[:, gcols] + st

    y = jnp.concatenate(ys, axis=1) + xs * dexp_ref[...]
    z = z_ref[...].astype(F32)
    y = y * (z * _sigmoid(z))
    outs = []
    for g in range(SSM_GROUPS):
        yg = y[:, g * gw:(g + 1) * gw]
        ms = jnp.mean(yg * yg, axis=-1, keepdims=True)
        outs.append(yg * lax.rsqrt(ms + EPS))
    o_ref[...] = (jnp.concatenate(outs, axis=1) * ng_ref[...]).astype(o_ref.dtype)


def _mixer_d(proj, dt_raw, conv_w, conv_b, dt_bias, a_log, d_skip, norm_g, batch, seq):
    m = proj.shape[0]
    nc = seq // CHUNK
    h = BF16_SUBLANES
    pad = LANES - SSM_HEADS
    row = lambda v: v.reshape(1, -1)
    xmat = (jnp.arange(W_MIX)[None, :] // SSM_HEAD_DIM == jnp.arange(LANES)[:, None]).astype(BF16)
    xmat = jnp.concatenate([xmat] * 3, axis=0)
    const = lambda shape: pl.BlockSpec(shape, lambda b, c: (0,) * len(shape))
    return pl.pallas_call(
        _ssd_kernel,
        grid=(batch, nc),
        in_specs=[pl.BlockSpec((CHUNK, W_MIX), lambda b, c: (b * nc + c, COL_D_Z)),
                  pl.BlockSpec((CHUNK, SSM_CONV_DIM), lambda b, c: (b * nc + c, COL_D_XBC)),
                  pl.BlockSpec((h, SSM_CONV_DIM),
                               lambda b, c: (jnp.maximum((b * nc + c) * (CHUNK // h) - 1, 0), COL_D_XBC)),
                  pl.BlockSpec((CHUNK, LANES), lambda b, c: (b * nc + c, 0)),
                  const((SSM_CONV, SSM_CONV_DIM)), const((1, SSM_CONV_DIM)),
                  const((1, LANES)), const((1, LANES)), const((1, W_MIX)), const((1, W_MIX)),
                  const((3 * LANES, W_MIX))],
        out_specs=pl.BlockSpec((CHUNK, W_MIX), lambda b, c: (b * nc + c, 0)),
        out_shape=jax.ShapeDtypeStruct((m, W_MIX), BF16),
        scratch_shapes=[pltpu.VMEM((SSM_GROUPS, SSM_STATE, W_MIX // SSM_GROUPS), F32),
                        pltpu.VMEM((CHUNK + h, SSM_CONV_DIM), F32)],
        compiler_params=_params(2),
        name="mixer_d_ssd",
    )(proj, proj, proj, dt_raw, conv_w, row(conv_b),
      row(jnp.pad(dt_bias, (0, pad))), row(jnp.pad(a_log, (0, pad))),
      row(jnp.repeat(d_skip, SSM_HEAD_DIM)), row(norm_g), xmat)


def _merge_kernel(ya_ref, yb_ref, yc_ref, yd_ref, wb_ref, g0_ref, g1_ref, g2_ref, g3_ref, o_ref):
    acc = None
    for j, (y_ref, g_ref) in enumerate(
            zip((ya_ref, yb_ref, yc_ref, yd_ref), (g0_ref, g1_ref, g2_ref, g3_ref))):
        t = jnp.dot(y_ref[...], wb_ref[j], preferred_element_type=F32) * g_ref[...].astype(F32)
        acc = t if acc is None else acc + t
    o_ref[...] = acc.astype(o_ref.dtype)


def _merge(ys, w_branch, gates, *, layer, bm=512, bn=1024):
    m = gates.shape[0]
    bm = min(bm, m)
    nb = D_MODEL // bn
    yspec = pl.BlockSpec((bm, W_MIX), lambda n, i: (i, 0))
    gspec = lambda j: pl.BlockSpec((bm, bn), lambda n, i: (i, j * nb + n))
    return pl.pallas_call(
        _merge_kernel,
        grid=(nb, m // bm),
        in_specs=[yspec] * N_BRANCH
        + [pl.BlockSpec((None, N_BRANCH, W_MIX, bn), lambda n, i: (layer, 0, 0, n))]
        + [gspec(j) for j in range(N_BRANCH)],
        out_specs=pl.BlockSpec((bm, bn), lambda n, i: (i, n)),
        out_shape=jax.ShapeDtypeStruct((m, D_MODEL), BF16),
        compiler_params=_params(2),
        name="branch_merge",
    )(*ys, w_branch, gates, gates, gates, gates)


def _ffn_gu_kernel(h_ref, s_ref, wg_ref, wu_ref, o_ref, wg_scr, wu_scr, *, n_blocks):
    j, i = pl.program_id(0), pl.program_id(1)
    chunk = wg_ref.shape[0]
    slot = j % 2

    @pl.when(j < n_blocks)
    def _():
        rows = pl.ds(pl.multiple_of(i * chunk, chunk), chunk)
        wg_scr[slot, rows, :] = wg_ref[...].astype(wg_scr.dtype)
        wu_scr[slot, rows, :] = wu_ref[...].astype(wu_scr.dtype)

    @pl.when(j == 0)
    def _():
        o_ref[...] = jnp.zeros_like(o_ref)

    @pl.when(j > 0)
    def _():
        wg, wu = wg_scr[1 - slot], wu_scr[1 - slot]
        rows_per = h_ref.shape[0] // ROW_CHUNKS
        for c in range(ROW_CHUNKS):
            rows = slice(c * rows_per, (c + 1) * rows_per)
            h = h_ref[rows, :]
            sc = s_ref[rows, :]
            g = _row_scale(jnp.dot(h, wg, preferred_element_type=F32), sc)
            u = _row_scale(jnp.dot(h, wu, preferred_element_type=F32), sc)
            o_ref[rows, :] = (g * _sigmoid(g) * u).astype(o_ref.dtype)


def _ffn_gate_up(h, s, wg, wu, *, layer, bm=2048, bn=256):
    m, k = h.shape
    f = wg.shape[2]
    bm = min(bm, m)
    nm = m // bm
    nb = f // bn
    chunk = k // nm
    assert chunk * nm == k and chunk % BF16_SUBLANES == 0
    wspec = pl.BlockSpec((None, chunk, bn), lambda j, i: (layer, i, jnp.minimum(j, nb - 1)))
    return pl.pallas_call(
        functools.partial(_ffn_gu_kernel, n_blocks=nb),
        grid=(nb + 1, nm),
        in_specs=[pl.BlockSpec((bm, k), lambda j, i: (i, 0)),
                  pl.BlockSpec((bm, LANES), lambda j, i: (i, 0)), wspec, wspec],
        out_specs=pl.BlockSpec((bm, bn), _WARMUP_OUT_INDEX),
        out_shape=jax.ShapeDtypeStruct((m, f), BF16),
        scratch_shapes=[pltpu.VMEM((2, k, bn), BF16)] * 2,
        compiler_params=_params(2),
        name="ffn_gate_up",
    )(h, s, wg, wu)


def kernel(x, norm_mix, w_in, conv_a, sgu_norm, sgu_w, sgu_b, q_norm, k_norm, ssm_conv_w, ssm_conv_b, ssm_dt_bias, ssm_a_log, ssm_d, ssm_norm, w_branch, w_out, norm_ffn, w_ffn_gate, w_ffn_up, w_ffn_down):
    batch, seq, d = x.shape
    m = batch * seq
    depth = w_in.shape[0]
    xr = x.reshape(m, d)
    q_fold = (1.0 / math.sqrt(SB_HEAD_DIM)) * (1.0 / math.log(2.0))
    w_in_t = jnp.swapaxes(w_in, 1, 2)
    w_branch_bf = w_branch.astype(BF16)
    w_out_bf = w_out.astype(BF16)
    w_down_bf = w_ffn_down.astype(BF16)
    xb, s = _norm_prep(xr, norm_mix[0])
    for l in range(depth):
        proj = _in_proj_stream(xb, s, w_in_t, layer=l, row0=0, shift=0, n=N_MIX, bm=1024, bn=768,
                               out_dtype=BF16, name="in_proj_mix")
        dt_raw = _in_proj(xb, s, w_in_t, layer=l, row0=N_MIX, shift=0, n=LANES, bm=1024, bn=LANES,
                          out_dtype=F32, valid_cols=SSM_HEADS, name="in_proj_dt")
        gates = _in_proj_stream(xb, s, w_in_t, layer=l, row0=N_MIX, shift=GATE_SHIFT, n=N_BRANCH * d,
                                bm=1024, bn=1024, out_dtype=BF16, act="sigmoid", name="in_proj_gates")

        y_a = _mixer_a(proj, conv_a[l], seq)
        y_b = _mixer_b(proj, sgu_norm[l], sgu_w[l], sgu_b[l])
        qn, kn = _qknorm(proj, q_norm[l] * q_fold, k_norm[l])
        y_c = _attention(qn, kn, proj, batch, seq)
        y_d = _mixer_d(proj, dt_raw, ssm_conv_w[l], ssm_conv_b[l], ssm_dt_bias[l], ssm_a_log[l],
                       ssm_d[l], ssm_norm[l], batch, seq)

        merged = _merge((y_a, y_b, y_c, y_d), w_branch_bf, gates, layer=l)
        xr, xb, s = _resid_matmul(merged, w_out_bf, xr, norm_ffn[l], layer=l, bm=1024, bn=512,
                                  name="out_proj")

        act = _ffn_gate_up(xb, s, w_ffn_gate, w_ffn_up, layer=l)
        if l + 1 < depth:
            xr, xb, s = _resid_matmul(act, w_down_bf, xr, norm_mix[l + 1], layer=l, bm=512, bn=512,
                                      name="ffn_down")
        else:
            xr = _resid_matmul(act, w_down_bf, xr, None, layer=l, bm=512, bn=512, name="ffn_down")
    return xr.reshape(batch, seq, d)
```

```python
import functools
import math

import jax
import jax.numpy as jnp
from jax import lax
from jax.experimental import pallas as pl
from jax.experimental.pallas import tpu as pltpu

F32 = jnp.float32
BF16 = jnp.bfloat16

EPS = 1e-6
D_MODEL = 4096
W_MIX = D_MODEL // 4
GROUP = 128
CHUNK = 128
CONV_A = 3
SGU_GROUPS = W_MIX // GROUP
SB_HEAD_DIM = 128
SB_HEADS = W_MIX // SB_HEAD_DIM
SSM_HEAD_DIM = 64
SSM_HEADS = W_MIX // SSM_HEAD_DIM
SSM_GROUPS = 2
SSM_HPG = SSM_HEADS // SSM_GROUPS
SSM_STATE = 128
SSM_CONV = 4
SSM_CONV_DIM = W_MIX + 2 * SSM_GROUPS * SSM_STATE
N_BRANCH = 4
N_MIX = 9 * W_MIX + SSM_CONV_DIM
GATE_SHIFT = SSM_HEADS
LANES = 128
BF16_SUBLANES = 16
VMEM_LIMIT = 52 * 1024 * 1024

COL_A_B, COL_A_C, COL_A_X = 0, 1, 2
COL_B_U, COL_B_V = 3, 4
COL_C_Q, COL_C_K, COL_C_V = 5, 6, 7
COL_D_Z = 8
COL_D_XBC = (9 * W_MIX) // SSM_CONV_DIM


def _params(n_axes):
    return pltpu.CompilerParams(
        dimension_semantics=("arbitrary",) * n_axes, vmem_limit_bytes=VMEM_LIMIT)


def _sigmoid(x):
    return 1.0 / (1.0 + jnp.exp(-x))


def _softplus(x):
    return jnp.maximum(x, 0.0) + jnp.log1p(jnp.exp(-jnp.abs(x)))


def _gelu_tanh(x):
    c = math.sqrt(2.0 / math.pi)
    return x * (0.5 * (1.0 + jnp.tanh(c * (x + 0.044715 * (x * x * x)))))


def _norm_prep_kernel(x_ref, g_ref, xb_ref, s_ref):
    x = x_ref[...]
    ms = jnp.mean(x * x, axis=-1, keepdims=True)
    xb_ref[...] = (x * g_ref[...]).astype(xb_ref.dtype)
    s_ref[...] = jnp.broadcast_to(lax.rsqrt(ms + EPS), s_ref.shape)


def _norm_prep(x, g, tm=256):
    m, d = x.shape
    return pl.pallas_call(
        _norm_prep_kernel,
        grid=(m // tm,),
        in_specs=[pl.BlockSpec((tm, d), lambda i: (i, 0)),
                  pl.BlockSpec((1, d), lambda i: (0, 0))],
        out_specs=[pl.BlockSpec((tm, d), lambda i: (i, 0)),
                   pl.BlockSpec((tm, LANES), lambda i: (i, 0))],
        out_shape=[jax.ShapeDtypeStruct((m, d), BF16), jax.ShapeDtypeStruct((m, LANES), F32)],
        compiler_params=_params(1),
        name="norm_prep",
    )(x, g.reshape(1, d))


ROW_CHUNKS = 4


def _row_scale(acc, s):
    reps = acc.shape[1] // LANES
    return acc * (s if reps == 1 else jnp.concatenate([s] * reps, axis=1))


CAST_ROWS = 512


def _cast_weight_block(dst_scr, src_ref):
    k = dst_scr.shape[0]
    rows_per_step = min(CAST_ROWS, k)

    def body(s, carry):
        rows = pl.ds(pl.multiple_of(s * rows_per_step, rows_per_step), rows_per_step)
        dst_scr[rows, :] = src_ref[rows, :].astype(dst_scr.dtype)
        return carry

    lax.fori_loop(0, k // rows_per_step, body, 0)


def _resid_mm_kernel(*refs, norm, n_steps, n_cols):
    a_ref, w_ref, r_ref = refs[0], refs[1], refs[2]
    acc = jnp.dot(a_ref[...], w_ref[...], preferred_element_type=F32) + r_ref[...]
    if not norm:
        refs[3][...] = acc
        return
    g_ref, o_ref, xb_ref, s_ref = refs[3], refs[4], refs[5], refs[6]
    j = pl.program_id(1)
    o_ref[...] = acc
    xb_ref[...] = (acc * g_ref[...]).astype(xb_ref.dtype)
    sq = acc * acc
    part = sq[:, 0:LANES]
    for c in range(1, acc.shape[1] // LANES):
        part = part + sq[:, c * LANES:(c + 1) * LANES]

    @pl.when(j == 0)
    def _():
        s_ref[...] = part

    @pl.when(j != 0)
    def _():
        s_ref[...] += part

    @pl.when(j == n_steps - 1)
    def _():
        tot = jnp.sum(s_ref[...], axis=1, keepdims=True)
        s_ref[...] = jnp.broadcast_to(lax.rsqrt(tot * (1.0 / n_cols) + EPS), s_ref.shape)


def _resid_matmul(a, w, res, g_next, *, bm, bn, name):
    m, k = a.shape
    n = w.shape[1]
    bm = min(bm, m)
    norm = g_next is not None
    ij = lambda i, j: (i, j)
    in_specs = [pl.BlockSpec((bm, k), lambda i, j: (i, 0)),
                pl.BlockSpec((k, bn), lambda i, j: (0, j)),
                pl.BlockSpec((bm, bn), ij)]
    args = [a, w, res]
    out_specs = [pl.BlockSpec((bm, bn), ij)]
    out_shape = [jax.ShapeDtypeStruct((m, n), F32)]
    if norm:
        in_specs.append(pl.BlockSpec((1, bn), lambda i, j: (0, j)))
        args.append(g_next.reshape(1, n))
        out_specs += [pl.BlockSpec((bm, bn), ij), pl.BlockSpec((bm, LANES), lambda i, j: (i, 0))]
        out_shape += [jax.ShapeDtypeStruct((m, n), BF16), jax.ShapeDtypeStruct((m, LANES), F32)]
    out = pl.pallas_call(
        functools.partial(_resid_mm_kernel, norm=norm, n_steps=n // bn, n_cols=n),
        grid=(m // bm, n // bn),
        in_specs=in_specs,
        out_specs=out_specs,
        out_shape=out_shape,
        compiler_params=_params(2),
        name=name,
    )(*args)
    return out if norm else out[0]


CAST_ROWS_T = 128


def _in_proj_kernel(*refs, shift, act, valid_cols):
    a_ref, s_ref, wm_ref = refs[0], refs[1], refs[2]
    o_ref, w_scr = refs[-2], refs[-1]
    rows = w_scr.shape[0]

    @pl.when(pl.program_id(1) == 0)
    def _():
        step = min(CAST_ROWS_T, rows)
        for c in range(rows // step):
            lo = c * step + shift
            if lo + step <= rows:
                src = wm_ref[lo:lo + step, :]
            else:
                src = jnp.concatenate([wm_ref[lo:rows, :], refs[3][0:lo + step - rows, :]], axis=0)
            w_scr[c * step:(c + 1) * step, :] = src.astype(w_scr.dtype)

    acc = lax.dot_general(a_ref[...], w_scr[...], (((1,), (1,)), ((), ())),
                          preferred_element_type=F32)
    acc = _row_scale(acc, s_ref[...])
    if act == "sigmoid":
        acc = _sigmoid(acc)
    if valid_cols is not None:
        lane = lax.broadcasted_iota(jnp.int32, acc.shape, 1)
        acc = jnp.where(lane < valid_cols, acc, 0.0)
    o_ref[...] = acc.astype(o_ref.dtype)


def _in_proj(a, s, w_t, *, layer, row0, shift, n, bm, bn, out_dtype, act=None, valid_cols=None, name):
    m, k = a.shape
    bm = min(bm, m)
    in_specs = [pl.BlockSpec((bm, k), lambda j, i: (i, 0)),
                pl.BlockSpec((bm, LANES), lambda j, i: (i, 0)),
                pl.BlockSpec((None, bn, k), lambda j, i: (layer, row0 // bn + j, 0))]
    args = [a, s, w_t]
    if shift:
        in_specs.append(pl.BlockSpec(
            (None, shift, k), lambda j, i: (layer, (row0 + (j + 1) * bn) // shift, 0)))
        args.append(w_t)
    return pl.pallas_call(
        functools.partial(_in_proj_kernel, shift=shift, act=act, valid_cols=valid_cols),
        grid=(n // bn, m // bm),
        in_specs=in_specs,
        out_specs=pl.BlockSpec((bm, bn), lambda j, i: (i, j)),
        out_shape=jax.ShapeDtypeStruct((m, n), out_dtype),
        scratch_shapes=[pltpu.VMEM((bn, k), BF16)],
        compiler_params=_params(2),
        name=name,
    )(*args)


def _side_cast_specs(w, layer, n_steps, step_of):
    rows, cols = w.shape[1], w.shape[2]
    chunk = rows // n_steps
    assert chunk * n_steps == rows and chunk % BF16_SUBLANES == 0
    return (pl.BlockSpec((None, chunk, cols), lambda *g: (layer, step_of(*g), 0)),
            pl.BlockSpec((chunk, cols), lambda *g: (step_of(*g), 0)),
            jax.ShapeDtypeStruct((rows, cols), BF16))


def _WARMUP_OUT_INDEX(j, i):
    return jnp.where(j == 0, 0, i), jnp.maximum(j - 1, 0)


def _in_proj_stream_kernel(*refs, shift, act, n_blocks, side):
    a_ref, s_ref, wc_ref = refs[0], refs[1], refs[2]
    w_scr = refs[-1]
    o_ref = refs[-3] if side else refs[-2]
    j, i = pl.program_id(0), pl.program_id(1)
    chunk = wc_ref.shape[0]
    slot = j % 2

    @pl.when(j < n_blocks)
    def _():
        src = wc_ref[...]
        if shift:
            src = jnp.concatenate([src[shift:, :], refs[3][...]], axis=0)
        w_scr[slot, pl.ds(pl.multiple_of(i * chunk, chunk), chunk), :] = src.astype(w_scr.dtype)

    @pl.when(j == 0)
    def _():
        o_ref[...] = jnp.zeros_like(o_ref)

    @pl.when(j > 0)
    def _():
        rows_per = a_ref.shape[0] // ROW_CHUNKS
        w = w_scr[1 - slot]
        for c in range(ROW_CHUNKS):
            rows = slice(c * rows_per, (c + 1) * rows_per)
            acc = lax.dot_general(a_ref[rows, :], w, (((1,), (1,)), ((), ())),
                                  preferred_element_type=F32)
            acc = _row_scale(acc, s_ref[rows, :])
            if act == "sigmoid":
                acc = _sigmoid(acc)
            o_ref[rows, :] = acc.astype(o_ref.dtype)
        if side:
            refs[-2][...] = refs[3 + bool(shift)][...].astype(BF16)


def _in_proj_stream(a, s, w_t, *, layer, row0, shift, n, bm, bn, out_dtype, act=None, side_w=None, name):
    m, k = a.shape
    bm = min(bm, m)
    nm = m // bm
    nb = n // bn
    chunk = bn // nm
    assert chunk * nm == bn and chunk % BF16_SUBLANES == 0 and row0 % chunk == 0
    wblk = lambda j: jnp.minimum(j, nb - 1)
    in_specs = [pl.BlockSpec((bm, k), lambda j, i: (i, 0)),
                pl.BlockSpec((bm, LANES), lambda j, i: (i, 0)),
                pl.BlockSpec((None, chunk, k), lambda j, i: (layer, row0 // chunk + wblk(j) * nm + i, 0))]
    args = [a, s, w_t]
    if shift:
        in_specs.append(pl.BlockSpec(
            (None, shift, k),
            lambda j, i: (layer, (row0 + wblk(j) * bn + (i + 1) * chunk) // shift, 0)))
        args.append(w_t)
    out_specs = [pl.BlockSpec((bm, bn), _WARMUP_OUT_INDEX)]
    out_shape = [jax.ShapeDtypeStruct((m, n), out_dtype)]
    if side_w is not None:
        side_in, side_out, side_shape = _side_cast_specs(
            side_w, layer, nb * nm, lambda j, i: jnp.where(j == 0, 0, (j - 1) * nm + i))
        in_specs.append(side_in)
        args.append(side_w)
        out_specs.append(side_out)
        out_shape.append(side_shape)
    out = pl.pallas_call(
        functools.partial(_in_proj_stream_kernel, shift=shift, act=act, n_blocks=nb,
                          side=side_w is not None),
        grid=(nb + 1, nm),
        in_specs=in_specs,
        out_specs=out_specs,
        out_shape=out_shape,
        scratch_shapes=[pltpu.VMEM((2, bn, k), BF16)],
        compiler_params=_params(2),
        name=name,
    )(*args)
    return out if side_w is not None else out[0]


def _conv_a_kernel(b_ref, c_ref, x_ref, ch_ref, xh_ref, w_ref, o_ref, p_scr, *, blocks_per_seq):
    i = pl.program_id(0)
    tm = b_ref.shape[0]
    h = BF16_SUBLANES
    p = c_ref[...].astype(F32) * x_ref[...].astype(F32)
    ph = ch_ref[...].astype(F32) * xh_ref[...].astype(F32)
    ph = jnp.where(i % blocks_per_seq == 0, 0.0, ph)
    p_scr[0:h, :] = ph
    p_scr[h:h + tm, :] = p
    w = w_ref[...]
    y = w[2:3, :] * p + w[1:2, :] * p_scr[h - 1:h - 1 + tm, :] + w[0:1, :] * p_scr[h - 2:h - 2 + tm, :]
    o_ref[...] = (b_ref[...].astype(F32) * y).astype(o_ref.dtype)


def _mixer_a(proj, conv_w, seq, tm=512):
    m = proj.shape[0]
    tm = min(tm, seq)
    h = BF16_SUBLANES
    halo = lambda col: pl.BlockSpec(
        (h, W_MIX), lambda i: (jnp.maximum(i * (tm // h) - 1, 0), col))
    main = lambda col: pl.BlockSpec((tm, W_MIX), lambda i: (i, col))
    return pl.pallas_call(
        functools.partial(_conv_a_kernel, blocks_per_seq=seq // tm),
        grid=(m // tm,),
        in_specs=[main(COL_A_B), main(COL_A_C), main(COL_A_X), halo(COL_A_C), halo(COL_A_X),
                  pl.BlockSpec((CONV_A, W_MIX), lambda i: (0, 0))],
        out_specs=pl.BlockSpec((tm, W_MIX), lambda i: (i, 0)),
        out_shape=jax.ShapeDtypeStruct((m, W_MIX), BF16),
        scratch_shapes=[pltpu.VMEM((tm + h, W_MIX), F32)],
        compiler_params=_params(1),
        name="mixer_a_conv",
    )(proj, proj, proj, proj, proj, conv_w)


def _sgu_kernel(u_ref, v_ref, ng_ref, w_ref, bexp_ref, o_ref):
    tm = u_ref.shape[0]
    vf = _gelu_tanh(v_ref[...].astype(F32))
    mu = jnp.mean(vf, axis=-1, keepdims=True)
    d = vf - mu
    var = jnp.mean(d * d, axis=-1, keepdims=True)
    vn = (d * lax.rsqrt(var + EPS) * ng_ref[...]).astype(BF16)
    t_idx = lax.broadcasted_iota(jnp.int32, (CHUNK, CHUNK), 0)
    s_idx = lax.broadcasted_iota(jnp.int32, (CHUNK, CHUNK), 1)
    causal = s_idx <= t_idx
    ws = [jnp.where(causal, w_ref[g], 0.0).astype(BF16) for g in range(SGU_GROUPS)]
    bexp = bexp_ref[...]
    for c in range(tm // CHUNK):
        rows = slice(c * CHUNK, (c + 1) * CHUNK)
        vn_c = vn[rows, :]
        mixed = jnp.concatenate(
            [jnp.dot(ws[g], vn_c[:, g * GROUP:(g + 1) * GROUP], preferred_element_type=F32)
             for g in range(SGU_GROUPS)], axis=1)
        u = _gelu_tanh(u_ref[rows, :].astype(F32))
        o_ref[rows, :] = (u * (mixed + bexp)).astype(o_ref.dtype)


def _mixer_b(proj, norm_g, w_s, b_s, tm=512):
    m = proj.shape[0]
    tm = min(tm, m)
    bexp = jnp.repeat(b_s.T, GROUP, axis=1)
    return pl.pallas_call(
        _sgu_kernel,
        grid=(m // tm,),
        in_specs=[pl.BlockSpec((tm, W_MIX), lambda i: (i, COL_B_U)),
                  pl.BlockSpec((tm, W_MIX), lambda i: (i, COL_B_V)),
                  pl.BlockSpec((1, W_MIX), lambda i: (0, 0)),
                  pl.BlockSpec((SGU_GROUPS, CHUNK, CHUNK), lambda i: (0, 0, 0)),
                  pl.BlockSpec((CHUNK, W_MIX), lambda i: (0, 0))],
        out_specs=pl.BlockSpec((tm, W_MIX), lambda i: (i, 0)),
        out_shape=jax.ShapeDtypeStruct((m, W_MIX), BF16),
        compiler_params=_params(1),
        name="mixer_b_sgu",
    )(proj, proj, norm_g.reshape(1, W_MIX), w_s, bexp)


def _qknorm_kernel(q_ref, k_ref, qg_ref, kg_ref, qo_ref, ko_ref):
    for src, g_ref, dst in ((q_ref, qg_ref, qo_ref), (k_ref, kg_ref, ko_ref)):
        g = g_ref[...]
        for h in range(SB_HEADS):
            cols = slice(h * SB_HEAD_DIM, (h + 1) * SB_HEAD_DIM)
            x = src[:, cols].astype(F32)
            ms = jnp.mean(x * x, axis=-1, keepdims=True)
            dst[:, cols] = (x * lax.rsqrt(ms + EPS) * g).astype(dst.dtype)


def _qknorm(proj, q_g, k_g, tm=512):
    m = proj.shape[0]
    tm = min(tm, m)
    spec = lambda col: pl.BlockSpec((tm, W_MIX), lambda i: (i, col))
    gspec = pl.BlockSpec((1, SB_HEAD_DIM), lambda i: (0, 0))
    return pl.pallas_call(
        _qknorm_kernel,
        grid=(m // tm,),
        in_specs=[spec(COL_C_Q), spec(COL_C_K), gspec, gspec],
        out_specs=[spec(0), spec(0)],
        out_shape=[jax.ShapeDtypeStruct((m, W_MIX), BF16)] * 2,
        compiler_params=_params(1),
        name="qk_norm",
    )(proj, proj, q_g.reshape(1, -1), k_g.reshape(1, -1))


EXP2_ZERO_BELOW = -160.0
R_DEAD = -1e30


def _attn_kernel(q_ref, k_ref, v_ref, o_ref, acc_scr, *, tq, sub):
    i = pl.program_id(2)
    n_groups = tq // sub
    qs = [q_ref[g * sub:(g + 1) * sub, :] for g in range(n_groups)]
    jj = lax.broadcasted_iota(jnp.int32, (sub, sub), 0)
    ss = lax.broadcasted_iota(jnp.int32, (sub, sub), 1)
    uneg = jnp.where(jj > ss, -1.0, 0.0).astype(BF16)

    def tile(q, off, r, mask):
        rows = pl.ds(off, sub)
        z2 = lax.dot_general(q, k_ref[rows, :], (((1,), (1,)), ((), ())),
                             preferred_element_type=F32)
        neg_abs = pltpu.bitcast(pltpu.bitcast(z2, jnp.int32) | jnp.int32(-2 ** 31), F32)
        l2 = jnp.log(1.0 + jnp.exp2(neg_abs)) * (1.0 / math.log(2.0))
        sp = jnp.maximum(z2, 0.0) + l2
        lb = jnp.minimum(z2, 0.0) - l2
        if mask is not None:
            sp = jnp.where(mask, sp, 0.0)
        rev = jnp.dot(sp.astype(BF16), uneg, preferred_element_type=F32)
        a = jnp.exp2(lb + rev + r)
        if mask is not None:
            a = jnp.where(mask, a, 0.0)
        pv = jnp.dot(a.astype(BF16), v_ref[rows, :], preferred_element_type=F32)
        return pv, r - jnp.sum(sp, axis=1, keepdims=True)

    base = i * n_groups
    rs = []
    for g in range(n_groups):
        pv, r = tile(qs[g], pl.multiple_of((base + g) * sub, sub), jnp.zeros((sub, 1), F32), ss < jj)
        acc_scr[g * sub:(g + 1) * sub, :] = pv
        rs.append(r)

    n_steps = base + n_groups - 1

    def cond(carry):
        step, _, r_max = carry
        return jnp.logical_and(step < n_steps, r_max > EXP2_ZERO_BELOW)

    def body(carry):
        step, rs, _ = carry
        new_rs = []
        for g in range(n_groups):
            t_idx = base + g - 1 - step
            r_in = jnp.where(t_idx >= 0, rs[g], R_DEAD)
            pv, r_out = tile(qs[g], pl.multiple_of(jnp.maximum(t_idx, 0) * sub, sub), r_in, None)
            acc_scr[g * sub:(g + 1) * sub, :] += pv
            new_rs.append(r_out)
        r_max = functools.reduce(jnp.maximum, [jnp.max(r) for r in new_rs])
        return step + 1, tuple(new_rs), r_max

    r_max = functools.reduce(jnp.maximum, [jnp.max(r) for r in rs])
    lax.while_loop(cond, body, (jnp.int32(0), tuple(rs), r_max))
    o_ref[...] = acc_scr[...].astype(o_ref.dtype)


def _attention(qn, kn, proj, batch, seq, tq=2048, sub=256):
    m = qn.shape[0]
    tq = min(tq, seq)
    nq = seq // tq
    v_col0 = COL_C_V * (W_MIX // SB_HEAD_DIM)
    return pl.pallas_call(
        functools.partial(_attn_kernel, tq=tq, sub=sub),
        grid=(batch, SB_HEADS, nq),
        in_specs=[pl.BlockSpec((tq, SB_HEAD_DIM), lambda b, h, i: (b * nq + i, h)),
                  pl.BlockSpec((seq, SB_HEAD_DIM), lambda b, h, i: (b, h)),
                  pl.BlockSpec((seq, SB_HEAD_DIM), lambda b, h, i: (b, v_col0 + h))],
        out_specs=pl.BlockSpec((tq, SB_HEAD_DIM), lambda b, h, i: (b * nq + i, h)),
        out_shape=jax.ShapeDtypeStruct((m, W_MIX), BF16),
        scratch_shapes=[pltpu.VMEM((tq, SB_HEAD_DIM), F32)],
        compiler_params=_params(3),
        name="stick_breaking_attn",
    )(qn, kn, proj)


def _split3(x):
    h1 = x.astype(BF16)
    r1 = x - h1.astype(F32)
    h2 = r1.astype(BF16)
    h3 = (r1 - h2.astype(F32)).astype(BF16)
    return [h1, h2, h3]


def _ssd_kernel(z_ref, xbc_ref, halo_ref, dt_ref, cw_ref, cb_ref, dtb_ref, alog_ref, dexp_ref,
                ng_ref, xmat_ref, o_ref, ht_scr, xp_scr):
    c = pl.program_id(1)
    L = CHUNK
    h = BF16_SUBLANES
    gw = W_MIX // SSM_GROUPS

    @pl.when(c == 0)
    def _():
        ht_scr[...] = jnp.zeros_like(ht_scr)

    xp_scr[0:h, :] = jnp.where(c == 0, 0.0, halo_ref[...].astype(F32))
    xp_scr[h:h + L, :] = xbc_ref[...].astype(F32)
    cw = cw_ref[...]
    conv = cb_ref[...]
    for k in range(SSM_CONV):
        s0 = h - (SSM_CONV - 1) + k
        conv = conv + cw[k:k + 1, :] * xp_scr[s0:s0 + L, :]
    xc = conv * _sigmoid(conv)
    xs = xc[:, :W_MIX]
    bmat = xc[:, W_MIX:W_MIX + SSM_GROUPS * SSM_STATE].astype(BF16)
    cmat = xc[:, W_MIX + SSM_GROUPS * SSM_STATE:].astype(BF16)

    dt = _softplus(dt_ref[...] + dtb_ref[...])
    da = dt * (-jnp.exp(alog_ref[...]))
    t_i = lax.broadcasted_iota(jnp.int32, (L, L), 0)
    s_i = lax.broadcasted_iota(jnp.int32, (L, L), 1)
    causal = s_i <= t_i
    tri3 = jnp.concatenate([causal.astype(BF16)] * 3, axis=1)
    cum = jnp.dot(tri3, jnp.concatenate(_split3(da), axis=0), preferred_element_type=F32)
    cum_t = cum.T
    xmat3 = xmat_ref[...]
    dt_e = jnp.dot(jnp.concatenate(_split3(dt), axis=1), xmat3, preferred_element_type=F32)
    cum_e = jnp.dot(jnp.concatenate(_split3(cum), axis=1), xmat3, preferred_element_type=F32)
    cl_e = cum_e[L - 1:L, :]
    xdt = xs * dt_e
    xdec = (xdt * jnp.exp(cl_e - cum_e)).astype(BF16)
    ecum_e = jnp.exp(cum_e)
    chunk_dec = jnp.exp(cl_e)
    lane = lax.broadcasted_iota(jnp.int32, (L, LANES), 1)
    left = lane < SSM_HEAD_DIM

    ys = []
    for g in range(SSM_GROUPS):
        bg = bmat[:, g * SSM_STATE:(g + 1) * SSM_STATE]
        cg = cmat[:, g * SSM_STATE:(g + 1) * SSM_STATE]
        gcols = slice(g * gw, (g + 1) * gw)
        cb = lax.dot_general(cg, bg, (((1,), (1,)), ((), ())), preferred_element_type=F32)
        ht = ht_scr[g]
        y_off = jnp.dot(cg, ht.astype(BF16), preferred_element_type=F32)
        parts = []
        for pr in range(SSM_HPG // 2):
            ms = []
            for r in range(2):
                idx = g * SSM_HPG + 2 * pr + r
                seg = cum[:, idx:idx + 1] - cum_t[idx:idx + 1, :]
                dec = jnp.exp(jnp.where(causal, seg, -jnp.inf))
                ms.append((cb * dec).astype(BF16))
            c0 = g * gw + pr * LANES
            xpair = xdt[:, c0:c0 + LANES]
            rhs = jnp.concatenate(
                [jnp.where(left, xpair, 0.0), jnp.where(left, 0.0, xpair)], axis=0).astype(BF16)
            parts.append(jnp.dot(jnp.concatenate(ms, axis=1), rhs, preferred_element_type=F32))
        y_diag = jnp.concatenate(parts, axis=1)
        ys.append(y_diag + y_off * ecum_e[:, gcols])
        st = lax.dot_general(bg, xdec[:, gcols], (((0,), (0,)), ((), ())),
                             preferred_element_type=F32)
        ht_scr[g] = ht * chunk_dec[:, gcols] + st

    y = jnp.concatenate(ys, axis=1) + xs * dexp_ref[...]
    z = z_ref[...].astype(F32)
    y = y * (z * _sigmoid(z))
    outs = []
    for g in range(SSM_GROUPS):
        yg = y[:, g * gw:(g + 1) * gw]
        ms = jnp.mean(yg * yg, axis=-1, keepdims=True)
        outs.append(yg * lax.rsqrt(ms + EPS))
    o_ref[...] = (jnp.concatenate(outs, axis=1) * ng_ref[...]).astype(o_ref.dtype)


def _mixer_d(proj, dt_raw, conv_w, conv_b, dt_bias, a_log, d_skip, norm_g, batch, seq):
    m = proj.shape[0]
    nc = seq // CHUNK
    h = BF16_SUBLANES
    pad = LANES - SSM_HEADS
    row = lambda v: v.reshape(1, -1)
    xmat = (jnp.arange(W_MIX)[None, :] // SSM_HEAD_DIM == jnp.arange(LANES)[:, None]).astype(BF16)
    xmat = jnp.concatenate([xmat] * 3, axis=0)
    const = lambda shape: pl.BlockSpec(shape, lambda b, c: (0,) * len(shape))
    return pl.pallas_call(
        _ssd_kernel,
        grid=(batch, nc),
        in_specs=[pl.BlockSpec((CHUNK, W_MIX), lambda b, c: (b * nc + c, COL_D_Z)),
                  pl.BlockSpec((CHUNK, SSM_CONV_DIM), lambda b, c: (b * nc + c, COL_D_XBC)),
                  pl.BlockSpec((h, SSM_CONV_DIM),
                               lambda b, c: (jnp.maximum((b * nc + c) * (CHUNK // h) - 1, 0), COL_D_XBC)),
                  pl.BlockSpec((CHUNK, LANES), lambda b, c: (b * nc + c, 0)),
                  const((SSM_CONV, SSM_CONV_DIM)), const((1, SSM_CONV_DIM)),
                  const((1, LANES)), const((1, LANES)), const((1, W_MIX)), const((1, W_MIX)),
                  const((3 * LANES, W_MIX))],
        out_specs=pl.BlockSpec((CHUNK, W_MIX), lambda b, c: (b * nc + c, 0)),
        out_shape=jax.ShapeDtypeStruct((m, W_MIX), BF16),
        scratch_shapes=[pltpu.VMEM((SSM_GROUPS, SSM_STATE, W_MIX // SSM_GROUPS), F32),
                        pltpu.VMEM((CHUNK + h, SSM_CONV_DIM), F32)],
        compiler_params=_params(2),
        name="mixer_d_ssd",
    )(proj, proj, proj, dt_raw, conv_w, row(conv_b),
      row(jnp.pad(dt_bias, (0, pad))), row(jnp.pad(a_log, (0, pad))),
      row(jnp.repeat(d_skip, SSM_HEAD_DIM)), row(norm_g), xmat)


def _merge_kernel(ya_ref, yb_ref, yc_ref, yd_ref, wb_ref, g0_ref, g1_ref, g2_ref, g3_ref, wsrc_ref,
                  o_ref, wdst_ref):
    wdst_ref[...] = wsrc_ref[...].astype(wdst_ref.dtype)
    acc = None
    for j, (y_ref, g_ref) in enumerate(
            zip((ya_ref, yb_ref, yc_ref, yd_ref), (g0_ref, g1_ref, g2_ref, g3_ref))):
        t = jnp.dot(y_ref[...], wb_ref[j], preferred_element_type=F32) * g_ref[...].astype(F32)
        acc = t if acc is None else acc + t
    o_ref[...] = acc.astype(o_ref.dtype)


def _merge(ys, w_branch, gates, side_w, *, layer, bm=512, bn=1024):
    m = gates.shape[0]
    bm = min(bm, m)
    nb = D_MODEL // bn
    nm = m // bm
    side_in, side_out, side_shape = _side_cast_specs(side_w, layer, nb * nm, lambda n, i: n * nm + i)
    yspec = pl.BlockSpec((bm, W_MIX), lambda n, i: (i, 0))
    gspec = lambda j: pl.BlockSpec((bm, bn), lambda n, i: (i, j * nb + n))
    return pl.pallas_call(
        _merge_kernel,
        grid=(nb, m // bm),
        in_specs=[yspec] * N_BRANCH
        + [pl.BlockSpec((N_BRANCH, W_MIX, bn), lambda n, i: (0, 0, n))]
        + [gspec(j) for j in range(N_BRANCH)] + [side_in],
        out_specs=[pl.BlockSpec((bm, bn), lambda n, i: (i, n)), side_out],
        out_shape=[jax.ShapeDtypeStruct((m, D_MODEL), BF16), side_shape],
        compiler_params=_params(2),
        name="branch_merge",
    )(*ys, w_branch, gates, gates, gates, gates, side_w)


def _ffn_gu_kernel(h_ref, s_ref, wg_ref, wu_ref, wsrc_ref, o_ref, wdst_ref, wg_scr, wu_scr, *, n_blocks):
    j, i = pl.program_id(0), pl.program_id(1)
    chunk = wg_ref.shape[0]
    slot = j % 2

    @pl.when(j < n_blocks)
    def _():
        rows = pl.ds(pl.multiple_of(i * chunk, chunk), chunk)
        wg_scr[slot, rows, :] = wg_ref[...].astype(wg_scr.dtype)
        wu_scr[slot, rows, :] = wu_ref[...].astype(wu_scr.dtype)

    @pl.when(j == 0)
    def _():
        o_ref[...] = jnp.zeros_like(o_ref)

    @pl.when(j > 0)
    def _():
        wg, wu = wg_scr[1 - slot], wu_scr[1 - slot]
        rows_per = h_ref.shape[0] // ROW_CHUNKS
        for c in range(ROW_CHUNKS):
            rows = slice(c * rows_per, (c + 1) * rows_per)
            h = h_ref[rows, :]
            sc = s_ref[rows, :]
            g = _row_scale(jnp.dot(h, wg, preferred_element_type=F32), sc)
            u = _row_scale(jnp.dot(h, wu, preferred_element_type=F32), sc)
            o_ref[rows, :] = (g * _sigmoid(g) * u).astype(o_ref.dtype)
        wdst_ref[...] = wsrc_ref[...].astype(wdst_ref.dtype)


def _ffn_gate_up(h, s, wg, wu, side_w, *, layer, bm=2048, bn=256):
    m, k = h.shape
    f = wg.shape[2]
    bm = min(bm, m)
    nm = m // bm
    nb = f // bn
    chunk = k // nm
    assert chunk * nm == k and chunk % BF16_SUBLANES == 0
    wspec = pl.BlockSpec((None, chunk, bn), lambda j, i: (layer, i, jnp.minimum(j, nb - 1)))
    side_in, side_out, side_shape = _side_cast_specs(
        side_w, layer, nb * nm, lambda j, i: jnp.where(j == 0, 0, (j - 1) * nm + i))
    return pl.pallas_call(
        functools.partial(_ffn_gu_kernel, n_blocks=nb),
        grid=(nb + 1, nm),
        in_specs=[pl.BlockSpec((bm, k), lambda j, i: (i, 0)),
                  pl.BlockSpec((bm, LANES), lambda j, i: (i, 0)), wspec, wspec, side_in],
        out_specs=[pl.BlockSpec((bm, bn), _WARMUP_OUT_INDEX), side_out],
        out_shape=[jax.ShapeDtypeStruct((m, f), BF16), side_shape],
        scratch_shapes=[pltpu.VMEM((2, k, bn), BF16)] * 2,
        compiler_params=_params(2),
        name="ffn_gate_up",
    )(h, s, wg, wu, side_w)


def kernel(x, norm_mix, w_in, conv_a, sgu_norm, sgu_w, sgu_b, q_norm, k_norm, ssm_conv_w, ssm_conv_b, ssm_dt_bias, ssm_a_log, ssm_d, ssm_norm, w_branch, w_out, norm_ffn, w_ffn_gate, w_ffn_up, w_ffn_down):
    batch, seq, d = x.shape
    m = batch * seq
    depth = w_in.shape[0]
    xr = x.reshape(m, d)
    q_fold = (1.0 / math.sqrt(SB_HEAD_DIM)) * (1.0 / math.log(2.0))
    w_in_t = jnp.swapaxes(w_in, 1, 2)
    w_branch_2d = w_branch.reshape(depth, N_BRANCH * W_MIX, d)
    xb, s = _norm_prep(xr, norm_mix[0])
    for l in range(depth):
        proj = _in_proj_stream(xb, s, w_in_t, layer=l, row0=0, shift=0, n=N_MIX, bm=1024, bn=768,
                               out_dtype=BF16, name="in_proj_mix")
        dt_raw = _in_proj(xb, s, w_in_t, layer=l, row0=N_MIX, shift=0, n=LANES, bm=1024, bn=LANES,
                          out_dtype=F32, valid_cols=SSM_HEADS, name="in_proj_dt")
        gates, w_branch_bf = _in_proj_stream(
            xb, s, w_in_t, layer=l, row0=N_MIX, shift=GATE_SHIFT, n=N_BRANCH * d, bm=1024, bn=1024,
            out_dtype=BF16, act="sigmoid", side_w=w_branch_2d, name="in_proj_gates")

        y_a = _mixer_a(proj, conv_a[l], seq)
        y_b = _mixer_b(proj, sgu_norm[l], sgu_w[l], sgu_b[l])
        qn, kn = _qknorm(proj, q_norm[l] * q_fold, k_norm[l])
        y_c = _attention(qn, kn, proj, batch, seq)
        y_d = _mixer_d(proj, dt_raw, ssm_conv_w[l], ssm_conv_b[l], ssm_dt_bias[l], ssm_a_log[l],
                       ssm_d[l], ssm_norm[l], batch, seq)

        merged, w_out_bf = _merge((y_a, y_b, y_c, y_d), w_branch_bf.reshape(N_BRANCH, W_MIX, d), gates,
                                  w_out, layer=l)
        xr, xb, s = _resid_matmul(merged, w_out_bf, xr, norm_ffn[l], bm=1024, bn=512, name="out_proj")

        act, w_down_bf = _ffn_gate_up(xb, s, w_ffn_gate, w_ffn_up, w_ffn_down, layer=l)
        if l + 1 < depth:
            xr, xb, s = _resid_matmul(act, w_down_bf, xr, norm_mix[l + 1], bm=512, bn=512,
                                      name="ffn_down")
        else:
            xr = _resid_matmul(act, w_down_bf, xr, None, bm=512, bn=512, name="ffn_down")
    return xr.reshape(batch, seq, d)
```

```python
import functools
import math

import jax
import jax.numpy as jnp
from jax import lax
from jax.experimental import pallas as pl
from jax.experimental.pallas import tpu as pltpu

F32 = jnp.float32
BF16 = jnp.bfloat16

EPS = 1e-6
D_MODEL = 4096
W_MIX = D_MODEL // 4
GROUP = 128
CHUNK = 128
CONV_A = 3
SGU_GROUPS = W_MIX // GROUP
SB_HEAD_DIM = 128
SB_HEADS = W_MIX // SB_HEAD_DIM
SSM_HEAD_DIM = 64
SSM_HEADS = W_MIX // SSM_HEAD_DIM
SSM_GROUPS = 2
SSM_HPG = SSM_HEADS // SSM_GROUPS
SSM_STATE = 128
SSM_CONV = 4
SSM_CONV_DIM = W_MIX + 2 * SSM_GROUPS * SSM_STATE
N_BRANCH = 4
N_MIX = 9 * W_MIX + SSM_CONV_DIM
GATE_SHIFT = SSM_HEADS
LANES = 128
BF16_SUBLANES = 16
VMEM_LIMIT = 52 * 1024 * 1024

COL_A_B, COL_A_C, COL_A_X = 0, 1, 2
COL_B_U, COL_B_V = 3, 4
COL_C_Q, COL_C_K, COL_C_V = 5, 6, 7
COL_D_Z = 8
COL_D_XBC = (9 * W_MIX) // SSM_CONV_DIM


def _params(n_axes):
    return pltpu.CompilerParams(
        dimension_semantics=("arbitrary",) * n_axes, vmem_limit_bytes=VMEM_LIMIT)


def _sigmoid(x):
    return 1.0 / (1.0 + jnp.exp(-x))


def _softplus(x):
    return jnp.maximum(x, 0.0) + jnp.log1p(jnp.exp(-jnp.abs(x)))


def _gelu_tanh(x):
    c = math.sqrt(2.0 / math.pi)
    return x * (0.5 * (1.0 + jnp.tanh(c * (x + 0.044715 * (x * x * x)))))


def _norm_prep_kernel(x_ref, g_ref, xb_ref, s_ref):
    x = x_ref[...]
    ms = jnp.mean(x * x, axis=-1, keepdims=True)
    xb_ref[...] = (x * g_ref[...]).astype(xb_ref.dtype)
    s_ref[...] = jnp.broadcast_to(lax.rsqrt(ms + EPS), s_ref.shape)


def _norm_prep(x, g, tm=256):
    m, d = x.shape
    return pl.pallas_call(
        _norm_prep_kernel,
        grid=(m // tm,),
        in_specs=[pl.BlockSpec((tm, d), lambda i: (i, 0)),
                  pl.BlockSpec((1, d), lambda i: (0, 0))],
        out_specs=[pl.BlockSpec((tm, d), lambda i: (i, 0)),
                   pl.BlockSpec((tm, LANES), lambda i: (i, 0))],
        out_shape=[jax.ShapeDtypeStruct((m, d), BF16), jax.ShapeDtypeStruct((m, LANES), F32)],
        compiler_params=_params(1),
        name="norm_prep",
    )(x, g.reshape(1, d))


ROW_CHUNKS = 4


def _row_scale(acc, s):
    reps = acc.shape[1] // LANES
    return acc * (s if reps == 1 else jnp.concatenate([s] * reps, axis=1))


CAST_ROWS = 512


def _cast_weight_block(dst_scr, src_ref):
    k = dst_scr.shape[0]
    rows_per_step = min(CAST_ROWS, k)

    def body(s, carry):
        rows = pl.ds(pl.multiple_of(s * rows_per_step, rows_per_step), rows_per_step)
        dst_scr[rows, :] = src_ref[rows, :].astype(dst_scr.dtype)
        return carry

    lax.fori_loop(0, k // rows_per_step, body, 0)


def _resid_mm_kernel(*refs, norm, n_steps, n_cols):
    a_ref, w_ref, r_ref = refs[0], refs[1], refs[2]
    acc = jnp.dot(a_ref[...], w_ref[...], preferred_element_type=F32) + r_ref[...]
    if not norm:
        refs[3][...] = acc
        return
    g_ref, o_ref, xb_ref, s_ref = refs[3], refs[4], refs[5], refs[6]
    j = pl.program_id(1)
    o_ref[...] = acc
    xb_ref[...] = (acc * g_ref[...]).astype(xb_ref.dtype)
    sq = acc * acc
    part = sq[:, 0:LANES]
    for c in range(1, acc.shape[1] // LANES):
        part = part + sq[:, c * LANES:(c + 1) * LANES]

    @pl.when(j == 0)
    def _():
        s_ref[...] = part

    @pl.when(j != 0)
    def _():
        s_ref[...] += part

    @pl.when(j == n_steps - 1)
    def _():
        tot = jnp.sum(s_ref[...], axis=1, keepdims=True)
        s_ref[...] = jnp.broadcast_to(lax.rsqrt(tot * (1.0 / n_cols) + EPS), s_ref.shape)


def _resid_matmul(a, w, res, g_next, *, bm, bn, name):
    m, k = a.shape
    n = w.shape[1]
    bm = min(bm, m)
    norm = g_next is not None
    ij = lambda i, j: (i, j)
    in_specs = [pl.BlockSpec((bm, k), lambda i, j: (i, 0)),
                pl.BlockSpec((k, bn), lambda i, j: (0, j)),
                pl.BlockSpec((bm, bn), ij)]
    args = [a, w, res]
    out_specs = [pl.BlockSpec((bm, bn), ij)]
    out_shape = [jax.ShapeDtypeStruct((m, n), F32)]
    if norm:
        in_specs.append(pl.BlockSpec((1, bn), lambda i, j: (0, j)))
        args.append(g_next.reshape(1, n))
        out_specs += [pl.BlockSpec((bm, bn), ij), pl.BlockSpec((bm, LANES), lambda i, j: (i, 0))]
        out_shape += [jax.ShapeDtypeStruct((m, n), BF16), jax.ShapeDtypeStruct((m, LANES), F32)]
    out = pl.pallas_call(
        functools.partial(_resid_mm_kernel, norm=norm, n_steps=n // bn, n_cols=n),
        grid=(m // bm, n // bn),
        in_specs=in_specs,
        out_specs=out_specs,
        out_shape=out_shape,
        compiler_params=_params(2),
        name=name,
    )(*args)
    return out if norm else out[0]


CAST_ROWS_T = 128


def _in_proj_kernel(*refs, shift, act, valid_cols):
    a_ref, s_ref, wm_ref = refs[0], refs[1], refs[2]
    o_ref, w_scr = refs[-2], refs[-1]
    rows = w_scr.shape[0]

    @pl.when(pl.program_id(1) == 0)
    def _():
        step = min(CAST_ROWS_T, rows)
        for c in range(rows // step):
            lo = c * step + shift
            if lo + step <= rows:
                src = wm_ref[lo:lo + step, :]
            else:
                src = jnp.concatenate([wm_ref[lo:rows, :], refs[3][0:lo + step - rows, :]], axis=0)
            w_scr[c * step:(c + 1) * step, :] = src.astype(w_scr.dtype)

    acc = lax.dot_general(a_ref[...], w_scr[...], (((1,), (1,)), ((), ())),
                          preferred_element_type=F32)
    acc = _row_scale(acc, s_ref[...])
    if act == "sigmoid":
        acc = _sigmoid(acc)
    if valid_cols is not None:
        lane = lax.broadcasted_iota(jnp.int32, acc.shape, 1)
        acc = jnp.where(lane < valid_cols, acc, 0.0)
    o_ref[...] = acc.astype(o_ref.dtype)


def _in_proj(a, s, w_t, *, layer, row0, shift, n, bm, bn, out_dtype, act=None, valid_cols=None, name):
    m, k = a.shape
    bm = min(bm, m)
    in_specs = [pl.BlockSpec((bm, k), lambda j, i: (i, 0)),
                pl.BlockSpec((bm, LANES), lambda j, i: (i, 0)),
                pl.BlockSpec((None, bn, k), lambda j, i: (layer, row0 // bn + j, 0))]
    args = [a, s, w_t]
    if shift:
        in_specs.append(pl.BlockSpec(
            (None, shift, k), lambda j, i: (layer, (row0 + (j + 1) * bn) // shift, 0)))
        args.append(w_t)
    return pl.pallas_call(
        functools.partial(_in_proj_kernel, shift=shift, act=act, valid_cols=valid_cols),
        grid=(n // bn, m // bm),
        in_specs=in_specs,
        out_specs=pl.BlockSpec((bm, bn), lambda j, i: (i, j)),
        out_shape=jax.ShapeDtypeStruct((m, n), out_dtype),
        scratch_shapes=[pltpu.VMEM((bn, k), BF16)],
        compiler_params=_params(2),
        name=name,
    )(*args)


def _side_cast_specs(w, layer, n_steps, step_of):
    rows, cols = w.shape[1], w.shape[2]
    chunk = rows // n_steps
    assert chunk * n_steps == rows and chunk % BF16_SUBLANES == 0
    return (pl.BlockSpec((None, chunk, cols), lambda *g: (layer, step_of(*g), 0)),
            pl.BlockSpec((chunk, cols), lambda *g: (step_of(*g), 0)),
            jax.ShapeDtypeStruct((rows, cols), BF16))


def _WARMUP_OUT_INDEX(j, i):
    return jnp.where(j == 0, 0, i), jnp.maximum(j - 1, 0)


def _in_proj_stream_kernel(*refs, shift, act, side):
    a_ref, s_ref, wc_ref = refs[0], refs[1], refs[2]
    w_scr = refs[-1]
    o_ref = refs[-3] if side else refs[-2]
    j, i = pl.program_id(0), pl.program_id(1)
    chunk = wc_ref.shape[0]
    slot = j % 2

    def convert_chunk():
        src = wc_ref[...]
        if shift:
            src = jnp.concatenate([src[shift:, :], refs[3][...]], axis=0)
        w_scr[slot, pl.ds(pl.multiple_of(i * chunk, chunk), chunk), :] = src.astype(w_scr.dtype)

    @pl.when(j == 0)
    def _():
        convert_chunk()
        o_ref[...] = jnp.zeros_like(o_ref)

    @pl.when(j > 0)
    def _():
        convert_chunk()
        rows_per = a_ref.shape[0] // ROW_CHUNKS
        w = w_scr[1 - slot]
        for c in range(ROW_CHUNKS):
            rows = slice(c * rows_per, (c + 1) * rows_per)
            acc = lax.dot_general(a_ref[rows, :], w, (((1,), (1,)), ((), ())),
                                  preferred_element_type=F32)
            acc = _row_scale(acc, s_ref[rows, :])
            if act == "sigmoid":
                acc = _sigmoid(acc)
            o_ref[rows, :] = acc.astype(o_ref.dtype)
        if side:
            refs[-2][...] = refs[3 + bool(shift)][...].astype(BF16)


def _in_proj_stream(a, s, w_t, *, layer, row0, shift, n, bm, bn, out_dtype, act=None, side_w=None, name):
    m, k = a.shape
    bm = min(bm, m)
    nm = m // bm
    nb = n // bn
    chunk = bn // nm
    assert chunk * nm == bn and chunk % BF16_SUBLANES == 0 and row0 % chunk == 0
    wblk = lambda j: jnp.minimum(j, nb - 1)
    in_specs = [pl.BlockSpec((bm, k), lambda j, i: (i, 0)),
                pl.BlockSpec((bm, LANES), lambda j, i: (i, 0)),
                pl.BlockSpec((None, chunk, k), lambda j, i: (layer, row0 // chunk + wblk(j) * nm + i, 0))]
    args = [a, s, w_t]
    if shift:
        in_specs.append(pl.BlockSpec(
            (None, shift, k),
            lambda j, i: (layer, (row0 + wblk(j) * bn + (i + 1) * chunk) // shift, 0)))
        args.append(w_t)
    out_specs = [pl.BlockSpec((bm, bn), _WARMUP_OUT_INDEX)]
    out_shape = [jax.ShapeDtypeStruct((m, n), out_dtype)]
    if side_w is not None:
        side_in, side_out, side_shape = _side_cast_specs(
            side_w, layer, nb * nm, lambda j, i: jnp.where(j == 0, 0, (j - 1) * nm + i))
        in_specs.append(side_in)
        args.append(side_w)
        out_specs.append(side_out)
        out_shape.append(side_shape)
    out = pl.pallas_call(
        functools.partial(_in_proj_stream_kernel, shift=shift, act=act, side=side_w is not None),
        grid=(nb + 1, nm),
        in_specs=in_specs,
        out_specs=out_specs,
        out_shape=out_shape,
        scratch_shapes=[pltpu.VMEM((2, bn, k), BF16)],
        compiler_params=_params(2),
        name=name,
    )(*args)
    return out if side_w is not None else out[0]


def _conv_a_kernel(b_ref, c_ref, x_ref, ch_ref, xh_ref, w_ref, o_ref, p_scr, *, blocks_per_seq):
    i = pl.program_id(0)
    tm = b_ref.shape[0]
    h = BF16_SUBLANES
    p = c_ref[...].astype(F32) * x_ref[...].astype(F32)
    ph = ch_ref[...].astype(F32) * xh_ref[...].astype(F32)
    ph = jnp.where(i % blocks_per_seq == 0, 0.0, ph)
    p_scr[0:h, :] = ph
    p_scr[h:h + tm, :] = p
    w = w_ref[...]
    y = w[2:3, :] * p + w[1:2, :] * p_scr[h - 1:h - 1 + tm, :] + w[0:1, :] * p_scr[h - 2:h - 2 + tm, :]
    o_ref[...] = (b_ref[...].astype(F32) * y).astype(o_ref.dtype)


def _mixer_a(proj, conv_w, seq, tm=1024):
    m = proj.shape[0]
    tm = min(tm, seq)
    h = BF16_SUBLANES
    halo = lambda col: pl.BlockSpec(
        (h, W_MIX), lambda i: (jnp.maximum(i * (tm // h) - 1, 0), col))
    main = lambda col: pl.BlockSpec((tm, W_MIX), lambda i: (i, col))
    return pl.pallas_call(
        functools.partial(_conv_a_kernel, blocks_per_seq=seq // tm),
        grid=(m // tm,),
        in_specs=[main(COL_A_B), main(COL_A_C), main(COL_A_X), halo(COL_A_C), halo(COL_A_X),
                  pl.BlockSpec((CONV_A, W_MIX), lambda i: (0, 0))],
        out_specs=pl.BlockSpec((tm, W_MIX), lambda i: (i, 0)),
        out_shape=jax.ShapeDtypeStruct((m, W_MIX), BF16),
        scratch_shapes=[pltpu.VMEM((tm + h, W_MIX), F32)],
        compiler_params=_params(1),
        name="mixer_a_conv",
    )(proj, proj, proj, proj, proj, conv_w)


def _sgu_kernel(u_ref, v_ref, ng_ref, w_ref, bexp_ref, o_ref):
    tm = u_ref.shape[0]
    vf = _gelu_tanh(v_ref[...].astype(F32))
    mu = jnp.mean(vf, axis=-1, keepdims=True)
    d = vf - mu
    var = jnp.mean(d * d, axis=-1, keepdims=True)
    vn = (d * lax.rsqrt(var + EPS) * ng_ref[...]).astype(BF16)
    t_idx = lax.broadcasted_iota(jnp.int32, (CHUNK, CHUNK), 0)
    s_idx = lax.broadcasted_iota(jnp.int32, (CHUNK, CHUNK), 1)
    causal = s_idx <= t_idx
    ws = [jnp.where(causal, w_ref[g], 0.0).astype(BF16) for g in range(SGU_GROUPS)]
    bexp = bexp_ref[...]
    for c in range(tm // CHUNK):
        rows = slice(c * CHUNK, (c + 1) * CHUNK)
        vn_c = vn[rows, :]
        mixed = jnp.concatenate(
            [jnp.dot(ws[g], vn_c[:, g * GROUP:(g + 1) * GROUP], preferred_element_type=F32)
             for g in range(SGU_GROUPS)], axis=1)
        u = _gelu_tanh(u_ref[rows, :].astype(F32))
        o_ref[rows, :] = (u * (mixed + bexp)).astype(o_ref.dtype)


def _mixer_b(proj, norm_g, w_s, b_s, tm=1024):
    m = proj.shape[0]
    tm = min(tm, m)
    bexp = jnp.repeat(b_s.T, GROUP, axis=1)
    return pl.pallas_call(
        _sgu_kernel,
        grid=(m // tm,),
        in_specs=[pl.BlockSpec((tm, W_MIX), lambda i: (i, COL_B_U)),
                  pl.BlockSpec((tm, W_MIX), lambda i: (i, COL_B_V)),
                  pl.BlockSpec((1, W_MIX), lambda i: (0, 0)),
                  pl.BlockSpec((SGU_GROUPS, CHUNK, CHUNK), lambda i: (0, 0, 0)),
                  pl.BlockSpec((CHUNK, W_MIX), lambda i: (0, 0))],
        out_specs=pl.BlockSpec((tm, W_MIX), lambda i: (i, 0)),
        out_shape=jax.ShapeDtypeStruct((m, W_MIX), BF16),
        compiler_params=_params(1),
        name="mixer_b_sgu",
    )(proj, proj, norm_g.reshape(1, W_MIX), w_s, bexp)


def _qknorm_kernel(q_ref, k_ref, qg_ref, kg_ref, qo_ref, ko_ref):
    for src, g_ref, dst in ((q_ref, qg_ref, qo_ref), (k_ref, kg_ref, ko_ref)):
        g = g_ref[...]
        for h in range(SB_HEADS):
            cols = slice(h * SB_HEAD_DIM, (h + 1) * SB_HEAD_DIM)
            x = src[:, cols].astype(F32)
            ms = jnp.mean(x * x, axis=-1, keepdims=True)
            dst[:, cols] = (x * lax.rsqrt(ms + EPS) * g).astype(dst.dtype)


def _qknorm(proj, q_g, k_g, tm=1024):
    m = proj.shape[0]
    tm = min(tm, m)
    spec = lambda col: pl.BlockSpec((tm, W_MIX), lambda i: (i, col))
    gspec = pl.BlockSpec((1, SB_HEAD_DIM), lambda i: (0, 0))
    return pl.pallas_call(
        _qknorm_kernel,
        grid=(m // tm,),
        in_specs=[spec(COL_C_Q), spec(COL_C_K), gspec, gspec],
        out_specs=[spec(0), spec(0)],
        out_shape=[jax.ShapeDtypeStruct((m, W_MIX), BF16)] * 2,
        compiler_params=_params(1),
        name="qk_norm",
    )(proj, proj, q_g.reshape(1, -1), k_g.reshape(1, -1))


EXP2_ZERO_BELOW = -160.0
R_DEAD = -1e30


def _attn_kernel(q_ref, k_ref, v_ref, o_ref, acc_scr, *, tq, sub):
    i = pl.program_id(2)
    n_groups = tq // sub
    qs = [q_ref[g * sub:(g + 1) * sub, :] for g in range(n_groups)]
    jj = lax.broadcasted_iota(jnp.int32, (sub, sub), 0)
    ss = lax.broadcasted_iota(jnp.int32, (sub, sub), 1)
    uneg = jnp.where(jj > ss, -1.0, 0.0).astype(BF16)

    def tile(q, off, r, mask):
        rows = pl.ds(off, sub)
        z2 = lax.dot_general(q, k_ref[rows, :], (((1,), (1,)), ((), ())),
                             preferred_element_type=F32)
        neg_abs = pltpu.bitcast(pltpu.bitcast(z2, jnp.int32) | jnp.int32(-2 ** 31), F32)
        l2 = jnp.log(1.0 + jnp.exp2(neg_abs)) * (1.0 / math.log(2.0))
        sp = jnp.maximum(z2, 0.0) + l2
        lb = jnp.minimum(z2, 0.0) - l2
        if mask is not None:
            sp = jnp.where(mask, sp, 0.0)
        rev = jnp.dot(sp.astype(BF16), uneg, preferred_element_type=F32)
        a = jnp.exp2(lb + rev + r)
        if mask is not None:
            a = jnp.where(mask, a, 0.0)
        pv = jnp.dot(a.astype(BF16), v_ref[rows, :], preferred_element_type=F32)
        return pv, r - jnp.sum(sp, axis=1, keepdims=True)

    base = i * n_groups
    rs = []
    for g in range(n_groups):
        pv, r = tile(qs[g], pl.multiple_of((base + g) * sub, sub), jnp.zeros((sub, 1), F32), ss < jj)
        acc_scr[g * sub:(g + 1) * sub, :] = pv
        rs.append(r)

    n_steps = base + n_groups - 1

    def cond(carry):
        step, _, r_max = carry
        return jnp.logical_and(step < n_steps, r_max > EXP2_ZERO_BELOW)

    def body(carry):
        step, rs, _ = carry
        new_rs = []
        for g in range(n_groups):
            t_idx = base + g - 1 - step
            r_in = jnp.where(t_idx >= 0, rs[g], R_DEAD)
            pv, r_out = tile(qs[g], pl.multiple_of(jnp.maximum(t_idx, 0) * sub, sub), r_in, None)
            acc_scr[g * sub:(g + 1) * sub, :] += pv
            new_rs.append(r_out)
        r_max = functools.reduce(jnp.maximum, [jnp.max(r) for r in new_rs])
        return step + 1, tuple(new_rs), r_max

    r_max = functools.reduce(jnp.maximum, [jnp.max(r) for r in rs])
    lax.while_loop(cond, body, (jnp.int32(0), tuple(rs), r_max))
    o_ref[...] = acc_scr[...].astype(o_ref.dtype)


def _attention(qn, kn, proj, batch, seq, tq=2048, sub=256):
    m = qn.shape[0]
    tq = min(tq, seq)
    nq = seq // tq
    v_col0 = COL_C_V * (W_MIX // SB_HEAD_DIM)
    return pl.pallas_call(
        functools.partial(_attn_kernel, tq=tq, sub=sub),
        grid=(batch, SB_HEADS, nq),
        in_specs=[pl.BlockSpec((tq, SB_HEAD_DIM), lambda b, h, i: (b * nq + i, h)),
                  pl.BlockSpec((seq, SB_HEAD_DIM), lambda b, h, i: (b, h)),
                  pl.BlockSpec((seq, SB_HEAD_DIM), lambda b, h, i: (b, v_col0 + h))],
        out_specs=pl.BlockSpec((tq, SB_HEAD_DIM), lambda b, h, i: (b * nq + i, h)),
        out_shape=jax.ShapeDtypeStruct((m, W_MIX), BF16),
        scratch_shapes=[pltpu.VMEM((tq, SB_HEAD_DIM), F32)],
        compiler_params=_params(3),
        name="stick_breaking_attn",
    )(qn, kn, proj)


def _split3(x):
    h1 = x.astype(BF16)
    r1 = x - h1.astype(F32)
    h2 = r1.astype(BF16)
    h3 = (r1 - h2.astype(F32)).astype(BF16)
    return [h1, h2, h3]


def _ssd_kernel(z_ref, xbc_ref, halo_ref, dt_ref, cw_ref, cb_ref, dtb_ref, alog_ref, dexp_ref,
                ng_ref, xmat_ref, o_ref, ht_scr, xp_scr):
    c = pl.program_id(1)
    L = CHUNK
    h = BF16_SUBLANES
    gw = W_MIX // SSM_GROUPS

    @pl.when(c == 0)
    def _():
        ht_scr[...] = jnp.zeros_like(ht_scr)

    xp_scr[0:h, :] = jnp.where(c == 0, 0.0, halo_ref[...].astype(F32))
    xp_scr[h:h + L, :] = xbc_ref[...].astype(F32)
    cw = cw_ref[...]
    conv = cb_ref[...]
    for k in range(SSM_CONV):
        s0 = h - (SSM_CONV - 1) + k
        conv = conv + cw[k:k + 1, :] * xp_scr[s0:s0 + L, :]
    xc = conv * _sigmoid(conv)
    xs = xc[:, :W_MIX]
    bmat = xc[:, W_MIX:W_MIX + SSM_GROUPS * SSM_STATE].astype(BF16)
    cmat = xc[:, W_MIX + SSM_GROUPS * SSM_STATE:].astype(BF16)

    dt = _softplus(dt_ref[...] + dtb_ref[...])
    da = dt * (-jnp.exp(alog_ref[...]))
    t_i = lax.broadcasted_iota(jnp.int32, (L, L), 0)
    s_i = lax.broadcasted_iota(jnp.int32, (L, L), 1)
    causal = s_i <= t_i
    tri3 = jnp.concatenate([causal.astype(BF16)] * 3, axis=1)
    cum = jnp.dot(tri3, jnp.concatenate(_split3(da), axis=0), preferred_element_type=F32)
    cum_t = cum.T
    xmat3 = xmat_ref[...]
    dt_e = jnp.dot(jnp.concatenate(_split3(dt), axis=1), xmat3, preferred_element_type=F32)
    cum_e = jnp.dot(jnp.concatenate(_split3(cum), axis=1), xmat3, preferred_element_type=F32)
    cl_e = cum_e[L - 1:L, :]
    xdt = xs * dt_e
    xdec = (xdt * jnp.exp(cl_e - cum_e)).astype(BF16)
    ecum_e = jnp.exp(cum_e)
    chunk_dec = jnp.exp(cl_e)
    lane = lax.broadcasted_iota(jnp.int32, (L, LANES), 1)
    left = lane < SSM_HEAD_DIM

    ys = []
    for g in range(SSM_GROUPS):
        bg = bmat[:, g * SSM_STATE:(g + 1) * SSM_STATE]
        cg = cmat[:, g * SSM_STATE:(g + 1) * SSM_STATE]
        gcols = slice(g * gw, (g + 1) * gw)
        cb = lax.dot_general(cg, bg, (((1,), (1,)), ((), ())), preferred_element_type=F32)
        ht = ht_scr[g]
        y_off = jnp.dot(cg, ht.astype(BF16), preferred_element_type=F32)
        parts = []
        for pr in range(SSM_HPG // 2):
            ms = []
            for r in range(2):
                idx = g * SSM_HPG + 2 * pr + r
                seg = cum[:, idx:idx + 1] - cum_t[idx:idx + 1, :]
                dec = jnp.exp(jnp.where(causal, seg, -jnp.inf))
                ms.append((cb * dec).astype(BF16))
            c0 = g * gw + pr * LANES
            xpair = xdt[:, c0:c0 + LANES]
            rhs = jnp.concatenate(
                [jnp.where(left, xpair, 0.0), jnp.where(left, 0.0, xpair)], axis=0).astype(BF16)
            parts.append(jnp.dot(jnp.concatenate(ms, axis=1), rhs, preferred_element_type=F32))
        y_diag = jnp.concatenate(parts, axis=1)
        ys.append(y_diag + y_off * ecum_e[:, gcols])
        st = lax.dot_general(bg, xdec[:, gcols], (((0,), (0,)), ((), ())),
                             preferred_element_type=F32)
        ht_scr[g] = ht * chunk_dec[:, gcols] + st

    y = jnp.concatenate(ys, axis=1) + xs * dexp_ref[...]
    z = z_ref[...].astype(F32)
    y = y * (z * _sigmoid(z))
    outs = []
    for g in range(SSM_GROUPS):
        yg = y[:, g * gw:(g + 1) * gw]
        ms = jnp.mean(yg * yg, axis=-1, keepdims=True)
        outs.append(yg * lax.rsqrt(ms + EPS))
    o_ref[...] = (jnp.concatenate(outs, axis=1) * ng_ref[...]).astype(o_ref.dtype)


def _mixer_d(proj, dt_raw, conv_w, conv_b, dt_bias, a_log, d_skip, norm_g, batch, seq):
    m = proj.shape[0]
    nc = seq // CHUNK
    h = BF16_SUBLANES
    pad = LANES - SSM_HEADS
    row = lambda v: v.reshape(1, -1)
    xmat = (jnp.arange(W_MIX)[None, :] // SSM_HEAD_DIM == jnp.arange(LANES)[:, None]).astype(BF16)
    xmat = jnp.concatenate([xmat] * 3, axis=0)
    const = lambda shape: pl.BlockSpec(shape, lambda b, c: (0,) * len(shape))
    return pl.pallas_call(
        _ssd_kernel,
        grid=(batch, nc),
        in_specs=[pl.BlockSpec((CHUNK, W_MIX), lambda b, c: (b * nc + c, COL_D_Z)),
                  pl.BlockSpec((CHUNK, SSM_CONV_DIM), lambda b, c: (b * nc + c, COL_D_XBC)),
                  pl.BlockSpec((h, SSM_CONV_DIM),
                               lambda b, c: (jnp.maximum((b * nc + c) * (CHUNK // h) - 1, 0), COL_D_XBC)),
                  pl.BlockSpec((CHUNK, LANES), lambda b, c: (b * nc + c, 0)),
                  const((SSM_CONV, SSM_CONV_DIM)), const((1, SSM_CONV_DIM)),
                  const((1, LANES)), const((1, LANES)), const((1, W_MIX)), const((1, W_MIX)),
                  const((3 * LANES, W_MIX))],
        out_specs=pl.BlockSpec((CHUNK, W_MIX), lambda b, c: (b * nc + c, 0)),
        out_shape=jax.ShapeDtypeStruct((m, W_MIX), BF16),
        scratch_shapes=[pltpu.VMEM((SSM_GROUPS, SSM_STATE, W_MIX // SSM_GROUPS), F32),
                        pltpu.VMEM((CHUNK + h, SSM_CONV_DIM), F32)],
        compiler_params=_params(2),
        name="mixer_d_ssd",
    )(proj, proj, proj, dt_raw, conv_w, row(conv_b),
      row(jnp.pad(dt_bias, (0, pad))), row(jnp.pad(a_log, (0, pad))),
      row(jnp.repeat(d_skip, SSM_HEAD_DIM)), row(norm_g), xmat)


def _merge_kernel(ya_ref, yb_ref, yc_ref, yd_ref, wb_ref, g0_ref, g1_ref, g2_ref, g3_ref, wsrc_ref,
                  o_ref, wdst_ref):
    wdst_ref[...] = wsrc_ref[...].astype(wdst_ref.dtype)
    acc = None
    for j, (y_ref, g_ref) in enumerate(
            zip((ya_ref, yb_ref, yc_ref, yd_ref), (g0_ref, g1_ref, g2_ref, g3_ref))):
        t = jnp.dot(y_ref[...], wb_ref[j], preferred_element_type=F32) * g_ref[...].astype(F32)
        acc = t if acc is None else acc + t
    o_ref[...] = acc.astype(o_ref.dtype)


def _merge(ys, w_branch, gates, side_w, *, layer, bm=512, bn=1024):
    m = gates.shape[0]
    bm = min(bm, m)
    nb = D_MODEL // bn
    nm = m // bm
    side_in, side_out, side_shape = _side_cast_specs(side_w, layer, nb * nm, lambda n, i: n * nm + i)
    yspec = pl.BlockSpec((bm, W_MIX), lambda n, i: (i, 0))
    gspec = lambda j: pl.BlockSpec((bm, bn), lambda n, i: (i, j * nb + n))
    return pl.pallas_call(
        _merge_kernel,
        grid=(nb, m // bm),
        in_specs=[yspec] * N_BRANCH
        + [pl.BlockSpec((N_BRANCH, W_MIX, bn), lambda n, i: (0, 0, n))]
        + [gspec(j) for j in range(N_BRANCH)] + [side_in],
        out_specs=[pl.BlockSpec((bm, bn), lambda n, i: (i, n)), side_out],
        out_shape=[jax.ShapeDtypeStruct((m, D_MODEL), BF16), side_shape],
        compiler_params=_params(2),
        name="branch_merge",
    )(*ys, w_branch, gates, gates, gates, gates, side_w)


def _ffn_gu_kernel(h_ref, s_ref, wg_ref, wu_ref, wsrc_ref, o_ref, wdst_ref, wg_scr, wu_scr):
    j, i = pl.program_id(0), pl.program_id(1)
    chunk = wg_ref.shape[0]
    slot = j % 2

    def convert_chunk():
        rows = pl.ds(pl.multiple_of(i * chunk, chunk), chunk)
        wg_scr[slot, rows, :] = wg_ref[...].astype(wg_scr.dtype)
        wu_scr[slot, rows, :] = wu_ref[...].astype(wu_scr.dtype)

    @pl.when(j == 0)
    def _():
        convert_chunk()
        o_ref[...] = jnp.zeros_like(o_ref)

    @pl.when(j > 0)
    def _():
        convert_chunk()
        wg, wu = wg_scr[1 - slot], wu_scr[1 - slot]
        rows_per = h_ref.shape[0] // ROW_CHUNKS
        for c in range(ROW_CHUNKS):
            rows = slice(c * rows_per, (c + 1) * rows_per)
            h = h_ref[rows, :]
            sc = s_ref[rows, :]
            g = _row_scale(jnp.dot(h, wg, preferred_element_type=F32), sc)
            u = _row_scale(jnp.dot(h, wu, preferred_element_type=F32), sc)
            o_ref[rows, :] = (g * _sigmoid(g) * u).astype(o_ref.dtype)
        wdst_ref[...] = wsrc_ref[...].astype(wdst_ref.dtype)


def _ffn_gate_up(h, s, wg, wu, side_w, *, layer, bm=2048, bn=256):
    m, k = h.shape
    f = wg.shape[2]
    bm = min(bm, m)
    nm = m // bm
    nb = f // bn
    chunk = k // nm
    assert chunk * nm == k and chunk % BF16_SUBLANES == 0
    wspec = pl.BlockSpec((None, chunk, bn), lambda j, i: (layer, i, jnp.minimum(j, nb - 1)))
    side_in, side_out, side_shape = _side_cast_specs(
        side_w, layer, nb * nm, lambda j, i: jnp.where(j == 0, 0, (j - 1) * nm + i))
    return pl.pallas_call(
        _ffn_gu_kernel,
        grid=(nb + 1, nm),
        in_specs=[pl.BlockSpec((bm, k), lambda j, i: (i, 0)),
                  pl.BlockSpec((bm, LANES), lambda j, i: (i, 0)), wspec, wspec, side_in],
        out_specs=[pl.BlockSpec((bm, bn), _WARMUP_OUT_INDEX), side_out],
        out_shape=[jax.ShapeDtypeStruct((m, f), BF16), side_shape],
        scratch_shapes=[pltpu.VMEM((2, k, bn), BF16)] * 2,
        compiler_params=_params(2),
        name="ffn_gate_up",
    )(h, s, wg, wu, side_w)


def kernel(x, norm_mix, w_in, conv_a, sgu_norm, sgu_w, sgu_b, q_norm, k_norm, ssm_conv_w, ssm_conv_b, ssm_dt_bias, ssm_a_log, ssm_d, ssm_norm, w_branch, w_out, norm_ffn, w_ffn_gate, w_ffn_up, w_ffn_down):
    batch, seq, d = x.shape
    m = batch * seq
    depth = w_in.shape[0]
    xr = x.reshape(m, d)
    q_fold = (1.0 / math.sqrt(SB_HEAD_DIM)) * (1.0 / math.log(2.0))
    w_in_t = jnp.swapaxes(w_in, 1, 2)
    w_branch_2d = w_branch.reshape(depth, N_BRANCH * W_MIX, d)
    xb, s = _norm_prep(xr, norm_mix[0])
    for l in range(depth):
        proj = _in_proj_stream(xb, s, w_in_t, layer=l, row0=0, shift=0, n=N_MIX, bm=1024, bn=768,
                               out_dtype=BF16, name="in_proj_mix")
        dt_raw = _in_proj(xb, s, w_in_t, layer=l, row0=N_MIX, shift=0, n=LANES, bm=1024, bn=LANES,
                          out_dtype=F32, valid_cols=SSM_HEADS, name="in_proj_dt")
        gates, w_branch_bf = _in_proj_stream(
            xb, s, w_in_t, layer=l, row0=N_MIX, shift=GATE_SHIFT, n=N_BRANCH * d, bm=1024, bn=1024,
            out_dtype=BF16, act="sigmoid", side_w=w_branch_2d, name="in_proj_gates")

        y_a = _mixer_a(proj, conv_a[l], seq)
        y_b = _mixer_b(proj, sgu_norm[l], sgu_w[l], sgu_b[l])
        qn, kn = _qknorm(proj, q_norm[l] * q_fold, k_norm[l])
        y_c = _attention(qn, kn, proj, batch, seq)
        y_d = _mixer_d(proj, dt_raw, ssm_conv_w[l], ssm_conv_b[l], ssm_dt_bias[l], ssm_a_log[l],
                       ssm_d[l], ssm_norm[l], batch, seq)

        merged, w_out_bf = _merge((y_a, y_b, y_c, y_d), w_branch_bf.reshape(N_BRANCH, W_MIX, d), gates,
                                  w_out, layer=l)
        xr, xb, s = _resid_matmul(merged, w_out_bf, xr, norm_ffn[l], bm=1024, bn=512, name="out_proj")

        act, w_down_bf = _ffn_gate_up(xb, s, w_ffn_gate, w_ffn_up, w_ffn_down, layer=l)
        if l + 1 < depth:
            xr, xb, s = _resid_matmul(act, w_down_bf, xr, norm_mix[l + 1], bm=512, bn=512,
                                      name="ffn_down")
        else:
            xr = _resid_matmul(act, w_down_bf, xr, None, bm=512, bn=512, name="ffn_down")
    return xr.reshape(batch, seq, d)
```

```python
import functools
import math

import jax
import jax.numpy as jnp
from jax import lax
from jax.experimental import pallas as pl
from jax.experimental.pallas import tpu as pltpu

F32 = jnp.float32
BF16 = jnp.bfloat16

EPS = 1e-6
D_MODEL = 4096
W_MIX = D_MODEL // 4
GROUP = 128
CHUNK = 128
CONV_A = 3
SGU_GROUPS = W_MIX // GROUP
SB_HEAD_DIM = 128
SB_HEADS = W_MIX // SB_HEAD_DIM
SSM_HEAD_DIM = 64
SSM_HEADS = W_MIX // SSM_HEAD_DIM
SSM_GROUPS = 2
SSM_HPG = SSM_HEADS // SSM_GROUPS
SSM_STATE = 128
SSM_CONV = 4
SSM_CONV_DIM = W_MIX + 2 * SSM_GROUPS * SSM_STATE
N_BRANCH = 4
N_MIX = 9 * W_MIX + SSM_CONV_DIM
GATE_SHIFT = SSM_HEADS
LANES = 128
BF16_SUBLANES = 16
VMEM_LIMIT = 52 * 1024 * 1024

COL_A_B, COL_A_C, COL_A_X = 0, 1, 2
COL_B_U, COL_B_V = 3, 4
COL_C_Q, COL_C_K, COL_C_V = 5, 6, 7
COL_D_Z = 8
COL_D_XBC = (9 * W_MIX) // SSM_CONV_DIM


def _params(n_axes):
    return pltpu.CompilerParams(
        dimension_semantics=("arbitrary",) * n_axes, vmem_limit_bytes=VMEM_LIMIT)


def _sigmoid(x):
    return 1.0 / (1.0 + jnp.exp(-x))


def _softplus(x):
    return jnp.maximum(x, 0.0) + jnp.log1p(jnp.exp(-jnp.abs(x)))


def _gelu_tanh(x):
    c = math.sqrt(2.0 / math.pi)
    return x * (0.5 * (1.0 + jnp.tanh(c * (x + 0.044715 * (x * x * x)))))


def _norm_prep_kernel(x_ref, g_ref, wdt_ref, xb_ref, s_ref, dt_ref):
    x = x_ref[...]
    ms = jnp.mean(x * x, axis=-1, keepdims=True)
    xb = (x * g_ref[...]).astype(xb_ref.dtype)
    scale = jnp.broadcast_to(lax.rsqrt(ms + EPS), s_ref.shape)
    xb_ref[...] = xb
    s_ref[...] = scale
    dt_ref[...] = jnp.dot(xb, wdt_ref[...], preferred_element_type=F32) * scale


def _norm_prep(x, g, w_dt, tm=256):
    m, d = x.shape
    return pl.pallas_call(
        _norm_prep_kernel,
        grid=(m // tm,),
        in_specs=[pl.BlockSpec((tm, d), lambda i: (i, 0)),
                  pl.BlockSpec((1, d), lambda i: (0, 0)),
                  pl.BlockSpec((d, LANES), lambda i: (0, 0))],
        out_specs=[pl.BlockSpec((tm, d), lambda i: (i, 0)),
                   pl.BlockSpec((tm, LANES), lambda i: (i, 0)),
                   pl.BlockSpec((tm, LANES), lambda i: (i, 0))],
        out_shape=[jax.ShapeDtypeStruct((m, d), BF16), jax.ShapeDtypeStruct((m, LANES), F32),
                   jax.ShapeDtypeStruct((m, LANES), F32)],
        compiler_params=_params(1),
        name="norm_prep",
    )(x, g.reshape(1, d), w_dt)


def _dt_proj_kernel(xb_ref, s_ref, wdt_ref, o_ref):
    o_ref[...] = jnp.dot(xb_ref[...], wdt_ref[...], preferred_element_type=F32) * s_ref[...]


def _dt_proj(xb, s, w_dt, tm=1024):
    m, d = xb.shape
    tm = min(tm, m)
    return pl.pallas_call(
        _dt_proj_kernel,
        grid=(m // tm,),
        in_specs=[pl.BlockSpec((tm, d), lambda i: (i, 0)),
                  pl.BlockSpec((tm, LANES), lambda i: (i, 0)),
                  pl.BlockSpec((d, LANES), lambda i: (0, 0))],
        out_specs=pl.BlockSpec((tm, LANES), lambda i: (i, 0)),
        out_shape=jax.ShapeDtypeStruct((m, LANES), F32),
        compiler_params=_params(1),
        name="dt_proj",
    )(xb, s, w_dt)


ROW_CHUNKS = 4


def _row_scale(acc, s):
    reps = acc.shape[1] // LANES
    return acc * (s if reps == 1 else jnp.concatenate([s] * reps, axis=1))


def _resid_mm_kernel(*refs, norm, n_steps, n_cols):
    a_ref, w_ref, r_ref = refs[0], refs[1], refs[2]
    acc = jnp.dot(a_ref[...], w_ref[...], preferred_element_type=F32) + r_ref[...]
    if not norm:
        refs[3][...] = acc
        return
    g_ref, o_ref, xb_ref, s_ref = refs[3], refs[4], refs[5], refs[6]
    j = pl.program_id(1)
    o_ref[...] = acc
    xb_ref[...] = (acc * g_ref[...]).astype(xb_ref.dtype)
    sq = acc * acc
    part = sq[:, 0:LANES]
    for c in range(1, acc.shape[1] // LANES):
        part = part + sq[:, c * LANES:(c + 1) * LANES]

    @pl.when(j == 0)
    def _():
        s_ref[...] = part

    @pl.when(j != 0)
    def _():
        s_ref[...] += part

    @pl.when(j == n_steps - 1)
    def _():
        tot = jnp.sum(s_ref[...], axis=1, keepdims=True)
        s_ref[...] = jnp.broadcast_to(lax.rsqrt(tot * (1.0 / n_cols) + EPS), s_ref.shape)


def _resid_matmul(a, w, res, g_next, *, bm, bn, name):
    m, k = a.shape
    n = w.shape[1]
    bm = min(bm, m)
    norm = g_next is not None
    ij = lambda i, j: (i, j)
    in_specs = [pl.BlockSpec((bm, k), lambda i, j: (i, 0)),
                pl.BlockSpec((k, bn), lambda i, j: (0, j)),
                pl.BlockSpec((bm, bn), ij)]
    args = [a, w, res]
    out_specs = [pl.BlockSpec((bm, bn), ij)]
    out_shape = [jax.ShapeDtypeStruct((m, n), F32)]
    if norm:
        in_specs.append(pl.BlockSpec((1, bn), lambda i, j: (0, j)))
        args.append(g_next.reshape(1, n))
        out_specs += [pl.BlockSpec((bm, bn), ij), pl.BlockSpec((bm, LANES), lambda i, j: (i, 0))]
        out_shape += [jax.ShapeDtypeStruct((m, n), BF16), jax.ShapeDtypeStruct((m, LANES), F32)]
    out = pl.pallas_call(
        functools.partial(_resid_mm_kernel, norm=norm, n_steps=n // bn, n_cols=n),
        grid=(m // bm, n // bn),
        in_specs=in_specs,
        out_specs=out_specs,
        out_shape=out_shape,
        compiler_params=_params(2),
        name=name,
    )(*args)
    return out if norm else out[0]


def _side_cast_specs(w, layer, n_steps, step_of):
    rows, cols = w.shape[1], w.shape[2]
    chunk = rows // n_steps
    assert chunk * n_steps == rows and chunk % BF16_SUBLANES == 0
    return (pl.BlockSpec((None, chunk, cols), lambda *g: (layer, step_of(*g), 0)),
            pl.BlockSpec((chunk, cols), lambda *g: (step_of(*g), 0)),
            jax.ShapeDtypeStruct((rows, cols), BF16))


def _WARMUP_OUT_INDEX(j, i):
    return jnp.where(j == 0, 0, i), jnp.maximum(j - 1, 0)


def _in_proj_stream_kernel(*refs, shift, act, side):
    a_ref, s_ref, wc_ref = refs[0], refs[1], refs[2]
    w_scr = refs[-1]
    o_ref = refs[-3] if side else refs[-2]
    j, i = pl.program_id(0), pl.program_id(1)
    chunk = wc_ref.shape[0]
    slot = j % 2

    def convert_chunk():
        src = wc_ref[...]
        if shift:
            src = jnp.concatenate([src[shift:, :], refs[3][...]], axis=0)
        w_scr[slot, pl.ds(pl.multiple_of(i * chunk, chunk), chunk), :] = src.astype(w_scr.dtype)

    @pl.when(j == 0)
    def _():
        convert_chunk()
        o_ref[...] = jnp.zeros_like(o_ref)

    @pl.when(j > 0)
    def _():
        convert_chunk()
        rows_per = a_ref.shape[0] // ROW_CHUNKS
        w = w_scr[1 - slot]
        for c in range(ROW_CHUNKS):
            rows = slice(c * rows_per, (c + 1) * rows_per)
            acc = lax.dot_general(a_ref[rows, :], w, (((1,), (1,)), ((), ())),
                                  preferred_element_type=F32)
            acc = _row_scale(acc, s_ref[rows, :])
            if act == "sigmoid":
                acc = _sigmoid(acc)
            o_ref[rows, :] = acc.astype(o_ref.dtype)
        if side:
            refs[-2][...] = refs[3 + bool(shift)][...].astype(BF16)


def _in_proj_stream(a, s, w_t, *, layer, row0, shift, n, bm, bn, out_dtype, act=None, side_w=None, name):
    m, k = a.shape
    bm = min(bm, m)
    nm = m // bm
    nb = n // bn
    chunk = bn // nm
    assert chunk * nm == bn and chunk % BF16_SUBLANES == 0 and row0 % chunk == 0
    wblk = lambda j: jnp.minimum(j, nb - 1)
    in_specs = [pl.BlockSpec((bm, k), lambda j, i: (i, 0)),
                pl.BlockSpec((bm, LANES), lambda j, i: (i, 0)),
                pl.BlockSpec((None, chunk, k), lambda j, i: (layer, row0 // chunk + wblk(j) * nm + i, 0))]
    args = [a, s, w_t]
    if shift:
        in_specs.append(pl.BlockSpec(
            (None, shift, k),
            lambda j, i: (layer, (row0 + wblk(j) * bn + (i + 1) * chunk) // shift, 0)))
        args.append(w_t)
    out_specs = [pl.BlockSpec((bm, bn), _WARMUP_OUT_INDEX)]
    out_shape = [jax.ShapeDtypeStruct((m, n), out_dtype)]
    if side_w is not None:
        side_in, side_out, side_shape = _side_cast_specs(
            side_w, layer, nb * nm, lambda j, i: jnp.where(j == 0, 0, (j - 1) * nm + i))
        in_specs.append(side_in)
        args.append(side_w)
        out_specs.append(side_out)
        out_shape.append(side_shape)
    out = pl.pallas_call(
        functools.partial(_in_proj_stream_kernel, shift=shift, act=act, side=side_w is not None),
        grid=(nb + 1, nm),
        in_specs=in_specs,
        out_specs=out_specs,
        out_shape=out_shape,
        scratch_shapes=[pltpu.VMEM((2, bn, k), BF16)],
        compiler_params=_params(2),
        name=name,
    )(*args)
    return out if side_w is not None else out[0]


def _conv_a_kernel(b_ref, c_ref, x_ref, ch_ref, xh_ref, w_ref, o_ref, p_scr, *, blocks_per_seq):
    i = pl.program_id(0)
    tm = b_ref.shape[0]
    h = BF16_SUBLANES
    p = c_ref[...].astype(F32) * x_ref[...].astype(F32)
    ph = ch_ref[...].astype(F32) * xh_ref[...].astype(F32)
    ph = jnp.where(i % blocks_per_seq == 0, 0.0, ph)
    p_scr[0:h, :] = ph
    p_scr[h:h + tm, :] = p
    w = w_ref[...]
    y = w[2:3, :] * p + w[1:2, :] * p_scr[h - 1:h - 1 + tm, :] + w[0:1, :] * p_scr[h - 2:h - 2 + tm, :]
    o_ref[...] = (b_ref[...].astype(F32) * y).astype(o_ref.dtype)


def _mixer_a(proj, conv_w, seq, tm=1024):
    m = proj.shape[0]
    tm = min(tm, seq)
    h = BF16_SUBLANES
    halo = lambda col: pl.BlockSpec(
        (h, W_MIX), lambda i: (jnp.maximum(i * (tm // h) - 1, 0), col))
    main = lambda col: pl.BlockSpec((tm, W_MIX), lambda i: (i, col))
    return pl.pallas_call(
        functools.partial(_conv_a_kernel, blocks_per_seq=seq // tm),
        grid=(m // tm,),
        in_specs=[main(COL_A_B), main(COL_A_C), main(COL_A_X), halo(COL_A_C), halo(COL_A_X),
                  pl.BlockSpec((CONV_A, W_MIX), lambda i: (0, 0))],
        out_specs=pl.BlockSpec((tm, W_MIX), lambda i: (i, 0)),
        out_shape=jax.ShapeDtypeStruct((m, W_MIX), BF16),
        scratch_shapes=[pltpu.VMEM((tm + h, W_MIX), F32)],
        compiler_params=_params(1),
        name="mixer_a_conv",
    )(proj, proj, proj, proj, proj, conv_w)


def _sgu_kernel(u_ref, v_ref, ng_ref, w_ref, bexp_ref, o_ref):
    tm = u_ref.shape[0]
    vf = _gelu_tanh(v_ref[...].astype(F32))
    mu = jnp.mean(vf, axis=-1, keepdims=True)
    d = vf - mu
    var = jnp.mean(d * d, axis=-1, keepdims=True)
    vn = (d * lax.rsqrt(var + EPS) * ng_ref[...]).astype(BF16)
    t_idx = lax.broadcasted_iota(jnp.int32, (CHUNK, CHUNK), 0)
    s_idx = lax.broadcasted_iota(jnp.int32, (CHUNK, CHUNK), 1)
    causal = s_idx <= t_idx
    ws = [jnp.where(causal, w_ref[g], 0.0).astype(BF16) for g in range(SGU_GROUPS)]
    bexp = bexp_ref[...]
    for c in range(tm // CHUNK):
        rows = slice(c * CHUNK, (c + 1) * CHUNK)
        vn_c = vn[rows, :]
        mixed = jnp.concatenate(
            [jnp.dot(ws[g], vn_c[:, g * GROUP:(g + 1) * GROUP], preferred_element_type=F32)
             for g in range(SGU_GROUPS)], axis=1)
        u = _gelu_tanh(u_ref[rows, :].astype(F32))
        o_ref[rows, :] = (u * (mixed + bexp)).astype(o_ref.dtype)


def _mixer_b(proj, norm_g, w_s, b_s, tm=1024):
    m = proj.shape[0]
    tm = min(tm, m)
    bexp = jnp.repeat(b_s.T, GROUP, axis=1)
    return pl.pallas_call(
        _sgu_kernel,
        grid=(m // tm,),
        in_specs=[pl.BlockSpec((tm, W_MIX), lambda i: (i, COL_B_U)),
                  pl.BlockSpec((tm, W_MIX), lambda i: (i, COL_B_V)),
                  pl.BlockSpec((1, W_MIX), lambda i: (0, 0)),
                  pl.BlockSpec((SGU_GROUPS, CHUNK, CHUNK), lambda i: (0, 0, 0)),
                  pl.BlockSpec((CHUNK, W_MIX), lambda i: (0, 0))],
        out_specs=pl.BlockSpec((tm, W_MIX), lambda i: (i, 0)),
        out_shape=jax.ShapeDtypeStruct((m, W_MIX), BF16),
        compiler_params=_params(1),
        name="mixer_b_sgu",
    )(proj, proj, norm_g.reshape(1, W_MIX), w_s, bexp)


def _qknorm_kernel(q_ref, k_ref, qg_ref, kg_ref, qo_ref, ko_ref):
    for src, g_ref, dst in ((q_ref, qg_ref, qo_ref), (k_ref, kg_ref, ko_ref)):
        g = g_ref[...]
        for h in range(SB_HEADS):
            cols = slice(h * SB_HEAD_DIM, (h + 1) * SB_HEAD_DIM)
            x = src[:, cols].astype(F32)
            ms = jnp.mean(x * x, axis=-1, keepdims=True)
            dst[:, cols] = (x * lax.rsqrt(ms + EPS) * g).astype(dst.dtype)


def _qknorm(proj, q_g, k_g, tm=1024):
    m = proj.shape[0]
    tm = min(tm, m)
    spec = lambda col: pl.BlockSpec((tm, W_MIX), lambda i: (i, col))
    gspec = pl.BlockSpec((1, SB_HEAD_DIM), lambda i: (0, 0))
    return pl.pallas_call(
        _qknorm_kernel,
        grid=(m // tm,),
        in_specs=[spec(COL_C_Q), spec(COL_C_K), gspec, gspec],
        out_specs=[spec(0), spec(0)],
        out_shape=[jax.ShapeDtypeStruct((m, W_MIX), BF16)] * 2,
        compiler_params=_params(1),
        name="qk_norm",
    )(proj, proj, q_g.reshape(1, -1), k_g.reshape(1, -1))


EXP2_ZERO_BELOW = -160.0
R_DEAD = -1e30


def _attn_kernel(q_ref, k_ref, v_ref, o_ref, acc_scr, *, tq, sub):
    i = pl.program_id(2)
    n_groups = tq // sub
    qs = [q_ref[g * sub:(g + 1) * sub, :] for g in range(n_groups)]
    jj = lax.broadcasted_iota(jnp.int32, (sub, sub), 0)
    ss = lax.broadcasted_iota(jnp.int32, (sub, sub), 1)
    uneg = jnp.where(jj > ss, -1.0, 0.0).astype(BF16)

    def tile(q, off, r, mask):
        rows = pl.ds(off, sub)
        z2 = lax.dot_general(q, k_ref[rows, :], (((1,), (1,)), ((), ())),
                             preferred_element_type=F32)
        neg_abs = pltpu.bitcast(pltpu.bitcast(z2, jnp.int32) | jnp.int32(-2 ** 31), F32)
        l2 = jnp.log(1.0 + jnp.exp2(neg_abs)) * (1.0 / math.log(2.0))
        sp = jnp.maximum(z2, 0.0) + l2
        lb = jnp.minimum(z2, 0.0) - l2
        if mask is not None:
            sp = jnp.where(mask, sp, 0.0)
        rev = jnp.dot(sp.astype(BF16), uneg, preferred_element_type=F32)
        a = jnp.exp2(lb + rev + r)
        if mask is not None:
            a = jnp.where(mask, a, 0.0)
        pv = jnp.dot(a.astype(BF16), v_ref[rows, :], preferred_element_type=F32)
        return pv, r - jnp.sum(sp, axis=1, keepdims=True)

    base = i * n_groups
    rs = []
    for g in range(n_groups):
        pv, r = tile(qs[g], pl.multiple_of((base + g) * sub, sub), jnp.zeros((sub, 1), F32), ss < jj)
        acc_scr[g * sub:(g + 1) * sub, :] = pv
        rs.append(r)

    n_steps = base + n_groups - 1

    def cond(carry):
        step, _, r_max = carry
        return jnp.logical_and(step < n_steps, r_max > EXP2_ZERO_BELOW)

    def body(carry):
        step, rs, _ = carry
        new_rs = []
        for g in range(n_groups):
            t_idx = base + g - 1 - step
            r_in = jnp.where(t_idx >= 0, rs[g], R_DEAD)
            pv, r_out = tile(qs[g], pl.multiple_of(jnp.maximum(t_idx, 0) * sub, sub), r_in, None)
            acc_scr[g * sub:(g + 1) * sub, :] += pv
            new_rs.append(r_out)
        r_max = functools.reduce(jnp.maximum, [jnp.max(r) for r in new_rs])
        return step + 1, tuple(new_rs), r_max

    r_max = functools.reduce(jnp.maximum, [jnp.max(r) for r in rs])
    lax.while_loop(cond, body, (jnp.int32(0), tuple(rs), r_max))
    o_ref[...] = acc_scr[...].astype(o_ref.dtype)


def _attention(qn, kn, proj, batch, seq, tq=2048, sub=256):
    m = qn.shape[0]
    tq = min(tq, seq)
    nq = seq // tq
    v_col0 = COL_C_V * (W_MIX // SB_HEAD_DIM)
    return pl.pallas_call(
        functools.partial(_attn_kernel, tq=tq, sub=sub),
        grid=(batch, SB_HEADS, nq),
        in_specs=[pl.BlockSpec((tq, SB_HEAD_DIM), lambda b, h, i: (b * nq + i, h)),
                  pl.BlockSpec((seq, SB_HEAD_DIM), lambda b, h, i: (b, h)),
                  pl.BlockSpec((seq, SB_HEAD_DIM), lambda b, h, i: (b, v_col0 + h))],
        out_specs=pl.BlockSpec((tq, SB_HEAD_DIM), lambda b, h, i: (b * nq + i, h)),
        out_shape=jax.ShapeDtypeStruct((m, W_MIX), BF16),
        scratch_shapes=[pltpu.VMEM((tq, SB_HEAD_DIM), F32)],
        compiler_params=_params(3),
        name="stick_breaking_attn",
    )(qn, kn, proj)


def _split3(x):
    h1 = x.astype(BF16)
    r1 = x - h1.astype(F32)
    h2 = r1.astype(BF16)
    h3 = (r1 - h2.astype(F32)).astype(BF16)
    return [h1, h2, h3]


def _ssd_kernel(z_ref, xbc_ref, halo_ref, dt_ref, cw_ref, cb_ref, dtb_ref, alog_ref, dexp_ref,
                ng_ref, xmat_ref, o_ref, ht_scr, xp_scr):
    c = pl.program_id(1)
    L = CHUNK
    h = BF16_SUBLANES
    gw = W_MIX // SSM_GROUPS

    @pl.when(c == 0)
    def _():
        ht_scr[...] = jnp.zeros_like(ht_scr)
        xp_scr[...] = jnp.zeros_like(xp_scr)

    xp_scr[L - h:L, :] = jnp.where(c == 0, jnp.zeros_like(halo_ref), halo_ref[...])
    xp_scr[L:2 * L, :] = xbc_ref[...]
    n_shift = SSM_CONV - 1
    out_row = lax.broadcasted_iota(jnp.int32, (n_shift * L, 2 * L), 0)
    src_row = lax.broadcasted_iota(jnp.int32, (n_shift * L, 2 * L), 1)
    delay = out_row // L + 1
    shift_mat = jnp.where(src_row == out_row - (delay - 1) * L + L - delay, 1.0, 0.0).astype(BF16)
    shifted = jnp.dot(shift_mat, xp_scr[...], preferred_element_type=F32)
    cw = cw_ref[...]
    conv = cb_ref[...] + cw[SSM_CONV - 1:SSM_CONV, :] * xbc_ref[...].astype(F32)
    for d in range(1, SSM_CONV):
        k = SSM_CONV - 1 - d
        conv = conv + cw[k:k + 1, :] * shifted[(d - 1) * L:d * L, :]
    xc = conv * _sigmoid(conv)
    xs = xc[:, :W_MIX]
    bmat = xc[:, W_MIX:W_MIX + SSM_GROUPS * SSM_STATE].astype(BF16)
    cmat = xc[:, W_MIX + SSM_GROUPS * SSM_STATE:].astype(BF16)

    dt = _softplus(dt_ref[...] + dtb_ref[...])
    da = dt * (-jnp.exp(alog_ref[...]))
    t_i = lax.broadcasted_iota(jnp.int32, (L, L), 0)
    s_i = lax.broadcasted_iota(jnp.int32, (L, L), 1)
    causal = s_i <= t_i
    tri3 = jnp.concatenate([causal.astype(BF16)] * 3, axis=1)
    cum = jnp.dot(tri3, jnp.concatenate(_split3(da), axis=0), preferred_element_type=F32)
    cum_t = cum.T
    xmat3 = xmat_ref[...]
    dt_e = jnp.dot(jnp.concatenate(_split3(dt), axis=1), xmat3, preferred_element_type=F32)
    cum_e = jnp.dot(jnp.concatenate(_split3(cum), axis=1), xmat3, preferred_element_type=F32)
    cl_e = cum_e[L - 1:L, :]
    xdt = xs * dt_e
    xdec = (xdt * jnp.exp(cl_e - cum_e)).astype(BF16)
    ecum_e = jnp.exp(cum_e)
    chunk_dec = jnp.exp(cl_e)
    lane = lax.broadcasted_iota(jnp.int32, (L, LANES), 1)
    left = lane < SSM_HEAD_DIM

    ys = []
    for g in range(SSM_GROUPS):
        bg = bmat[:, g * SSM_STATE:(g + 1) * SSM_STATE]
        cg = cmat[:, g * SSM_STATE:(g + 1) * SSM_STATE]
        gcols = slice(g * gw, (g + 1) * gw)
        cb = lax.dot_general(cg, bg, (((1,), (1,)), ((), ())), preferred_element_type=F32)
        ht = ht_scr[g]
        y_off = jnp.dot(cg, ht.astype(BF16), preferred_element_type=F32)
        parts = []
        for pr in range(SSM_HPG // 2):
            ms = []
            for r in range(2):
                idx = g * SSM_HPG + 2 * pr + r
                seg = cum[:, idx:idx + 1] - cum_t[idx:idx + 1, :]
                dec = jnp.exp(jnp.where(causal, seg, -jnp.inf))
                ms.append((cb * dec).astype(BF16))
            c0 = g * gw + pr * LANES
            xpair = xdt[:, c0:c0 + LANES]
            rhs = jnp.concatenate(
                [jnp.where(left, xpair, 0.0), jnp.where(left, 0.0, xpair)], axis=0).astype(BF16)
            parts.append(jnp.dot(jnp.concatenate(ms, axis=1), rhs, preferred_element_type=F32))
        y_diag = jnp.concatenate(parts, axis=1)
        ys.append(y_diag + y_off * ecum_e[:, gcols])
        st = lax.dot_general(bg, xdec[:, gcols], (((0,), (0,)), ((), ())),
                             preferred_element_type=F32)
        ht_scr[g] = ht * chunk_dec[:, gcols] + st

    y = jnp.concatenate(ys, axis=1) + xs * dexp_ref[...]
    z = z_ref[...].astype(F32)
    y = y * (z * _sigmoid(z))
    outs = []
    for g in range(SSM_GROUPS):
        yg = y[:, g * gw:(g + 1) * gw]
        ms = jnp.mean(yg * yg, axis=-1, keepdims=True)
        outs.append(yg * lax.rsqrt(ms + EPS))
    o_ref[...] = (jnp.concatenate(outs, axis=1) * ng_ref[...]).astype(o_ref.dtype)


def _mixer_d(proj, dt_raw, conv_w, conv_b, dt_bias, a_log, d_skip, norm_g, batch, seq):
    m = proj.shape[0]
    nc = seq // CHUNK
    h = BF16_SUBLANES
    pad = LANES - SSM_HEADS
    row = lambda v: v.reshape(1, -1)
    xmat = (jnp.arange(W_MIX)[None, :] // SSM_HEAD_DIM == jnp.arange(LANES)[:, None]).astype(BF16)
    xmat = jnp.concatenate([xmat] * 3, axis=0)
    const = lambda shape: pl.BlockSpec(shape, lambda b, c: (0,) * len(shape))
    return pl.pallas_call(
        _ssd_kernel,
        grid=(batch, nc),
        in_specs=[pl.BlockSpec((CHUNK, W_MIX), lambda b, c: (b * nc + c, COL_D_Z)),
                  pl.BlockSpec((CHUNK, SSM_CONV_DIM), lambda b, c: (b * nc + c, COL_D_XBC)),
                  pl.BlockSpec((h, SSM_CONV_DIM),
                               lambda b, c: (jnp.maximum((b * nc + c) * (CHUNK // h) - 1, 0), COL_D_XBC)),
                  pl.BlockSpec((CHUNK, LANES), lambda b, c: (b * nc + c, 0)),
                  const((SSM_CONV, SSM_CONV_DIM)), const((1, SSM_CONV_DIM)),
                  const((1, LANES)), const((1, LANES)), const((1, W_MIX)), const((1, W_MIX)),
                  const((3 * LANES, W_MIX))],
        out_specs=pl.BlockSpec((CHUNK, W_MIX), lambda b, c: (b * nc + c, 0)),
        out_shape=jax.ShapeDtypeStruct((m, W_MIX), BF16),
        scratch_shapes=[pltpu.VMEM((SSM_GROUPS, SSM_STATE, W_MIX // SSM_GROUPS), F32),
                        pltpu.VMEM((2 * CHUNK, SSM_CONV_DIM), BF16)],
        compiler_params=_params(2),
        name="mixer_d_ssd",
    )(proj, proj, proj, dt_raw, conv_w, row(conv_b),
      row(jnp.pad(dt_bias, (0, pad))), row(jnp.pad(a_log, (0, pad))),
      row(jnp.repeat(d_skip, SSM_HEAD_DIM)), row(norm_g), xmat)


def _merge_kernel(ya_ref, yb_ref, yc_ref, yd_ref, wb_ref, g0_ref, g1_ref, g2_ref, g3_ref, wsrc_ref,
                  o_ref, wdst_ref):
    wdst_ref[...] = wsrc_ref[...].astype(wdst_ref.dtype)
    acc = None
    for j, (y_ref, g_ref) in enumerate(
            zip((ya_ref, yb_ref, yc_ref, yd_ref), (g0_ref, g1_ref, g2_ref, g3_ref))):
        t = jnp.dot(y_ref[...], wb_ref[j], preferred_element_type=F32) * g_ref[...].astype(F32)
        acc = t if acc is None else acc + t
    o_ref[...] = acc.astype(o_ref.dtype)


def _merge(ys, w_branch, gates, side_w, *, layer, bm=512, bn=1024):
    m = gates.shape[0]
    bm = min(bm, m)
    nb = D_MODEL // bn
    nm = m // bm
    side_in, side_out, side_shape = _side_cast_specs(side_w, layer, nb * nm, lambda n, i: n * nm + i)
    yspec = pl.BlockSpec((bm, W_MIX), lambda n, i: (i, 0))
    gspec = lambda j: pl.BlockSpec((bm, bn), lambda n, i: (i, j * nb + n))
    return pl.pallas_call(
        _merge_kernel,
        grid=(nb, m // bm),
        in_specs=[yspec] * N_BRANCH
        + [pl.BlockSpec((N_BRANCH, W_MIX, bn), lambda n, i: (0, 0, n))]
        + [gspec(j) for j in range(N_BRANCH)] + [side_in],
        out_specs=[pl.BlockSpec((bm, bn), lambda n, i: (i, n)), side_out],
        out_shape=[jax.ShapeDtypeStruct((m, D_MODEL), BF16), side_shape],
        compiler_params=_params(2),
        name="branch_merge",
    )(*ys, w_branch, gates, gates, gates, gates, side_w)


def _ffn_gu_kernel(h_ref, s_ref, wg_ref, wu_ref, wsrc_ref, o_ref, wdst_ref, wg_scr, wu_scr):
    j, i = pl.program_id(0), pl.program_id(1)
    chunk = wg_ref.shape[0]
    slot = j % 2

    def convert_chunk():
        rows = pl.ds(pl.multiple_of(i * chunk, chunk), chunk)
        wg_scr[slot, rows, :] = wg_ref[...].astype(wg_scr.dtype)
        wu_scr[slot, rows, :] = wu_ref[...].astype(wu_scr.dtype)

    @pl.when(j == 0)
    def _():
        convert_chunk()
        o_ref[...] = jnp.zeros_like(o_ref)

    @pl.when(j > 0)
    def _():
        convert_chunk()
        wg, wu = wg_scr[1 - slot], wu_scr[1 - slot]
        rows_per = h_ref.shape[0] // ROW_CHUNKS
        for c in range(ROW_CHUNKS):
            rows = slice(c * rows_per, (c + 1) * rows_per)
            h = h_ref[rows, :]
            sc = s_ref[rows, :]
            g = _row_scale(jnp.dot(h, wg, preferred_element_type=F32), sc)
            u = _row_scale(jnp.dot(h, wu, preferred_element_type=F32), sc)
            o_ref[rows, :] = (g * _sigmoid(g) * u).astype(o_ref.dtype)
        wdst_ref[...] = wsrc_ref[...].astype(wdst_ref.dtype)


def _ffn_gate_up(h, s, wg, wu, side_w, *, layer, bm=2048, bn=256):
    m, k = h.shape
    f = wg.shape[2]
    bm = min(bm, m)
    nm = m // bm
    nb = f // bn
    chunk = k // nm
    assert chunk * nm == k and chunk % BF16_SUBLANES == 0
    wspec = pl.BlockSpec((None, chunk, bn), lambda j, i: (layer, i, jnp.minimum(j, nb - 1)))
    side_in, side_out, side_shape = _side_cast_specs(
        side_w, layer, nb * nm, lambda j, i: jnp.where(j == 0, 0, (j - 1) * nm + i))
    return pl.pallas_call(
        _ffn_gu_kernel,
        grid=(nb + 1, nm),
        in_specs=[pl.BlockSpec((bm, k), lambda j, i: (i, 0)),
                  pl.BlockSpec((bm, LANES), lambda j, i: (i, 0)), wspec, wspec, side_in],
        out_specs=[pl.BlockSpec((bm, bn), _WARMUP_OUT_INDEX), side_out],
        out_shape=[jax.ShapeDtypeStruct((m, f), BF16), side_shape],
        scratch_shapes=[pltpu.VMEM((2, k, bn), BF16)] * 2,
        compiler_params=_params(2),
        name="ffn_gate_up",
    )(h, s, wg, wu, side_w)


def kernel(x, norm_mix, w_in, conv_a, sgu_norm, sgu_w, sgu_b, q_norm, k_norm, ssm_conv_w, ssm_conv_b, ssm_dt_bias, ssm_a_log, ssm_d, ssm_norm, w_branch, w_out, norm_ffn, w_ffn_gate, w_ffn_up, w_ffn_down):
    batch, seq, d = x.shape
    m = batch * seq
    depth = w_in.shape[0]
    xr = x.reshape(m, d)
    q_fold = (1.0 / math.sqrt(SB_HEAD_DIM)) * (1.0 / math.log(2.0))
    w_in_t = jnp.swapaxes(w_in, 1, 2)
    w_branch_2d = w_branch.reshape(depth, N_BRANCH * W_MIX, d)
    w_dt = jnp.pad(w_in[:, :, N_MIX:N_MIX + SSM_HEADS],
                   ((0, 0), (0, 0), (0, LANES - SSM_HEADS))).astype(BF16)
    xb, s, dt_raw = _norm_prep(xr, norm_mix[0], w_dt[0])
    for l in range(depth):
        proj = _in_proj_stream(xb, s, w_in_t, layer=l, row0=0, shift=0, n=N_MIX, bm=1024, bn=768,
                               out_dtype=BF16, name="in_proj_mix")
        gates, w_branch_bf = _in_proj_stream(
            xb, s, w_in_t, layer=l, row0=N_MIX, shift=GATE_SHIFT, n=N_BRANCH * d, bm=1024, bn=1024,
            out_dtype=BF16, act="sigmoid", side_w=w_branch_2d, name="in_proj_gates")

        y_a = _mixer_a(proj, conv_a[l], seq)
        y_b = _mixer_b(proj, sgu_norm[l], sgu_w[l], sgu_b[l])
        qn, kn = _qknorm(proj, q_norm[l] * q_fold, k_norm[l])
        y_c = _attention(qn, kn, proj, batch, seq)
        y_d = _mixer_d(proj, dt_raw, ssm_conv_w[l], ssm_conv_b[l], ssm_dt_bias[l], ssm_a_log[l],
                       ssm_d[l], ssm_norm[l], batch, seq)

        merged, w_out_bf = _merge((y_a, y_b, y_c, y_d), w_branch_bf.reshape(N_BRANCH, W_MIX, d), gates,
                                  w_out, layer=l)
        xr, xb, s = _resid_matmul(merged, w_out_bf, xr, norm_ffn[l], bm=1024, bn=512, name="out_proj")

        act, w_down_bf = _ffn_gate_up(xb, s, w_ffn_gate, w_ffn_up, w_ffn_down, layer=l)
        if l + 1 < depth:
            xr, xb, s = _resid_matmul(act, w_down_bf, xr, norm_mix[l + 1], bm=512, bn=512,
                                      name="ffn_down")
            dt_raw = _dt_proj(xb, s, w_dt[l + 1])
        else:
            xr = _resid_matmul(act, w_down_bf, xr, None, bm=512, bn=512, name="ffn_down")
    return xr.reshape(batch, seq, d)
```

```python
import functools
import math

import jax
import jax.numpy as jnp
from jax import lax
from jax.experimental import pallas as pl
from jax.experimental.pallas import tpu as pltpu

F32 = jnp.float32
BF16 = jnp.bfloat16

EPS = 1e-6
D_MODEL = 4096
W_MIX = D_MODEL // 4
GROUP = 128
CHUNK = 128
CONV_A = 3
SGU_GROUPS = W_MIX // GROUP
SB_HEAD_DIM = 128
SB_HEADS = W_MIX // SB_HEAD_DIM
SSM_HEAD_DIM = 64
SSM_HEADS = W_MIX // SSM_HEAD_DIM
SSM_GROUPS = 2
SSM_HPG = SSM_HEADS // SSM_GROUPS
SSM_STATE = 128
SSM_CONV = 4
SSM_CONV_DIM = W_MIX + 2 * SSM_GROUPS * SSM_STATE
N_BRANCH = 4
N_MIX = 9 * W_MIX + SSM_CONV_DIM
GATE_SHIFT = SSM_HEADS
LANES = 128
BF16_SUBLANES = 16
VMEM_LIMIT = 56 * 1024 * 1024

COL_A_B, COL_A_C, COL_A_X = 0, 1, 2
COL_B_U, COL_B_V = 3, 4
COL_C_Q, COL_C_K, COL_C_V = 5, 6, 7
COL_D_Z = 8
COL_D_XBC = (9 * W_MIX) // SSM_CONV_DIM


def _params(n_axes):
    return pltpu.CompilerParams(
        dimension_semantics=("arbitrary",) * n_axes, vmem_limit_bytes=VMEM_LIMIT)


def _sigmoid(x):
    return 1.0 / (1.0 + jnp.exp(-x))


def _softplus(x):
    return jnp.maximum(x, 0.0) + jnp.log1p(jnp.exp(-jnp.abs(x)))


def _gelu_tanh(x):
    c = math.sqrt(2.0 / math.pi)
    return x * (0.5 * (1.0 + jnp.tanh(c * (x + 0.044715 * (x * x * x)))))


def _norm_prep_kernel(x_ref, g_ref, wdt_ref, xb_ref, s_ref, dt_ref):
    x = x_ref[...]
    ms = jnp.mean(x * x, axis=-1, keepdims=True)
    xb = (x * g_ref[...]).astype(xb_ref.dtype)
    scale = jnp.broadcast_to(lax.rsqrt(ms + EPS), s_ref.shape)
    xb_ref[...] = xb
    s_ref[...] = scale
    dt_ref[...] = jnp.dot(xb, wdt_ref[...], preferred_element_type=F32) * scale


def _norm_prep(x, g, w_dt, tm=256):
    m, d = x.shape
    return pl.pallas_call(
        _norm_prep_kernel,
        grid=(m // tm,),
        in_specs=[pl.BlockSpec((tm, d), lambda i: (i, 0)),
                  pl.BlockSpec((1, d), lambda i: (0, 0)),
                  pl.BlockSpec((d, LANES), lambda i: (0, 0))],
        out_specs=[pl.BlockSpec((tm, d), lambda i: (i, 0)),
                   pl.BlockSpec((tm, LANES), lambda i: (i, 0)),
                   pl.BlockSpec((tm, LANES), lambda i: (i, 0))],
        out_shape=[jax.ShapeDtypeStruct((m, d), BF16), jax.ShapeDtypeStruct((m, LANES), F32),
                   jax.ShapeDtypeStruct((m, LANES), F32)],
        compiler_params=_params(1),
        name="norm_prep",
    )(x, g.reshape(1, d), w_dt)


def _dt_proj_kernel(xb_ref, s_ref, wdt_ref, o_ref):
    o_ref[...] = jnp.dot(xb_ref[...], wdt_ref[...], preferred_element_type=F32) * s_ref[...]


def _dt_proj(xb, s, w_dt, tm=1024):
    m, d = xb.shape
    tm = min(tm, m)
    return pl.pallas_call(
        _dt_proj_kernel,
        grid=(m // tm,),
        in_specs=[pl.BlockSpec((tm, d), lambda i: (i, 0)),
                  pl.BlockSpec((tm, LANES), lambda i: (i, 0)),
                  pl.BlockSpec((d, LANES), lambda i: (0, 0))],
        out_specs=pl.BlockSpec((tm, LANES), lambda i: (i, 0)),
        out_shape=jax.ShapeDtypeStruct((m, LANES), F32),
        compiler_params=_params(1),
        name="dt_proj",
    )(xb, s, w_dt)


ROW_CHUNKS = 4


def _row_scale(acc, s):
    reps = acc.shape[1] // LANES
    return acc * (s if reps == 1 else jnp.concatenate([s] * reps, axis=1))


def _resid_mm_kernel(*refs, norm, n_steps, n_cols):
    a_ref, w_ref, r_ref = refs[0], refs[1], refs[2]
    acc = jnp.dot(a_ref[...], w_ref[...], preferred_element_type=F32) + r_ref[...]
    if not norm:
        refs[3][...] = acc
        return
    g_ref, o_ref, xb_ref, s_ref = refs[3], refs[4], refs[5], refs[6]
    j = pl.program_id(1)
    o_ref[...] = acc
    xb_ref[...] = (acc * g_ref[...]).astype(xb_ref.dtype)
    sq = acc * acc
    part = sq[:, 0:LANES]
    for c in range(1, acc.shape[1] // LANES):
        part = part + sq[:, c * LANES:(c + 1) * LANES]

    @pl.when(j == 0)
    def _():
        s_ref[...] = part

    @pl.when(j != 0)
    def _():
        s_ref[...] += part

    @pl.when(j == n_steps - 1)
    def _():
        tot = jnp.sum(s_ref[...], axis=1, keepdims=True)
        s_ref[...] = jnp.broadcast_to(lax.rsqrt(tot * (1.0 / n_cols) + EPS), s_ref.shape)


def _resid_matmul(a, w, res, g_next, *, bm, bn, name):
    m, k = a.shape
    n = w.shape[1]
    bm = min(bm, m)
    norm = g_next is not None
    ij = lambda i, j: (i, j)
    in_specs = [pl.BlockSpec((bm, k), lambda i, j: (i, 0)),
                pl.BlockSpec((k, bn), lambda i, j: (0, j)),
                pl.BlockSpec((bm, bn), ij)]
    args = [a, w, res]
    out_specs = [pl.BlockSpec((bm, bn), ij)]
    out_shape = [jax.ShapeDtypeStruct((m, n), F32)]
    if norm:
        in_specs.append(pl.BlockSpec((1, bn), lambda i, j: (0, j)))
        args.append(g_next.reshape(1, n))
        out_specs += [pl.BlockSpec((bm, bn), ij), pl.BlockSpec((bm, LANES), lambda i, j: (i, 0))]
        out_shape += [jax.ShapeDtypeStruct((m, n), BF16), jax.ShapeDtypeStruct((m, LANES), F32)]
    out = pl.pallas_call(
        functools.partial(_resid_mm_kernel, norm=norm, n_steps=n // bn, n_cols=n),
        grid=(m // bm, n // bn),
        in_specs=in_specs,
        out_specs=out_specs,
        out_shape=out_shape,
        compiler_params=_params(2),
        name=name,
    )(*args)
    return out if norm else out[0]


def _side_cast_specs(w, layer, n_steps, step_of):
    rows, cols = w.shape[1], w.shape[2]
    chunk = rows // n_steps
    assert chunk * n_steps == rows and chunk % BF16_SUBLANES == 0
    return (pl.BlockSpec((None, chunk, cols), lambda *g: (layer, step_of(*g), 0)),
            pl.BlockSpec((chunk, cols), lambda *g: (step_of(*g), 0)),
            jax.ShapeDtypeStruct((rows, cols), BF16))


def _WARMUP_OUT_INDEX(j, i):
    return jnp.where(j == 0, 0, i), jnp.maximum(j - 1, 0)


def _in_proj_stream_kernel(*refs, shift, act, n_side_in, n_side_out, side_fn, n_steps_i):
    n_in = 3 + bool(shift)
    a_ref, s_ref, wc_ref = refs[0], refs[1], refs[2]
    side_in = refs[n_in:n_in + n_side_in]
    o_ref = refs[n_in + n_side_in]
    side_out = refs[n_in + n_side_in + 1:n_in + n_side_in + 1 + n_side_out]
    w_scr = refs[n_in + n_side_in + 1 + n_side_out]
    side_scr = refs[n_in + n_side_in + 2 + n_side_out:]
    j, i = pl.program_id(0), pl.program_id(1)
    chunk = wc_ref.shape[0]
    slot = j % 2

    def convert_chunk():
        src = wc_ref[...]
        if shift:
            src = jnp.concatenate([src[shift:, :], refs[3][...]], axis=0)
        w_scr[slot, pl.ds(pl.multiple_of(i * chunk, chunk), chunk), :] = src.astype(w_scr.dtype)

    @pl.when(j == 0)
    def _():
        convert_chunk()
        o_ref[...] = jnp.zeros_like(o_ref)

    @pl.when(j > 0)
    def _():
        convert_chunk()
        rows_per = a_ref.shape[0] // ROW_CHUNKS
        w = w_scr[1 - slot]
        for c in range(ROW_CHUNKS):
            rows = slice(c * rows_per, (c + 1) * rows_per)
            acc = lax.dot_general(a_ref[rows, :], w, (((1,), (1,)), ((), ())),
                                  preferred_element_type=F32)
            acc = _row_scale(acc, s_ref[rows, :])
            if act == "sigmoid":
                acc = _sigmoid(acc)
            o_ref[rows, :] = acc.astype(o_ref.dtype)
        if side_fn is not None:
            side_fn((j - 1) * n_steps_i + i, side_in, side_out, side_scr)


def _in_proj_stream(a, s, w_t, *, layer, row0, shift, n, bm, bn, out_dtype, act=None, side=None, name):
    m, k = a.shape
    bm = min(bm, m)
    nm = m // bm
    nb = n // bn
    chunk = bn // nm
    assert chunk * nm == bn and chunk % BF16_SUBLANES == 0 and row0 % chunk == 0
    wblk = lambda j: jnp.minimum(j, nb - 1)
    in_specs = [pl.BlockSpec((bm, k), lambda j, i: (i, 0)),
                pl.BlockSpec((bm, LANES), lambda j, i: (i, 0)),
                pl.BlockSpec((None, chunk, k), lambda j, i: (layer, row0 // chunk + wblk(j) * nm + i, 0))]
    args = [a, s, w_t]
    if shift:
        in_specs.append(pl.BlockSpec(
            (None, shift, k),
            lambda j, i: (layer, (row0 + wblk(j) * bn + (i + 1) * chunk) // shift, 0)))
        args.append(w_t)
    out_specs = [pl.BlockSpec((bm, bn), _WARMUP_OUT_INDEX)]
    out_shape = [jax.ShapeDtypeStruct((m, n), out_dtype)]
    scratch = [pltpu.VMEM((2, bn, k), BF16)]
    side_fn, n_side_in, n_side_out = None, 0, 0
    if side is not None:
        side_fn, s_in, s_args, s_out, s_shape, s_scr = side(
            nb * nm, lambda j, i: jnp.where(j == 0, 0, (j - 1) * nm + i))
        in_specs += s_in
        args += s_args
        out_specs += s_out
        out_shape += s_shape
        scratch += s_scr
        n_side_in, n_side_out = len(s_in), len(s_out)
    out = pl.pallas_call(
        functools.partial(_in_proj_stream_kernel, shift=shift, act=act, n_side_in=n_side_in,
                          n_side_out=n_side_out, side_fn=side_fn, n_steps_i=nm),
        grid=(nb + 1, nm),
        in_specs=in_specs,
        out_specs=out_specs,
        out_shape=out_shape,
        scratch_shapes=scratch,
        compiler_params=_params(2),
        name=name,
    )(*args)
    return out if side is not None else out[0]


def _mixer_side_jobs(w_branch_2d, layer, proj, conv_w, q_g, k_g, seq):
    def build(n_steps, step_of):
        m = proj.shape[0]
        rows = m // n_steps
        h = BF16_SUBLANES
        assert rows * n_steps == m and rows % h == 0 and seq % rows == 0
        cast_in, cast_out, cast_shape = _side_cast_specs(w_branch_2d, layer, n_steps, step_of)
        main = lambda col: pl.BlockSpec((rows, W_MIX), lambda *g: (step_of(*g), col))
        halo = lambda col: pl.BlockSpec(
            (h, W_MIX), lambda *g: (jnp.maximum(step_of(*g) * (rows // h) - 1, 0), col))
        const = lambda shape: pl.BlockSpec(shape, lambda *g: (0,) * len(shape))
        in_specs = [cast_in, main(COL_A_B), main(COL_A_C), main(COL_A_X), halo(COL_A_C), halo(COL_A_X),
                    const((CONV_A, W_MIX)), main(COL_C_Q), main(COL_C_K),
                    const((1, SB_HEAD_DIM)), const((1, SB_HEAD_DIM))]
        args = [w_branch_2d, proj, proj, proj, proj, proj, conv_w, proj, proj,
                q_g.reshape(1, -1), k_g.reshape(1, -1)]
        out_row = pl.BlockSpec((rows, W_MIX), lambda *g: (step_of(*g), 0))
        out_specs = [cast_out, out_row, out_row, out_row]
        out_shapes = [cast_shape] + [jax.ShapeDtypeStruct((m, W_MIX), BF16)] * 3
        scratch = [pltpu.VMEM((rows + h, W_MIX), F32)]

        def fn(t, ins, outs, scr):
            outs[0][...] = ins[0][...].astype(BF16)
            _conv_a_body(t % (seq // rows) == 0, *ins[1:7], outs[1], scr[0])
            _qknorm_kernel(ins[7], ins[8], ins[9], ins[10], outs[2], outs[3])

        return fn, in_specs, args, out_specs, out_shapes, scratch
    return build


def _conv_a_body(seq_start, b_ref, c_ref, x_ref, ch_ref, xh_ref, w_ref, o_ref, p_scr):
    tm = b_ref.shape[0]
    h = BF16_SUBLANES
    p = c_ref[...].astype(F32) * x_ref[...].astype(F32)
    ph = ch_ref[...].astype(F32) * xh_ref[...].astype(F32)
    ph = jnp.where(seq_start, 0.0, ph)
    p_scr[0:h, :] = ph
    p_scr[h:h + tm, :] = p
    w = w_ref[...]
    y = w[2:3, :] * p + w[1:2, :] * p_scr[h - 1:h - 1 + tm, :] + w[0:1, :] * p_scr[h - 2:h - 2 + tm, :]
    o_ref[...] = (b_ref[...].astype(F32) * y).astype(o_ref.dtype)


def _sgu_kernel(u_ref, v_ref, ng_ref, w_ref, bexp_ref, o_ref):
    tm = u_ref.shape[0]
    vf = _gelu_tanh(v_ref[...].astype(F32))
    mu = jnp.mean(vf, axis=-1, keepdims=True)
    d = vf - mu
    var = jnp.mean(d * d, axis=-1, keepdims=True)
    vn = (d * lax.rsqrt(var + EPS) * ng_ref[...]).astype(BF16)
    t_idx = lax.broadcasted_iota(jnp.int32, (CHUNK, CHUNK), 0)
    s_idx = lax.broadcasted_iota(jnp.int32, (CHUNK, CHUNK), 1)
    causal = s_idx <= t_idx
    ws = [jnp.where(causal, w_ref[g], 0.0).astype(BF16) for g in range(SGU_GROUPS)]
    bexp = bexp_ref[...]
    for c in range(tm // CHUNK):
        rows = slice(c * CHUNK, (c + 1) * CHUNK)
        vn_c = vn[rows, :]
        mixed = jnp.concatenate(
            [jnp.dot(ws[g], vn_c[:, g * GROUP:(g + 1) * GROUP], preferred_element_type=F32)
             for g in range(SGU_GROUPS)], axis=1)
        u = _gelu_tanh(u_ref[rows, :].astype(F32))
        o_ref[rows, :] = (u * (mixed + bexp)).astype(o_ref.dtype)


def _mixer_b(proj, norm_g, w_s, b_s, tm=1024):
    m = proj.shape[0]
    tm = min(tm, m)
    bexp = jnp.repeat(b_s.T, GROUP, axis=1)
    return pl.pallas_call(
        _sgu_kernel,
        grid=(m // tm,),
        in_specs=[pl.BlockSpec((tm, W_MIX), lambda i: (i, COL_B_U)),
                  pl.BlockSpec((tm, W_MIX), lambda i: (i, COL_B_V)),
                  pl.BlockSpec((1, W_MIX), lambda i: (0, 0)),
                  pl.BlockSpec((SGU_GROUPS, CHUNK, CHUNK), lambda i: (0, 0, 0)),
                  pl.BlockSpec((CHUNK, W_MIX), lambda i: (0, 0))],
        out_specs=pl.BlockSpec((tm, W_MIX), lambda i: (i, 0)),
        out_shape=jax.ShapeDtypeStruct((m, W_MIX), BF16),
        compiler_params=_params(1),
        name="mixer_b_sgu",
    )(proj, proj, norm_g.reshape(1, W_MIX), w_s, bexp)


def _qknorm_kernel(q_ref, k_ref, qg_ref, kg_ref, qo_ref, ko_ref):
    for src, g_ref, dst in ((q_ref, qg_ref, qo_ref), (k_ref, kg_ref, ko_ref)):
        g = g_ref[...]
        for h in range(SB_HEADS):
            cols = slice(h * SB_HEAD_DIM, (h + 1) * SB_HEAD_DIM)
            x = src[:, cols].astype(F32)
            ms = jnp.mean(x * x, axis=-1, keepdims=True)
            dst[:, cols] = (x * lax.rsqrt(ms + EPS) * g).astype(dst.dtype)


EXP2_ZERO_BELOW = -160.0
R_DEAD = -1e30


def _attn_kernel(q_ref, k_ref, v_ref, o_ref, acc_scr, *, tq, sub):
    i = pl.program_id(2)
    n_groups = tq // sub
    qs = [q_ref[g * sub:(g + 1) * sub, :] for g in range(n_groups)]
    jj = lax.broadcasted_iota(jnp.int32, (sub, sub), 0)
    ss = lax.broadcasted_iota(jnp.int32, (sub, sub), 1)
    uneg = jnp.where(jj > ss, -1.0, 0.0).astype(BF16)

    def tile(q, off, r, mask):
        rows = pl.ds(off, sub)
        z2 = lax.dot_general(q, k_ref[rows, :], (((1,), (1,)), ((), ())),
                             preferred_element_type=F32)
        neg_abs = pltpu.bitcast(pltpu.bitcast(z2, jnp.int32) | jnp.int32(-2 ** 31), F32)
        l2 = jnp.log(1.0 + jnp.exp2(neg_abs)) * (1.0 / math.log(2.0))
        sp = jnp.maximum(z2, 0.0) + l2
        lb = jnp.minimum(z2, 0.0) - l2
        if mask is not None:
            sp = jnp.where(mask, sp, 0.0)
        rev = jnp.dot(sp.astype(BF16), uneg, preferred_element_type=F32)
        a = jnp.exp2(lb + rev + r)
        if mask is not None:
            a = jnp.where(mask, a, 0.0)
        pv = jnp.dot(a.astype(BF16), v_ref[rows, :], preferred_element_type=F32)
        return pv, r - jnp.sum(sp, axis=1, keepdims=True)

    base = i * n_groups
    rs = []
    for g in range(n_groups):
        pv, r = tile(qs[g], pl.multiple_of((base + g) * sub, sub), jnp.zeros((sub, 1), F32), ss < jj)
        acc_scr[g * sub:(g + 1) * sub, :] = pv
        rs.append(r)

    n_steps = base + n_groups - 1

    def cond(carry):
        step, _, r_max = carry
        return jnp.logical_and(step < n_steps, r_max > EXP2_ZERO_BELOW)

    def body(carry):
        step, rs, _ = carry
        new_rs = []
        for g in range(n_groups):
            t_idx = base + g - 1 - step
            r_in = jnp.where(t_idx >= 0, rs[g], R_DEAD)
            pv, r_out = tile(qs[g], pl.multiple_of(jnp.maximum(t_idx, 0) * sub, sub), r_in, None)
            acc_scr[g * sub:(g + 1) * sub, :] += pv
            new_rs.append(r_out)
        r_max = functools.reduce(jnp.maximum, [jnp.max(r) for r in new_rs])
        return step + 1, tuple(new_rs), r_max

    r_max = functools.reduce(jnp.maximum, [jnp.max(r) for r in rs])
    lax.while_loop(cond, body, (jnp.int32(0), tuple(rs), r_max))
    o_ref[...] = acc_scr[...].astype(o_ref.dtype)


def _attention(qn, kn, proj, batch, seq, tq=2048, sub=256):
    m = qn.shape[0]
    tq = min(tq, seq)
    nq = seq // tq
    v_col0 = COL_C_V * (W_MIX // SB_HEAD_DIM)
    return pl.pallas_call(
        functools.partial(_attn_kernel, tq=tq, sub=sub),
        grid=(batch, SB_HEADS, nq),
        in_specs=[pl.BlockSpec((tq, SB_HEAD_DIM), lambda b, h, i: (b * nq + i, h)),
                  pl.BlockSpec((seq, SB_HEAD_DIM), lambda b, h, i: (b, h)),
                  pl.BlockSpec((seq, SB_HEAD_DIM), lambda b, h, i: (b, v_col0 + h))],
        out_specs=pl.BlockSpec((tq, SB_HEAD_DIM), lambda b, h, i: (b * nq + i, h)),
        out_shape=jax.ShapeDtypeStruct((m, W_MIX), BF16),
        scratch_shapes=[pltpu.VMEM((tq, SB_HEAD_DIM), F32)],
        compiler_params=_params(3),
        name="stick_breaking_attn",
    )(qn, kn, proj)


def _split3(x):
    h1 = x.astype(BF16)
    r1 = x - h1.astype(F32)
    h2 = r1.astype(BF16)
    h3 = (r1 - h2.astype(F32)).astype(BF16)
    return [h1, h2, h3]


def _ssd_kernel(z_ref, xbc_ref, halo_ref, dt_ref, cw_ref, cb_ref, dtb_ref, alog_ref, dexp_ref,
                ng_ref, xmat_ref, o_ref, ht_scr, xp_scr):
    c = pl.program_id(1)
    L = CHUNK
    h = BF16_SUBLANES
    gw = W_MIX // SSM_GROUPS

    @pl.when(c == 0)
    def _():
        ht_scr[...] = jnp.zeros_like(ht_scr)
        xp_scr[...] = jnp.zeros_like(xp_scr)

    xp_scr[L - h:L, :] = jnp.where(c == 0, jnp.zeros_like(halo_ref), halo_ref[...])
    xp_scr[L:2 * L, :] = xbc_ref[...]
    n_shift = SSM_CONV - 1
    out_row = lax.broadcasted_iota(jnp.int32, (n_shift * L, 2 * L), 0)
    src_row = lax.broadcasted_iota(jnp.int32, (n_shift * L, 2 * L), 1)
    delay = out_row // L + 1
    shift_mat = jnp.where(src_row == out_row - (delay - 1) * L + L - delay, 1.0, 0.0).astype(BF16)
    shifted = jnp.dot(shift_mat, xp_scr[...], preferred_element_type=F32)
    cw = cw_ref[...]
    conv = cb_ref[...] + cw[SSM_CONV - 1:SSM_CONV, :] * xbc_ref[...].astype(F32)
    for d in range(1, SSM_CONV):
        k = SSM_CONV - 1 - d
        conv = conv + cw[k:k + 1, :] * shifted[(d - 1) * L:d * L, :]
    xc = conv * _sigmoid(conv)
    xs = xc[:, :W_MIX]
    bmat = xc[:, W_MIX:W_MIX + SSM_GROUPS * SSM_STATE].astype(BF16)
    cmat = xc[:, W_MIX + SSM_GROUPS * SSM_STATE:].astype(BF16)

    dt = _softplus(dt_ref[...] + dtb_ref[...])
    da = dt * (-jnp.exp(alog_ref[...]))
    t_i = lax.broadcasted_iota(jnp.int32, (L, L), 0)
    s_i = lax.broadcasted_iota(jnp.int32, (L, L), 1)
    causal = s_i <= t_i
    tri3 = jnp.concatenate([causal.astype(BF16)] * 3, axis=1)
    cum = jnp.dot(tri3, jnp.concatenate(_split3(da), axis=0), preferred_element_type=F32)
    cum_t = cum.T
    xmat3 = xmat_ref[...]
    dt_e = jnp.dot(jnp.concatenate(_split3(dt), axis=1), xmat3, preferred_element_type=F32)
    cum_e = jnp.dot(jnp.concatenate(_split3(cum), axis=1), xmat3, preferred_element_type=F32)
    cl_e = cum_e[L - 1:L, :]
    xdt = xs * dt_e
    xdec = (xdt * jnp.exp(cl_e - cum_e)).astype(BF16)
    ecum_e = jnp.exp(cum_e)
    chunk_dec = jnp.exp(cl_e)
    lane = lax.broadcasted_iota(jnp.int32, (L, LANES), 1)
    left = lane < SSM_HEAD_DIM

    ys = []
    for g in range(SSM_GROUPS):
        bg = bmat[:, g * SSM_STATE:(g + 1) * SSM_STATE]
        cg = cmat[:, g * SSM_STATE:(g + 1) * SSM_STATE]
        gcols = slice(g * gw, (g + 1) * gw)
        cb = lax.dot_general(cg, bg, (((1,), (1,)), ((), ())), preferred_element_type=F32)
        ht = ht_scr[g]
        y_off = jnp.dot(cg, ht.astype(BF16), preferred_element_type=F32)
        parts = []
        for pr in range(SSM_HPG // 2):
            ms = []
            for r in range(2):
                idx = g * SSM_HPG + 2 * pr + r
                seg = cum[:, idx:idx + 1] - cum_t[idx:idx + 1, :]
                dec = jnp.exp(jnp.where(causal, seg, -jnp.inf))
                ms.append((cb * dec).astype(BF16))
            c0 = g * gw + pr * LANES
            xpair = xdt[:, c0:c0 + LANES]
            rhs = jnp.concatenate(
                [jnp.where(left, xpair, 0.0), jnp.where(left, 0.0, xpair)], axis=0).astype(BF16)
            parts.append(jnp.dot(jnp.concatenate(ms, axis=1), rhs, preferred_element_type=F32))
        y_diag = jnp.concatenate(parts, axis=1)
        ys.append(y_diag + y_off * ecum_e[:, gcols])
        st = lax.dot_general(bg, xdec[:, gcols], (((0,), (0,)), ((), ())),
                             preferred_element_type=F32)
        ht_scr[g] = ht * chunk_dec[:, gcols] + st

    y = jnp.concatenate(ys, axis=1) + xs * dexp_ref[...]
    z = z_ref[...].astype(F32)
    y = y * (z * _sigmoid(z))
    outs = []
    for g in range(SSM_GROUPS):
        yg = y[:, g * gw:(g + 1) * gw]
        ms = jnp.mean(yg * yg, axis=-1, keepdims=True)
        outs.append(yg * lax.rsqrt(ms + EPS))
    o_ref[...] = (jnp.concatenate(outs, axis=1) * ng_ref[...]).astype(o_ref.dtype)


def _mixer_d(proj, dt_raw, conv_w, conv_b, dt_bias, a_log, d_skip, norm_g, batch, seq):
    m = proj.shape[0]
    nc = seq // CHUNK
    h = BF16_SUBLANES
    pad = LANES - SSM_HEADS
    row = lambda v: v.reshape(1, -1)
    xmat = (jnp.arange(W_MIX)[None, :] // SSM_HEAD_DIM == jnp.arange(LANES)[:, None]).astype(BF16)
    xmat = jnp.concatenate([xmat] * 3, axis=0)
    const = lambda shape: pl.BlockSpec(shape, lambda b, c: (0,) * len(shape))
    return pl.pallas_call(
        _ssd_kernel,
        grid=(batch, nc),
        in_specs=[pl.BlockSpec((CHUNK, W_MIX), lambda b, c: (b * nc + c, COL_D_Z)),
                  pl.BlockSpec((CHUNK, SSM_CONV_DIM), lambda b, c: (b * nc + c, COL_D_XBC)),
                  pl.BlockSpec((h, SSM_CONV_DIM),
                               lambda b, c: (jnp.maximum((b * nc + c) * (CHUNK // h) - 1, 0), COL_D_XBC)),
                  pl.BlockSpec((CHUNK, LANES), lambda b, c: (b * nc + c, 0)),
                  const((SSM_CONV, SSM_CONV_DIM)), const((1, SSM_CONV_DIM)),
                  const((1, LANES)), const((1, LANES)), const((1, W_MIX)), const((1, W_MIX)),
                  const((3 * LANES, W_MIX))],
        out_specs=pl.BlockSpec((CHUNK, W_MIX), lambda b, c: (b * nc + c, 0)),
        out_shape=jax.ShapeDtypeStruct((m, W_MIX), BF16),
        scratch_shapes=[pltpu.VMEM((SSM_GROUPS, SSM_STATE, W_MIX // SSM_GROUPS), F32),
                        pltpu.VMEM((2 * CHUNK, SSM_CONV_DIM), BF16)],
        compiler_params=_params(2),
        name="mixer_d_ssd",
    )(proj, proj, proj, dt_raw, conv_w, row(conv_b),
      row(jnp.pad(dt_bias, (0, pad))), row(jnp.pad(a_log, (0, pad))),
      row(jnp.repeat(d_skip, SSM_HEAD_DIM)), row(norm_g), xmat)


def _merge_kernel(ya_ref, yb_ref, yc_ref, yd_ref, wb_ref, g0_ref, g1_ref, g2_ref, g3_ref, wsrc_ref,
                  o_ref, wdst_ref):
    wdst_ref[...] = wsrc_ref[...].astype(wdst_ref.dtype)
    acc = None
    for j, (y_ref, g_ref) in enumerate(
            zip((ya_ref, yb_ref, yc_ref, yd_ref), (g0_ref, g1_ref, g2_ref, g3_ref))):
        t = jnp.dot(y_ref[...], wb_ref[j], preferred_element_type=F32) * g_ref[...].astype(F32)
        acc = t if acc is None else acc + t
    o_ref[...] = acc.astype(o_ref.dtype)


def _merge(ys, w_branch, gates, side_w, *, layer, bm=512, bn=1024):
    m = gates.shape[0]
    bm = min(bm, m)
    nb = D_MODEL // bn
    nm = m // bm
    side_in, side_out, side_shape = _side_cast_specs(side_w, layer, nb * nm, lambda n, i: n * nm + i)
    yspec = pl.BlockSpec((bm, W_MIX), lambda n, i: (i, 0))
    gspec = lambda j: pl.BlockSpec((bm, bn), lambda n, i: (i, j * nb + n))
    return pl.pallas_call(
        _merge_kernel,
        grid=(nb, m // bm),
        in_specs=[yspec] * N_BRANCH
        + [pl.BlockSpec((N_BRANCH, W_MIX, bn), lambda n, i: (0, 0, n))]
        + [gspec(j) for j in range(N_BRANCH)] + [side_in],
        out_specs=[pl.BlockSpec((bm, bn), lambda n, i: (i, n)), side_out],
        out_shape=[jax.ShapeDtypeStruct((m, D_MODEL), BF16), side_shape],
        compiler_params=_params(2),
        name="branch_merge",
    )(*ys, w_branch, gates, gates, gates, gates, side_w)


def _ffn_gu_kernel(h_ref, s_ref, wg_ref, wu_ref, wsrc_ref, o_ref, wdst_ref, wg_scr, wu_scr):
    j, i = pl.program_id(0), pl.program_id(1)
    chunk = wg_ref.shape[0]
    slot = j % 2

    def convert_chunk():
        rows = pl.ds(pl.multiple_of(i * chunk, chunk), chunk)
        wg_scr[slot, rows, :] = wg_ref[...].astype(wg_scr.dtype)
        wu_scr[slot, rows, :] = wu_ref[...].astype(wu_scr.dtype)

    @pl.when(j == 0)
    def _():
        convert_chunk()
        o_ref[...] = jnp.zeros_like(o_ref)

    @pl.when(j > 0)
    def _():
        convert_chunk()
        wg, wu = wg_scr[1 - slot], wu_scr[1 - slot]
        rows_per = h_ref.shape[0] // ROW_CHUNKS
        for c in range(ROW_CHUNKS):
            rows = slice(c * rows_per, (c + 1) * rows_per)
            h = h_ref[rows, :]
            sc = s_ref[rows, :]
            g = _row_scale(jnp.dot(h, wg, preferred_element_type=F32), sc)
            u = _row_scale(jnp.dot(h, wu, preferred_element_type=F32), sc)
            o_ref[rows, :] = (g * _sigmoid(g) * u).astype(o_ref.dtype)
        wdst_ref[...] = wsrc_ref[...].astype(wdst_ref.dtype)


def _ffn_gate_up(h, s, wg, wu, side_w, *, layer, bm=2048, bn=256):
    m, k = h.shape
    f = wg.shape[2]
    bm = min(bm, m)
    nm = m // bm
    nb = f // bn
    chunk = k // nm
    assert chunk * nm == k and chunk % BF16_SUBLANES == 0
    wspec = pl.BlockSpec((None, chunk, bn), lambda j, i: (layer, i, jnp.minimum(j, nb - 1)))
    side_in, side_out, side_shape = _side_cast_specs(
        side_w, layer, nb * nm, lambda j, i: jnp.where(j == 0, 0, (j - 1) * nm + i))
    return pl.pallas_call(
        _ffn_gu_kernel,
        grid=(nb + 1, nm),
        in_specs=[pl.BlockSpec((bm, k), lambda j, i: (i, 0)),
                  pl.BlockSpec((bm, LANES), lambda j, i: (i, 0)), wspec, wspec, side_in],
        out_specs=[pl.BlockSpec((bm, bn), _WARMUP_OUT_INDEX), side_out],
        out_shape=[jax.ShapeDtypeStruct((m, f), BF16), side_shape],
        scratch_shapes=[pltpu.VMEM((2, k, bn), BF16)] * 2,
        compiler_params=_params(2),
        name="ffn_gate_up",
    )(h, s, wg, wu, side_w)


def kernel(x, norm_mix, w_in, conv_a, sgu_norm, sgu_w, sgu_b, q_norm, k_norm, ssm_conv_w, ssm_conv_b, ssm_dt_bias, ssm_a_log, ssm_d, ssm_norm, w_branch, w_out, norm_ffn, w_ffn_gate, w_ffn_up, w_ffn_down):
    batch, seq, d = x.shape
    m = batch * seq
    depth = w_in.shape[0]
    xr = x.reshape(m, d)
    q_fold = (1.0 / math.sqrt(SB_HEAD_DIM)) * (1.0 / math.log(2.0))
    w_in_t = jnp.swapaxes(w_in, 1, 2)
    w_branch_2d = w_branch.reshape(depth, N_BRANCH * W_MIX, d)
    w_dt = jnp.pad(w_in[:, :, N_MIX:N_MIX + SSM_HEADS],
                   ((0, 0), (0, 0), (0, LANES - SSM_HEADS))).astype(BF16)
    xb, s, dt_raw = _norm_prep(xr, norm_mix[0], w_dt[0])
    for l in range(depth):
        proj = _in_proj_stream(xb, s, w_in_t, layer=l, row0=0, shift=0, n=N_MIX, bm=1024, bn=768,
                               out_dtype=BF16, name="in_proj_mix")
        gates, w_branch_bf, y_a, qn, kn = _in_proj_stream(
            xb, s, w_in_t, layer=l, row0=N_MIX, shift=GATE_SHIFT, n=N_BRANCH * d, bm=1024, bn=1024,
            out_dtype=BF16, act="sigmoid", name="in_proj_gates",
            side=_mixer_side_jobs(w_branch_2d, l, proj, conv_a[l], q_norm[l] * q_fold, k_norm[l], seq))

        y_b = _mixer_b(proj, sgu_norm[l], sgu_w[l], sgu_b[l])
        y_c = _attention(qn, kn, proj, batch, seq)
        y_d = _mixer_d(proj, dt_raw, ssm_conv_w[l], ssm_conv_b[l], ssm_dt_bias[l], ssm_a_log[l],
                       ssm_d[l], ssm_norm[l], batch, seq)

        merged, w_out_bf = _merge((y_a, y_b, y_c, y_d), w_branch_bf.reshape(N_BRANCH, W_MIX, d), gates,
                                  w_out, layer=l)
        xr, xb, s = _resid_matmul(merged, w_out_bf, xr, norm_ffn[l], bm=1024, bn=512, name="out_proj")

        act, w_down_bf = _ffn_gate_up(xb, s, w_ffn_gate, w_ffn_up, w_ffn_down, layer=l)
        if l + 1 < depth:
            xr, xb, s = _resid_matmul(act, w_down_bf, xr, norm_mix[l + 1], bm=512, bn=512,
                                      name="ffn_down")
            dt_raw = _dt_proj(xb, s, w_dt[l + 1])
        else:
            xr = _resid_matmul(act, w_down_bf, xr, None, bm=512, bn=512, name="ffn_down")
    return xr.reshape(batch, seq, d)
```

```python
import functools
import math

import jax
import jax.numpy as jnp
from jax import lax
from jax.experimental import pallas as pl
from jax.experimental.pallas import tpu as pltpu

F32 = jnp.float32
BF16 = jnp.bfloat16

EPS = 1e-6
D_MODEL = 4096
W_MIX = D_MODEL // 4
GROUP = 128
CHUNK = 128
CONV_A = 3
SGU_GROUPS = W_MIX // GROUP
SB_HEAD_DIM = 128
SB_HEADS = W_MIX // SB_HEAD_DIM
SSM_HEAD_DIM = 64
SSM_HEADS = W_MIX // SSM_HEAD_DIM
SSM_GROUPS = 2
SSM_HPG = SSM_HEADS // SSM_GROUPS
SSM_STATE = 128
SSM_CONV = 4
SSM_CONV_DIM = W_MIX + 2 * SSM_GROUPS * SSM_STATE
N_BRANCH = 4
N_MIX = 9 * W_MIX + SSM_CONV_DIM
GATE_SHIFT = SSM_HEADS
LANES = 128
BF16_SUBLANES = 16
VMEM_LIMIT = 56 * 1024 * 1024

TILES = dict(
    norm_prep=256, dt_proj=1024, sgu=1024,
    attn_q=4096, attn_keys=256,
    in_proj_mix=(1024, 768), in_proj_gates=(1024, 1024),
    merge=(512, 1024), out_proj=(1024, 512),
    ffn_gate_up=(2048, 256), ffn_down=(512, 512),
)

COL_A_B, COL_A_C, COL_A_X = 0, 1, 2
COL_B_U, COL_B_V = 3, 4
COL_C_Q, COL_C_K, COL_C_V = 5, 6, 7
COL_D_Z = 8
COL_D_XBC = (9 * W_MIX) // SSM_CONV_DIM


def _params(n_axes):
    return pltpu.CompilerParams(
        dimension_semantics=("arbitrary",) * n_axes, vmem_limit_bytes=VMEM_LIMIT)


def _sigmoid(x):
    return 1.0 / (1.0 + jnp.exp(-x))


def _softplus(x):
    return jnp.maximum(x, 0.0) + jnp.log1p(jnp.exp(-jnp.abs(x)))


def _gelu_tanh(x):
    c = math.sqrt(2.0 / math.pi)
    return x * (0.5 * (1.0 + jnp.tanh(c * (x + 0.044715 * (x * x * x)))))


def _norm_prep_kernel(x_ref, g_ref, wdt_ref, xb_ref, s_ref, dt_ref):
    x = x_ref[...]
    ms = jnp.mean(x * x, axis=-1, keepdims=True)
    xb = (x * g_ref[...]).astype(xb_ref.dtype)
    scale = jnp.broadcast_to(lax.rsqrt(ms + EPS), s_ref.shape)
    xb_ref[...] = xb
    s_ref[...] = scale
    dt_ref[...] = jnp.dot(xb, wdt_ref[...], preferred_element_type=F32) * scale


def _norm_prep(x, g, w_dt, tm=TILES["norm_prep"]):
    m, d = x.shape
    return pl.pallas_call(
        _norm_prep_kernel,
        grid=(m // tm,),
        in_specs=[pl.BlockSpec((tm, d), lambda i: (i, 0)),
                  pl.BlockSpec((1, d), lambda i: (0, 0)),
                  pl.BlockSpec((d, LANES), lambda i: (0, 0))],
        out_specs=[pl.BlockSpec((tm, d), lambda i: (i, 0)),
                   pl.BlockSpec((tm, LANES), lambda i: (i, 0)),
                   pl.BlockSpec((tm, LANES), lambda i: (i, 0))],
        out_shape=[jax.ShapeDtypeStruct((m, d), BF16), jax.ShapeDtypeStruct((m, LANES), F32),
                   jax.ShapeDtypeStruct((m, LANES), F32)],
        compiler_params=_params(1),
        name="norm_prep",
    )(x, g.reshape(1, d), w_dt)


def _dt_proj_kernel(xb_ref, s_ref, wdt_ref, o_ref):
    o_ref[...] = jnp.dot(xb_ref[...], wdt_ref[...], preferred_element_type=F32) * s_ref[...]


def _dt_proj(xb, s, w_dt, tm=TILES["dt_proj"]):
    m, d = xb.shape
    tm = min(tm, m)
    return pl.pallas_call(
        _dt_proj_kernel,
        grid=(m // tm,),
        in_specs=[pl.BlockSpec((tm, d), lambda i: (i, 0)),
                  pl.BlockSpec((tm, LANES), lambda i: (i, 0)),
                  pl.BlockSpec((d, LANES), lambda i: (0, 0))],
        out_specs=pl.BlockSpec((tm, LANES), lambda i: (i, 0)),
        out_shape=jax.ShapeDtypeStruct((m, LANES), F32),
        compiler_params=_params(1),
        name="dt_proj",
    )(xb, s, w_dt)


ROW_CHUNKS = 4


def _row_scale(acc, s):
    reps = acc.shape[1] // LANES
    return acc * (s if reps == 1 else jnp.concatenate([s] * reps, axis=1))


def _resid_mm_kernel(*refs, norm, n_steps, n_cols):
    a_ref, w_ref, r_ref = refs[0], refs[1], refs[2]
    acc = jnp.dot(a_ref[...], w_ref[...], preferred_element_type=F32) + r_ref[...]
    if not norm:
        refs[3][...] = acc
        return
    g_ref, o_ref, xb_ref, s_ref = refs[3], refs[4], refs[5], refs[6]
    j = pl.program_id(1)
    o_ref[...] = acc
    xb_ref[...] = (acc * g_ref[...]).astype(xb_ref.dtype)
    sq = acc * acc
    part = sq[:, 0:LANES]
    for c in range(1, acc.shape[1] // LANES):
        part = part + sq[:, c * LANES:(c + 1) * LANES]

    @pl.when(j == 0)
    def _():
        s_ref[...] = part

    @pl.when(j != 0)
    def _():
        s_ref[...] += part

    @pl.when(j == n_steps - 1)
    def _():
        tot = jnp.sum(s_ref[...], axis=1, keepdims=True)
        s_ref[...] = jnp.broadcast_to(lax.rsqrt(tot * (1.0 / n_cols) + EPS), s_ref.shape)


def _resid_matmul(a, w, res, g_next, *, name):
    m, k = a.shape
    n = w.shape[1]
    bm, bn = TILES[name]
    bm = min(bm, m)
    norm = g_next is not None
    ij = lambda i, j: (i, j)
    in_specs = [pl.BlockSpec((bm, k), lambda i, j: (i, 0)),
                pl.BlockSpec((k, bn), lambda i, j: (0, j)),
                pl.BlockSpec((bm, bn), ij)]
    args = [a, w, res]
    out_specs = [pl.BlockSpec((bm, bn), ij)]
    out_shape = [jax.ShapeDtypeStruct((m, n), F32)]
    if norm:
        in_specs.append(pl.BlockSpec((1, bn), lambda i, j: (0, j)))
        args.append(g_next.reshape(1, n))
        out_specs += [pl.BlockSpec((bm, bn), ij), pl.BlockSpec((bm, LANES), lambda i, j: (i, 0))]
        out_shape += [jax.ShapeDtypeStruct((m, n), BF16), jax.ShapeDtypeStruct((m, LANES), F32)]
    out = pl.pallas_call(
        functools.partial(_resid_mm_kernel, norm=norm, n_steps=n // bn, n_cols=n),
        grid=(m // bm, n // bn),
        in_specs=in_specs,
        out_specs=out_specs,
        out_shape=out_shape,
        compiler_params=_params(2),
        name=name,
    )(*args)
    return out if norm else out[0]


def _side_cast_specs(w, layer, n_steps, step_of):
    rows, cols = w.shape[1], w.shape[2]
    chunk = rows // n_steps
    assert chunk * n_steps == rows and chunk % BF16_SUBLANES == 0
    return (pl.BlockSpec((None, chunk, cols), lambda *g: (layer, step_of(*g), 0)),
            pl.BlockSpec((chunk, cols), lambda *g: (step_of(*g), 0)),
            jax.ShapeDtypeStruct((rows, cols), BF16))


def _WARMUP_OUT_INDEX(j, i):
    return jnp.where(j == 0, 0, i), jnp.maximum(j - 1, 0)


def _in_proj_stream_kernel(*refs, shift, act, n_side_in, n_side_out, side_fn, n_steps_i):
    n_in = 3 + bool(shift)
    a_ref, s_ref, wc_ref = refs[0], refs[1], refs[2]
    side_in = refs[n_in:n_in + n_side_in]
    o_ref = refs[n_in + n_side_in]
    side_out = refs[n_in + n_side_in + 1:n_in + n_side_in + 1 + n_side_out]
    w_scr = refs[n_in + n_side_in + 1 + n_side_out]
    side_scr = refs[n_in + n_side_in + 2 + n_side_out:]
    j, i = pl.program_id(0), pl.program_id(1)
    chunk = wc_ref.shape[0]
    slot = j % 2

    def convert_chunk():
        src = wc_ref[...]
        if shift:
            src = jnp.concatenate([src[shift:, :], refs[3][...]], axis=0)
        w_scr[slot, pl.ds(pl.multiple_of(i * chunk, chunk), chunk), :] = src.astype(w_scr.dtype)

    @pl.when(j == 0)
    def _():
        convert_chunk()
        o_ref[...] = jnp.zeros_like(o_ref)

    @pl.when(j > 0)
    def _():
        convert_chunk()
        rows_per = a_ref.shape[0] // ROW_CHUNKS
        w = w_scr[1 - slot]
        for c in range(ROW_CHUNKS):
            rows = slice(c * rows_per, (c + 1) * rows_per)
            acc = lax.dot_general(a_ref[rows, :], w, (((1,), (1,)), ((), ())),
                                  preferred_element_type=F32)
            acc = _row_scale(acc, s_ref[rows, :])
            if act == "sigmoid":
                acc = _sigmoid(acc)
            o_ref[rows, :] = acc.astype(o_ref.dtype)
        if side_fn is not None:
            side_fn((j - 1) * n_steps_i + i, side_in, side_out, side_scr)


def _in_proj_stream(a, s, w_t, *, layer, row0, shift, n, out_dtype, act=None, side=None, name):
    m, k = a.shape
    bm, bn = TILES[name]
    bm = min(bm, m)
    nm = m // bm
    nb = n // bn
    chunk = bn // nm
    assert chunk * nm == bn and chunk % BF16_SUBLANES == 0 and row0 % chunk == 0
    wblk = lambda j: jnp.minimum(j, nb - 1)
    in_specs = [pl.BlockSpec((bm, k), lambda j, i: (i, 0)),
                pl.BlockSpec((bm, LANES), lambda j, i: (i, 0)),
                pl.BlockSpec((None, chunk, k), lambda j, i: (layer, row0 // chunk + wblk(j) * nm + i, 0))]
    args = [a, s, w_t]
    if shift:
        in_specs.append(pl.BlockSpec(
            (None, shift, k),
            lambda j, i: (layer, (row0 + wblk(j) * bn + (i + 1) * chunk) // shift, 0)))
        args.append(w_t)
    out_specs = [pl.BlockSpec((bm, bn), _WARMUP_OUT_INDEX)]
    out_shape = [jax.ShapeDtypeStruct((m, n), out_dtype)]
    scratch = [pltpu.VMEM((2, bn, k), BF16)]
    side_fn, n_side_in, n_side_out = None, 0, 0
    if side is not None:
        side_fn, s_in, s_args, s_out, s_shape, s_scr = side(
            nb * nm, lambda j, i: jnp.where(j == 0, 0, (j - 1) * nm + i))
        in_specs += s_in
        args += s_args
        out_specs += s_out
        out_shape += s_shape
        scratch += s_scr
        n_side_in, n_side_out = len(s_in), len(s_out)
    out = pl.pallas_call(
        functools.partial(_in_proj_stream_kernel, shift=shift, act=act, n_side_in=n_side_in,
                          n_side_out=n_side_out, side_fn=side_fn, n_steps_i=nm),
        grid=(nb + 1, nm),
        in_specs=in_specs,
        out_specs=out_specs,
        out_shape=out_shape,
        scratch_shapes=scratch,
        compiler_params=_params(2),
        name=name,
    )(*args)
    return out if side is not None else out[0]


def _mixer_side_jobs(w_branch_2d, layer, proj, conv_w, q_g, k_g, seq):
    def build(n_steps, step_of):
        m = proj.shape[0]
        rows = m // n_steps
        h = BF16_SUBLANES
        assert rows * n_steps == m and rows % h == 0 and seq % rows == 0
        cast_in, cast_out, cast_shape = _side_cast_specs(w_branch_2d, layer, n_steps, step_of)
        main = lambda col: pl.BlockSpec((rows, W_MIX), lambda *g: (step_of(*g), col))
        halo = lambda col: pl.BlockSpec(
            (h, W_MIX), lambda *g: (jnp.maximum(step_of(*g) * (rows // h) - 1, 0), col))
        const = lambda shape: pl.BlockSpec(shape, lambda *g: (0,) * len(shape))
        in_specs = [cast_in, main(COL_A_B), main(COL_A_C), main(COL_A_X), halo(COL_A_C), halo(COL_A_X),
                    const((CONV_A, W_MIX)), main(COL_C_Q), main(COL_C_K),
                    const((1, SB_HEAD_DIM)), const((1, SB_HEAD_DIM))]
        args = [w_branch_2d, proj, proj, proj, proj, proj, conv_w, proj, proj,
                q_g.reshape(1, -1), k_g.reshape(1, -1)]
        out_row = pl.BlockSpec((rows, W_MIX), lambda *g: (step_of(*g), 0))
        out_specs = [cast_out, out_row, out_row, out_row]
        out_shapes = [cast_shape] + [jax.ShapeDtypeStruct((m, W_MIX), BF16)] * 3
        scratch = [pltpu.VMEM((rows + h, W_MIX), F32)]

        def fn(t, ins, outs, scr):
            outs[0][...] = ins[0][...].astype(BF16)
            _conv_a_body(t % (seq // rows) == 0, *ins[1:7], outs[1], scr[0])
            _qknorm_kernel(ins[7], ins[8], ins[9], ins[10], outs[2], outs[3])

        return fn, in_specs, args, out_specs, out_shapes, scratch
    return build


def _conv_a_body(seq_start, b_ref, c_ref, x_ref, ch_ref, xh_ref, w_ref, o_ref, p_scr):
    tm = b_ref.shape[0]
    h = BF16_SUBLANES
    p = c_ref[...].astype(F32) * x_ref[...].astype(F32)
    ph = ch_ref[...].astype(F32) * xh_ref[...].astype(F32)
    ph = jnp.where(seq_start, 0.0, ph)
    p_scr[0:h, :] = ph
    p_scr[h:h + tm, :] = p
    w = w_ref[...]
    y = w[2:3, :] * p + w[1:2, :] * p_scr[h - 1:h - 1 + tm, :] + w[0:1, :] * p_scr[h - 2:h - 2 + tm, :]
    o_ref[...] = (b_ref[...].astype(F32) * y).astype(o_ref.dtype)


def _sgu_kernel(u_ref, v_ref, ng_ref, w_ref, bexp_ref, o_ref):
    tm = u_ref.shape[0]
    vf = _gelu_tanh(v_ref[...].astype(F32))
    mu = jnp.mean(vf, axis=-1, keepdims=True)
    d = vf - mu
    var = jnp.mean(d * d, axis=-1, keepdims=True)
    vn = (d * lax.rsqrt(var + EPS) * ng_ref[...]).astype(BF16)
    t_idx = lax.broadcasted_iota(jnp.int32, (CHUNK, CHUNK), 0)
    s_idx = lax.broadcasted_iota(jnp.int32, (CHUNK, CHUNK), 1)
    causal = s_idx <= t_idx
    ws = [jnp.where(causal, w_ref[g], 0.0).astype(BF16) for g in range(SGU_GROUPS)]
    bexp = bexp_ref[...]
    for c in range(tm // CHUNK):
        rows = slice(c * CHUNK, (c + 1) * CHUNK)
        vn_c = vn[rows, :]
        mixed = jnp.concatenate(
            [jnp.dot(ws[g], vn_c[:, g * GROUP:(g + 1) * GROUP], preferred_element_type=F32)
             for g in range(SGU_GROUPS)], axis=1)
        u = _gelu_tanh(u_ref[rows, :].astype(F32))
        o_ref[rows, :] = (u * (mixed + bexp)).astype(o_ref.dtype)


def _mixer_b(proj, norm_g, w_s, b_s, tm=TILES["sgu"]):
    m = proj.shape[0]
    tm = min(tm, m)
    bexp = jnp.repeat(b_s.T, GROUP, axis=1)
    return pl.pallas_call(
        _sgu_kernel,
        grid=(m // tm,),
        in_specs=[pl.BlockSpec((tm, W_MIX), lambda i: (i, COL_B_U)),
                  pl.BlockSpec((tm, W_MIX), lambda i: (i, COL_B_V)),
                  pl.BlockSpec((1, W_MIX), lambda i: (0, 0)),
                  pl.BlockSpec((SGU_GROUPS, CHUNK, CHUNK), lambda i: (0, 0, 0)),
                  pl.BlockSpec((CHUNK, W_MIX), lambda i: (0, 0))],
        out_specs=pl.BlockSpec((tm, W_MIX), lambda i: (i, 0)),
        out_shape=jax.ShapeDtypeStruct((m, W_MIX), BF16),
        compiler_params=_params(1),
        name="mixer_b_sgu",
    )(proj, proj, norm_g.reshape(1, W_MIX), w_s, bexp)


def _qknorm_kernel(q_ref, k_ref, qg_ref, kg_ref, qo_ref, ko_ref):
    for src, g_ref, dst in ((q_ref, qg_ref, qo_ref), (k_ref, kg_ref, ko_ref)):
        g = g_ref[...]
        for h in range(SB_HEADS):
            cols = slice(h * SB_HEAD_DIM, (h + 1) * SB_HEAD_DIM)
            x = src[:, cols].astype(F32)
            ms = jnp.mean(x * x, axis=-1, keepdims=True)
            dst[:, cols] = (x * lax.rsqrt(ms + EPS) * g).astype(dst.dtype)


EXP2_ZERO_BELOW = -160.0
R_DEAD = -1e30


def _attn_kernel(q_ref, k_ref, v_ref, o_ref, acc_scr, *, tq, sub):
    i = pl.program_id(2)
    n_groups = tq // sub
    qs = [q_ref[g * sub:(g + 1) * sub, :] for g in range(n_groups)]
    jj = lax.broadcasted_iota(jnp.int32, (sub, sub), 0)
    ss = lax.broadcasted_iota(jnp.int32, (sub, sub), 1)
    uneg = jnp.where(jj > ss, -1.0, 0.0).astype(BF16)

    def tile(q, off, r, mask):
        rows = pl.ds(off, sub)
        z2 = lax.dot_general(q, k_ref[rows, :], (((1,), (1,)), ((), ())),
                             preferred_element_type=F32)
        neg_abs = pltpu.bitcast(pltpu.bitcast(z2, jnp.int32) | jnp.int32(-2 ** 31), F32)
        l2 = jnp.log(1.0 + jnp.exp2(neg_abs)) * (1.0 / math.log(2.0))
        sp = jnp.maximum(z2, 0.0) + l2
        lb = jnp.minimum(z2, 0.0) - l2
        if mask is not None:
            sp = jnp.where(mask, sp, 0.0)
        rev = jnp.dot(sp.astype(BF16), uneg, preferred_element_type=F32)
        a = jnp.exp2(lb + rev + r)
        if mask is not None:
            a = jnp.where(mask, a, 0.0)
        pv = jnp.dot(a.astype(BF16), v_ref[rows, :], preferred_element_type=F32)
        return pv, r - jnp.sum(sp, axis=1, keepdims=True)

    base = i * n_groups
    rs = []
    for g in range(n_groups):
        pv, r = tile(qs[g], pl.multiple_of((base + g) * sub, sub), jnp.zeros((sub, 1), F32), ss < jj)
        acc_scr[g * sub:(g + 1) * sub, :] = pv
        rs.append(r)

    n_steps = base + n_groups - 1

    def cond(carry):
        step, _, r_max = carry
        return jnp.logical_and(step < n_steps, r_max > EXP2_ZERO_BELOW)

    def body(carry):
        step, rs, _ = carry
        new_rs = []
        for g in range(n_groups):
            t_idx = base + g - 1 - step
            r_in = jnp.where(t_idx >= 0, rs[g], R_DEAD)
            pv, r_out = tile(qs[g], pl.multiple_of(jnp.maximum(t_idx, 0) * sub, sub), r_in, None)
            acc_scr[g * sub:(g + 1) * sub, :] += pv
            new_rs.append(r_out)
        r_max = functools.reduce(jnp.maximum, [jnp.max(r) for r in new_rs])
        return step + 1, tuple(new_rs), r_max

    r_max = functools.reduce(jnp.maximum, [jnp.max(r) for r in rs])
    lax.while_loop(cond, body, (jnp.int32(0), tuple(rs), r_max))
    o_ref[...] = acc_scr[...].astype(o_ref.dtype)


def _attention(qn, kn, proj, batch, seq, tq=TILES["attn_q"], sub=TILES["attn_keys"]):
    m = qn.shape[0]
    tq = min(tq, seq)
    nq = seq // tq
    v_col0 = COL_C_V * (W_MIX // SB_HEAD_DIM)
    return pl.pallas_call(
        functools.partial(_attn_kernel, tq=tq, sub=sub),
        grid=(batch, SB_HEADS, nq),
        in_specs=[pl.BlockSpec((tq, SB_HEAD_DIM), lambda b, h, i: (b * nq + i, h)),
                  pl.BlockSpec((seq, SB_HEAD_DIM), lambda b, h, i: (b, h)),
                  pl.BlockSpec((seq, SB_HEAD_DIM), lambda b, h, i: (b, v_col0 + h))],
        out_specs=pl.BlockSpec((tq, SB_HEAD_DIM), lambda b, h, i: (b * nq + i, h)),
        out_shape=jax.ShapeDtypeStruct((m, W_MIX), BF16),
        scratch_shapes=[pltpu.VMEM((tq, SB_HEAD_DIM), F32)],
        compiler_params=_params(3),
        name="stick_breaking_attn",
    )(qn, kn, proj)


def _split3(x):
    h1 = x.astype(BF16)
    r1 = x - h1.astype(F32)
    h2 = r1.astype(BF16)
    h3 = (r1 - h2.astype(F32)).astype(BF16)
    return [h1, h2, h3]


def _ssd_kernel(z_ref, xbc_ref, halo_ref, dt_ref, cw_ref, cb_ref, dtb_ref, alog_ref, dexp_ref,
                ng_ref, xmat_ref, o_ref, ht_scr, xp_scr):
    c = pl.program_id(1)
    L = CHUNK
    h = BF16_SUBLANES
    gw = W_MIX // SSM_GROUPS

    @pl.when(c == 0)
    def _():
        ht_scr[...] = jnp.zeros_like(ht_scr)
        xp_scr[...] = jnp.zeros_like(xp_scr)

    xp_scr[L - h:L, :] = jnp.where(c == 0, jnp.zeros_like(halo_ref), halo_ref[...])
    xp_scr[L:2 * L, :] = xbc_ref[...]
    n_shift = SSM_CONV - 1
    out_row = lax.broadcasted_iota(jnp.int32, (n_shift * L, 2 * L), 0)
    src_row = lax.broadcasted_iota(jnp.int32, (n_shift * L, 2 * L), 1)
    delay = out_row // L + 1
    shift_mat = jnp.where(src_row == out_row - (delay - 1) * L + L - delay, 1.0, 0.0).astype(BF16)
    shifted = jnp.dot(shift_mat, xp_scr[...], preferred_element_type=F32)
    cw = cw_ref[...]
    conv = cb_ref[...] + cw[SSM_CONV - 1:SSM_CONV, :] * xbc_ref[...].astype(F32)
    for d in range(1, SSM_CONV):
        k = SSM_CONV - 1 - d
        conv = conv + cw[k:k + 1, :] * shifted[(d - 1) * L:d * L, :]
    xc = conv * _sigmoid(conv)
    xs = xc[:, :W_MIX]
    bmat = xc[:, W_MIX:W_MIX + SSM_GROUPS * SSM_STATE].astype(BF16)
    cmat = xc[:, W_MIX + SSM_GROUPS * SSM_STATE:].astype(BF16)

    dt = _softplus(dt_ref[...] + dtb_ref[...])
    da = dt * (-jnp.exp(alog_ref[...]))
    t_i = lax.broadcasted_iota(jnp.int32, (L, L), 0)
    s_i = lax.broadcasted_iota(jnp.int32, (L, L), 1)
    causal = s_i <= t_i
    tri3 = jnp.concatenate([causal.astype(BF16)] * 3, axis=1)
    cum = jnp.dot(tri3, jnp.concatenate(_split3(da), axis=0), preferred_element_type=F32)
    cum_t = cum.T
    xmat3 = xmat_ref[...]
    dt_e = jnp.dot(jnp.concatenate(_split3(dt), axis=1), xmat3, preferred_element_type=F32)
    cum_e = jnp.dot(jnp.concatenate(_split3(cum), axis=1), xmat3, preferred_element_type=F32)
    cl_e = cum_e[L - 1:L, :]
    xdt = xs * dt_e
    xdec = (xdt * jnp.exp(cl_e - cum_e)).astype(BF16)
    ecum_e = jnp.exp(cum_e)
    chunk_dec = jnp.exp(cl_e)
    lane = lax.broadcasted_iota(jnp.int32, (L, LANES), 1)
    left = lane < SSM_HEAD_DIM

    ys = []
    for g in range(SSM_GROUPS):
        bg = bmat[:, g * SSM_STATE:(g + 1) * SSM_STATE]
        cg = cmat[:, g * SSM_STATE:(g + 1) * SSM_STATE]
        gcols = slice(g * gw, (g + 1) * gw)
        cb = lax.dot_general(cg, bg, (((1,), (1,)), ((), ())), preferred_element_type=F32)
        ht = ht_scr[g]
        y_off = jnp.dot(cg, ht.astype(BF16), preferred_element_type=F32)
        parts = []
        for pr in range(SSM_HPG // 2):
            ms = []
            for r in range(2):
                idx = g * SSM_HPG + 2 * pr + r
                seg = cum[:, idx:idx + 1] - cum_t[idx:idx + 1, :]
                dec = jnp.exp(jnp.where(causal, seg, -jnp.inf))
                ms.append((cb * dec).astype(BF16))
            c0 = g * gw + pr * LANES
            xpair = xdt[:, c0:c0 + LANES]
            rhs = jnp.concatenate(
                [jnp.where(left, xpair, 0.0), jnp.where(left, 0.0, xpair)], axis=0).astype(BF16)
            parts.append(jnp.dot(jnp.concatenate(ms, axis=1), rhs, preferred_element_type=F32))
        y_diag = jnp.concatenate(parts, axis=1)
        ys.append(y_diag + y_off * ecum_e[:, gcols])
        st = lax.dot_general(bg, xdec[:, gcols], (((0,), (0,)), ((), ())),
                             preferred_element_type=F32)
        ht_scr[g] = ht * chunk_dec[:, gcols] + st

    y = jnp.concatenate(ys, axis=1) + xs * dexp_ref[...]
    z = z_ref[...].astype(F32)
    y = y * (z * _sigmoid(z))
    outs = []
    for g in range(SSM_GROUPS):
        yg = y[:, g * gw:(g + 1) * gw]
        ms = jnp.mean(yg * yg, axis=-1, keepdims=True)
        outs.append(yg * lax.rsqrt(ms + EPS))
    o_ref[...] = (jnp.concatenate(outs, axis=1) * ng_ref[...]).astype(o_ref.dtype)


def _mixer_d(proj, dt_raw, conv_w, conv_b, dt_bias, a_log, d_skip, norm_g, batch, seq):
    m = proj.shape[0]
    nc = seq // CHUNK
    h = BF16_SUBLANES
    pad = LANES - SSM_HEADS
    row = lambda v: v.reshape(1, -1)
    xmat = (jnp.arange(W_MIX)[None, :] // SSM_HEAD_DIM == jnp.arange(LANES)[:, None]).astype(BF16)
    xmat = jnp.concatenate([xmat] * 3, axis=0)
    const = lambda shape: pl.BlockSpec(shape, lambda b, c: (0,) * len(shape))
    return pl.pallas_call(
        _ssd_kernel,
        grid=(batch, nc),
        in_specs=[pl.BlockSpec((CHUNK, W_MIX), lambda b, c: (b * nc + c, COL_D_Z)),
                  pl.BlockSpec((CHUNK, SSM_CONV_DIM), lambda b, c: (b * nc + c, COL_D_XBC)),
                  pl.BlockSpec((h, SSM_CONV_DIM),
                               lambda b, c: (jnp.maximum((b * nc + c) * (CHUNK // h) - 1, 0), COL_D_XBC)),
                  pl.BlockSpec((CHUNK, LANES), lambda b, c: (b * nc + c, 0)),
                  const((SSM_CONV, SSM_CONV_DIM)), const((1, SSM_CONV_DIM)),
                  const((1, LANES)), const((1, LANES)), const((1, W_MIX)), const((1, W_MIX)),
                  const((3 * LANES, W_MIX))],
        out_specs=pl.BlockSpec((CHUNK, W_MIX), lambda b, c: (b * nc + c, 0)),
        out_shape=jax.ShapeDtypeStruct((m, W_MIX), BF16),
        scratch_shapes=[pltpu.VMEM((SSM_GROUPS, SSM_STATE, W_MIX // SSM_GROUPS), F32),
                        pltpu.VMEM((2 * CHUNK, SSM_CONV_DIM), BF16)],
        compiler_params=_params(2),
        name="mixer_d_ssd",
    )(proj, proj, proj, dt_raw, conv_w, row(conv_b),
      row(jnp.pad(dt_bias, (0, pad))), row(jnp.pad(a_log, (0, pad))),
      row(jnp.repeat(d_skip, SSM_HEAD_DIM)), row(norm_g), xmat)


def _merge_kernel(ya_ref, yb_ref, yc_ref, yd_ref, wb_ref, g0_ref, g1_ref, g2_ref, g3_ref, wsrc_ref,
                  o_ref, wdst_ref):
    wdst_ref[...] = wsrc_ref[...].astype(wdst_ref.dtype)
    acc = None
    for j, (y_ref, g_ref) in enumerate(
            zip((ya_ref, yb_ref, yc_ref, yd_ref), (g0_ref, g1_ref, g2_ref, g3_ref))):
        t = jnp.dot(y_ref[...], wb_ref[j], preferred_element_type=F32) * g_ref[...].astype(F32)
        acc = t if acc is None else acc + t
    o_ref[...] = acc.astype(o_ref.dtype)


def _merge(ys, w_branch, gates, side_w, *, layer):
    m = gates.shape[0]
    bm, bn = TILES["merge"]
    bm = min(bm, m)
    nb = D_MODEL // bn
    nm = m // bm
    side_in, side_out, side_shape = _side_cast_specs(side_w, layer, nb * nm, lambda n, i: n * nm + i)
    yspec = pl.BlockSpec((bm, W_MIX), lambda n, i: (i, 0))
    gspec = lambda j: pl.BlockSpec((bm, bn), lambda n, i: (i, j * nb + n))
    return pl.pallas_call(
        _merge_kernel,
        grid=(nb, m // bm),
        in_specs=[yspec] * N_BRANCH
        + [pl.BlockSpec((N_BRANCH, W_MIX, bn), lambda n, i: (0, 0, n))]
        + [gspec(j) for j in range(N_BRANCH)] + [side_in],
        out_specs=[pl.BlockSpec((bm, bn), lambda n, i: (i, n)), side_out],
        out_shape=[jax.ShapeDtypeStruct((m, D_MODEL), BF16), side_shape],
        compiler_params=_params(2),
        name="branch_merge",
    )(*ys, w_branch, gates, gates, gates, gates, side_w)


def _ffn_gu_kernel(h_ref, s_ref, wg_ref, wu_ref, wsrc_ref, o_ref, wdst_ref, wg_scr, wu_scr):
    j, i = pl.program_id(0), pl.program_id(1)
    chunk = wg_ref.shape[0]
    slot = j % 2

    def convert_chunk():
        rows = pl.ds(pl.multiple_of(i * chunk, chunk), chunk)
        wg_scr[slot, rows, :] = wg_ref[...].astype(wg_scr.dtype)
        wu_scr[slot, rows, :] = wu_ref[...].astype(wu_scr.dtype)

    @pl.when(j == 0)
    def _():
        convert_chunk()
        o_ref[...] = jnp.zeros_like(o_ref)

    @pl.when(j > 0)
    def _():
        convert_chunk()
        wg, wu = wg_scr[1 - slot], wu_scr[1 - slot]
        rows_per = h_ref.shape[0] // ROW_CHUNKS
        for c in range(ROW_CHUNKS):
            rows = slice(c * rows_per, (c + 1) * rows_per)
            h = h_ref[rows, :]
            sc = s_ref[rows, :]
            g = _row_scale(jnp.dot(h, wg, preferred_element_type=F32), sc)
            u = _row_scale(jnp.dot(h, wu, preferred_element_type=F32), sc)
            o_ref[rows, :] = (g * _sigmoid(g) * u).astype(o_ref.dtype)
        wdst_ref[...] = wsrc_ref[...].astype(wdst_ref.dtype)


def _ffn_gate_up(h, s, wg, wu, side_w, *, layer):
    m, k = h.shape
    f = wg.shape[2]
    bm, bn = TILES["ffn_gate_up"]
    bm = min(bm, m)
    nm = m // bm
    nb = f // bn
    chunk = k // nm
    assert chunk * nm == k and chunk % BF16_SUBLANES == 0
    wspec = pl.BlockSpec((None, chunk, bn), lambda j, i: (layer, i, jnp.minimum(j, nb - 1)))
    side_in, side_out, side_shape = _side_cast_specs(
        side_w, layer, nb * nm, lambda j, i: jnp.where(j == 0, 0, (j - 1) * nm + i))
    return pl.pallas_call(
        _ffn_gu_kernel,
        grid=(nb + 1, nm),
        in_specs=[pl.BlockSpec((bm, k), lambda j, i: (i, 0)),
                  pl.BlockSpec((bm, LANES), lambda j, i: (i, 0)), wspec, wspec, side_in],
        out_specs=[pl.BlockSpec((bm, bn), _WARMUP_OUT_INDEX), side_out],
        out_shape=[jax.ShapeDtypeStruct((m, f), BF16), side_shape],
        scratch_shapes=[pltpu.VMEM((2, k, bn), BF16)] * 2,
        compiler_params=_params(2),
        name="ffn_gate_up",
    )(h, s, wg, wu, side_w)


def kernel(x, norm_mix, w_in, conv_a, sgu_norm, sgu_w, sgu_b, q_norm, k_norm, ssm_conv_w, ssm_conv_b, ssm_dt_bias, ssm_a_log, ssm_d, ssm_norm, w_branch, w_out, norm_ffn, w_ffn_gate, w_ffn_up, w_ffn_down):
    batch, seq, d = x.shape
    m = batch * seq
    depth = w_in.shape[0]
    xr = x.reshape(m, d)
    q_fold = (1.0 / math.sqrt(SB_HEAD_DIM)) * (1.0 / math.log(2.0))
    w_in_t = jnp.swapaxes(w_in, 1, 2)
    w_branch_2d = w_branch.reshape(depth, N_BRANCH * W_MIX, d)
    w_dt = jnp.pad(w_in[:, :, N_MIX:N_MIX + SSM_HEADS],
                   ((0, 0), (0, 0), (0, LANES - SSM_HEADS))).astype(BF16)
    xb, s, dt_raw = _norm_prep(xr, norm_mix[0], w_dt[0])
    for l in range(depth):
        proj = _in_proj_stream(xb, s, w_in_t, layer=l, row0=0, shift=0, n=N_MIX, out_dtype=BF16,
                               name="in_proj_mix")
        gates, w_branch_bf, y_a, qn, kn = _in_proj_stream(
            xb, s, w_in_t, layer=l, row0=N_MIX, shift=GATE_SHIFT, n=N_BRANCH * d, out_dtype=BF16,
            act="sigmoid", name="in_proj_gates",
            side=_mixer_side_jobs(w_branch_2d, l, proj, conv_a[l], q_norm[l] * q_fold, k_norm[l], seq))

        y_b = _mixer_b(proj, sgu_norm[l], sgu_w[l], sgu_b[l])
        y_c = _attention(qn, kn, proj, batch, seq)
        y_d = _mixer_d(proj, dt_raw, ssm_conv_w[l], ssm_conv_b[l], ssm_dt_bias[l], ssm_a_log[l],
                       ssm_d[l], ssm_norm[l], batch, seq)

        merged, w_out_bf = _merge((y_a, y_b, y_c, y_d), w_branch_bf.reshape(N_BRANCH, W_MIX, d), gates,
                                  w_out, layer=l)
        xr, xb, s = _resid_matmul(merged, w_out_bf, xr, norm_ffn[l], name="out_proj")

        act, w_down_bf = _ffn_gate_up(xb, s, w_ffn_gate, w_ffn_up, w_ffn_down, layer=l)
        if l + 1 < depth:
            xr, xb, s = _resid_matmul(act, w_down_bf, xr, norm_mix[l + 1], name="ffn_down")
            dt_raw = _dt_proj(xb, s, w_dt[l + 1])
        else:
            xr = _resid_matmul(act, w_down_bf, xr, None, name="ffn_down")
    return xr.reshape(batch, seq, d)
```

```python
import functools
import math

import jax
import jax.numpy as jnp
from jax import lax
from jax.experimental import pallas as pl
from jax.experimental.pallas import tpu as pltpu

F32 = jnp.float32
BF16 = jnp.bfloat16

EPS = 1e-6
D_MODEL = 4096
W_MIX = D_MODEL // 4
GROUP = 128
CHUNK = 128
CONV_A = 3
SGU_GROUPS = W_MIX // GROUP
SB_HEAD_DIM = 128
SB_HEADS = W_MIX // SB_HEAD_DIM
SSM_HEAD_DIM = 64
SSM_HEADS = W_MIX // SSM_HEAD_DIM
SSM_GROUPS = 2
SSM_HPG = SSM_HEADS // SSM_GROUPS
SSM_STATE = 128
SSM_CONV = 4
SSM_CONV_DIM = W_MIX + 2 * SSM_GROUPS * SSM_STATE
N_BRANCH = 4
N_MIX = 9 * W_MIX + SSM_CONV_DIM
GATE_SHIFT = SSM_HEADS
LANES = 128
BF16_SUBLANES = 16
VMEM_LIMIT = 56 * 1024 * 1024

TILES = dict(
    norm_prep=512, dt_proj=1024, sgu=2048,
    attn_q=4096, attn_keys=256,
    in_proj_mix=(1024, 768), in_proj_gates=(1024, 1024),
    merge=(512, 1024), out_proj=(1024, 512),
    ffn_gate_up=(2048, 256), ffn_down=(512, 512),
)

COL_A_B, COL_A_C, COL_A_X = 0, 1, 2
COL_B_U, COL_B_V = 3, 4
COL_C_Q, COL_C_K, COL_C_V = 5, 6, 7
COL_D_Z = 8
COL_D_XBC = (9 * W_MIX) // SSM_CONV_DIM


def _params(n_axes):
    return pltpu.CompilerParams(
        dimension_semantics=("arbitrary",) * n_axes, vmem_limit_bytes=VMEM_LIMIT)


def _sigmoid(x):
    return 1.0 / (1.0 + jnp.exp(-x))


def _softplus(x):
    return jnp.maximum(x, 0.0) + jnp.log1p(jnp.exp(-jnp.abs(x)))


def _gelu_tanh(x):
    c = math.sqrt(2.0 / math.pi)
    return x * (0.5 * (1.0 + jnp.tanh(c * (x + 0.044715 * (x * x * x)))))


def _norm_prep_kernel(x_ref, g_ref, wdt_ref, xb_ref, s_ref, dt_ref):
    x = x_ref[...]
    ms = jnp.mean(x * x, axis=-1, keepdims=True)
    xb = (x * g_ref[...]).astype(xb_ref.dtype)
    scale = jnp.broadcast_to(lax.rsqrt(ms + EPS), s_ref.shape)
    xb_ref[...] = xb
    s_ref[...] = scale
    dt_ref[...] = jnp.dot(xb, wdt_ref[...], preferred_element_type=F32) * scale


def _norm_prep(x, g, w_dt, tm=TILES["norm_prep"]):
    m, d = x.shape
    return pl.pallas_call(
        _norm_prep_kernel,
        grid=(m // tm,),
        in_specs=[pl.BlockSpec((tm, d), lambda i: (i, 0)),
                  pl.BlockSpec((1, d), lambda i: (0, 0)),
                  pl.BlockSpec((d, LANES), lambda i: (0, 0))],
        out_specs=[pl.BlockSpec((tm, d), lambda i: (i, 0)),
                   pl.BlockSpec((tm, LANES), lambda i: (i, 0)),
                   pl.BlockSpec((tm, LANES), lambda i: (i, 0))],
        out_shape=[jax.ShapeDtypeStruct((m, d), BF16), jax.ShapeDtypeStruct((m, LANES), F32),
                   jax.ShapeDtypeStruct((m, LANES), F32)],
        compiler_params=_params(1),
        name="norm_prep",
    )(x, g.reshape(1, d), w_dt)


def _dt_proj_kernel(xb_ref, s_ref, wdt_ref, o_ref):
    o_ref[...] = jnp.dot(xb_ref[...], wdt_ref[...], preferred_element_type=F32) * s_ref[...]


def _dt_proj(xb, s, w_dt, tm=TILES["dt_proj"]):
    m, d = xb.shape
    tm = min(tm, m)
    return pl.pallas_call(
        _dt_proj_kernel,
        grid=(m // tm,),
        in_specs=[pl.BlockSpec((tm, d), lambda i: (i, 0)),
                  pl.BlockSpec((tm, LANES), lambda i: (i, 0)),
                  pl.BlockSpec((d, LANES), lambda i: (0, 0))],
        out_specs=pl.BlockSpec((tm, LANES), lambda i: (i, 0)),
        out_shape=jax.ShapeDtypeStruct((m, LANES), F32),
        compiler_params=_params(1),
        name="dt_proj",
    )(xb, s, w_dt)


ROW_CHUNKS = 4


def _row_scale(acc, s):
    reps = acc.shape[1] // LANES
    return acc * (s if reps == 1 else jnp.concatenate([s] * reps, axis=1))


def _resid_mm_kernel(*refs, norm, n_steps, n_cols):
    a_ref, w_ref, r_ref = refs[0], refs[1], refs[2]
    acc = jnp.dot(a_ref[...], w_ref[...], preferred_element_type=F32) + r_ref[...]
    if not norm:
        refs[3][...] = acc
        return
    g_ref, o_ref, xb_ref, s_ref = refs[3], refs[4], refs[5], refs[6]
    j = pl.program_id(1)
    o_ref[...] = acc
    xb_ref[...] = (acc * g_ref[...]).astype(xb_ref.dtype)
    sq = acc * acc
    part = sq[:, 0:LANES]
    for c in range(1, acc.shape[1] // LANES):
        part = part + sq[:, c * LANES:(c + 1) * LANES]

    @pl.when(j == 0)
    def _():
        s_ref[...] = part

    @pl.when(j != 0)
    def _():
        s_ref[...] += part

    @pl.when(j == n_steps - 1)
    def _():
        tot = jnp.sum(s_ref[...], axis=1, keepdims=True)
        s_ref[...] = jnp.broadcast_to(lax.rsqrt(tot * (1.0 / n_cols) + EPS), s_ref.shape)


def _resid_matmul(a, w, res, g_next, *, name):
    m, k = a.shape
    n = w.shape[1]
    bm, bn = TILES[name]
    bm = min(bm, m)
    norm = g_next is not None
    ij = lambda i, j: (i, j)
    in_specs = [pl.BlockSpec((bm, k), lambda i, j: (i, 0)),
                pl.BlockSpec((k, bn), lambda i, j: (0, j)),
                pl.BlockSpec((bm, bn), ij)]
    args = [a, w, res]
    out_specs = [pl.BlockSpec((bm, bn), ij)]
    out_shape = [jax.ShapeDtypeStruct((m, n), F32)]
    if norm:
        in_specs.append(pl.BlockSpec((1, bn), lambda i, j: (0, j)))
        args.append(g_next.reshape(1, n))
        out_specs += [pl.BlockSpec((bm, bn), ij), pl.BlockSpec((bm, LANES), lambda i, j: (i, 0))]
        out_shape += [jax.ShapeDtypeStruct((m, n), BF16), jax.ShapeDtypeStruct((m, LANES), F32)]
    out = pl.pallas_call(
        functools.partial(_resid_mm_kernel, norm=norm, n_steps=n // bn, n_cols=n),
        grid=(m // bm, n // bn),
        in_specs=in_specs,
        out_specs=out_specs,
        out_shape=out_shape,
        compiler_params=_params(2),
        name=name,
    )(*args)
    return out if norm else out[0]


def _side_cast_specs(w, layer, n_steps, step_of):
    rows, cols = w.shape[1], w.shape[2]
    chunk = rows // n_steps
    assert chunk * n_steps == rows and chunk % BF16_SUBLANES == 0
    return (pl.BlockSpec((None, chunk, cols), lambda *g: (layer, step_of(*g), 0)),
            pl.BlockSpec((chunk, cols), lambda *g: (step_of(*g), 0)),
            jax.ShapeDtypeStruct((rows, cols), BF16))


def _WARMUP_OUT_INDEX(j, i):
    return jnp.where(j == 0, 0, i), jnp.maximum(j - 1, 0)


def _in_proj_stream_kernel(*refs, shift, act, n_side_in, n_side_out, side_fn, n_steps_i):
    n_in = 3 + bool(shift)
    a_ref, s_ref, wc_ref = refs[0], refs[1], refs[2]
    side_in = refs[n_in:n_in + n_side_in]
    o_ref = refs[n_in + n_side_in]
    side_out = refs[n_in + n_side_in + 1:n_in + n_side_in + 1 + n_side_out]
    w_scr = refs[n_in + n_side_in + 1 + n_side_out]
    side_scr = refs[n_in + n_side_in + 2 + n_side_out:]
    j, i = pl.program_id(0), pl.program_id(1)
    chunk = wc_ref.shape[0]
    slot = j % 2

    def convert_chunk():
        src = wc_ref[...]
        if shift:
            src = jnp.concatenate([src[shift:, :], refs[3][...]], axis=0)
        w_scr[slot, pl.ds(pl.multiple_of(i * chunk, chunk), chunk), :] = src.astype(w_scr.dtype)

    @pl.when(j == 0)
    def _():
        convert_chunk()
        o_ref[...] = jnp.zeros_like(o_ref)

    @pl.when(j > 0)
    def _():
        convert_chunk()
        rows_per = a_ref.shape[0] // ROW_CHUNKS
        w = w_scr[1 - slot]
        for c in range(ROW_CHUNKS):
            rows = slice(c * rows_per, (c + 1) * rows_per)
            acc = lax.dot_general(a_ref[rows, :], w, (((1,), (1,)), ((), ())),
                                  preferred_element_type=F32)
            acc = _row_scale(acc, s_ref[rows, :])
            if act == "sigmoid":
                acc = _sigmoid(acc)
            o_ref[rows, :] = acc.astype(o_ref.dtype)
        if side_fn is not None:
            side_fn((j - 1) * n_steps_i + i, side_in, side_out, side_scr)


def _in_proj_stream(a, s, w_t, *, layer, row0, shift, n, out_dtype, act=None, side=None, name):
    m, k = a.shape
    bm, bn = TILES[name]
    bm = min(bm, m)
    nm = m // bm
    nb = n // bn
    chunk = bn // nm
    assert chunk * nm == bn and chunk % BF16_SUBLANES == 0 and row0 % chunk == 0
    wblk = lambda j: jnp.minimum(j, nb - 1)
    in_specs = [pl.BlockSpec((bm, k), lambda j, i: (i, 0)),
                pl.BlockSpec((bm, LANES), lambda j, i: (i, 0)),
                pl.BlockSpec((None, chunk, k), lambda j, i: (layer, row0 // chunk + wblk(j) * nm + i, 0))]
    args = [a, s, w_t]
    if shift:
        in_specs.append(pl.BlockSpec(
            (None, shift, k),
            lambda j, i: (layer, (row0 + wblk(j) * bn + (i + 1) * chunk) // shift, 0)))
        args.append(w_t)
    out_specs = [pl.BlockSpec((bm, bn), _WARMUP_OUT_INDEX)]
    out_shape = [jax.ShapeDtypeStruct((m, n), out_dtype)]
    scratch = [pltpu.VMEM((2, bn, k), BF16)]
    side_fn, n_side_in, n_side_out = None, 0, 0
    if side is not None:
        side_fn, s_in, s_args, s_out, s_shape, s_scr = side(
            nb * nm, lambda j, i: jnp.where(j == 0, 0, (j - 1) * nm + i))
        in_specs += s_in
        args += s_args
        out_specs += s_out
        out_shape += s_shape
        scratch += s_scr
        n_side_in, n_side_out = len(s_in), len(s_out)
    out = pl.pallas_call(
        functools.partial(_in_proj_stream_kernel, shift=shift, act=act, n_side_in=n_side_in,
                          n_side_out=n_side_out, side_fn=side_fn, n_steps_i=nm),
        grid=(nb + 1, nm),
        in_specs=in_specs,
        out_specs=out_specs,
        out_shape=out_shape,
        scratch_shapes=scratch,
        compiler_params=_params(2),
        name=name,
    )(*args)
    return out if side is not None else out[0]


def _mixer_side_jobs(w_branch_2d, layer, proj, conv_w, q_g, k_g, seq):
    def build(n_steps, step_of):
        m = proj.shape[0]
        rows = m // n_steps
        h = BF16_SUBLANES
        assert rows * n_steps == m and rows % h == 0 and seq % rows == 0
        cast_in, cast_out, cast_shape = _side_cast_specs(w_branch_2d, layer, n_steps, step_of)
        main = lambda col: pl.BlockSpec((rows, W_MIX), lambda *g: (step_of(*g), col))
        halo = lambda col: pl.BlockSpec(
            (h, W_MIX), lambda *g: (jnp.maximum(step_of(*g) * (rows // h) - 1, 0), col))
        const = lambda shape: pl.BlockSpec(shape, lambda *g: (0,) * len(shape))
        in_specs = [cast_in, main(COL_A_B), main(COL_A_C), main(COL_A_X), halo(COL_A_C), halo(COL_A_X),
                    const((CONV_A, W_MIX)), main(COL_C_Q), main(COL_C_K),
                    const((1, SB_HEAD_DIM)), const((1, SB_HEAD_DIM))]
        args = [w_branch_2d, proj, proj, proj, proj, proj, conv_w, proj, proj,
                q_g.reshape(1, -1), k_g.reshape(1, -1)]
        out_row = pl.BlockSpec((rows, W_MIX), lambda *g: (step_of(*g), 0))
        out_specs = [cast_out, out_row, out_row, out_row]
        out_shapes = [cast_shape] + [jax.ShapeDtypeStruct((m, W_MIX), BF16)] * 3
        scratch = [pltpu.VMEM((rows + h, W_MIX), F32)]

        def fn(t, ins, outs, scr):
            outs[0][...] = ins[0][...].astype(BF16)
            _conv_a_body(t % (seq // rows) == 0, *ins[1:7], outs[1], scr[0])
            _qknorm_kernel(ins[7], ins[8], ins[9], ins[10], outs[2], outs[3])

        return fn, in_specs, args, out_specs, out_shapes, scratch
    return build


def _conv_a_body(seq_start, b_ref, c_ref, x_ref, ch_ref, xh_ref, w_ref, o_ref, p_scr):
    tm = b_ref.shape[0]
    h = BF16_SUBLANES
    p = c_ref[...].astype(F32) * x_ref[...].astype(F32)
    ph = ch_ref[...].astype(F32) * xh_ref[...].astype(F32)
    ph = jnp.where(seq_start, 0.0, ph)
    p_scr[0:h, :] = ph
    p_scr[h:h + tm, :] = p
    w = w_ref[...]
    y = w[2:3, :] * p + w[1:2, :] * p_scr[h - 1:h - 1 + tm, :] + w[0:1, :] * p_scr[h - 2:h - 2 + tm, :]
    o_ref[...] = (b_ref[...].astype(F32) * y).astype(o_ref.dtype)


def _sgu_kernel(u_ref, v_ref, ng_ref, w_ref, bexp_ref, o_ref):
    tm = u_ref.shape[0]
    vf = _gelu_tanh(v_ref[...].astype(F32))
    mu = jnp.mean(vf, axis=-1, keepdims=True)
    d = vf - mu
    var = jnp.mean(d * d, axis=-1, keepdims=True)
    vn = (d * lax.rsqrt(var + EPS) * ng_ref[...]).astype(BF16)
    t_idx = lax.broadcasted_iota(jnp.int32, (CHUNK, CHUNK), 0)
    s_idx = lax.broadcasted_iota(jnp.int32, (CHUNK, CHUNK), 1)
    causal = s_idx <= t_idx
    ws = [jnp.where(causal, w_ref[g], 0.0).astype(BF16) for g in range(SGU_GROUPS)]
    bexp = bexp_ref[...]
    for c in range(tm // CHUNK):
        rows = slice(c * CHUNK, (c + 1) * CHUNK)
        vn_c = vn[rows, :]
        mixed = jnp.concatenate(
            [jnp.dot(ws[g], vn_c[:, g * GROUP:(g + 1) * GROUP], preferred_element_type=F32)
             for g in range(SGU_GROUPS)], axis=1)
        u = _gelu_tanh(u_ref[rows, :].astype(F32))
        o_ref[rows, :] = (u * (mixed + bexp)).astype(o_ref.dtype)


def _mixer_b(proj, norm_g, w_s, b_s, tm=TILES["sgu"]):
    m = proj.shape[0]
    tm = min(tm, m)
    bexp = jnp.repeat(b_s.T, GROUP, axis=1)
    return pl.pallas_call(
        _sgu_kernel,
        grid=(m // tm,),
        in_specs=[pl.BlockSpec((tm, W_MIX), lambda i: (i, COL_B_U)),
                  pl.BlockSpec((tm, W_MIX), lambda i: (i, COL_B_V)),
                  pl.BlockSpec((1, W_MIX), lambda i: (0, 0)),
                  pl.BlockSpec((SGU_GROUPS, CHUNK, CHUNK), lambda i: (0, 0, 0)),
                  pl.BlockSpec((CHUNK, W_MIX), lambda i: (0, 0))],
        out_specs=pl.BlockSpec((tm, W_MIX), lambda i: (i, 0)),
        out_shape=jax.ShapeDtypeStruct((m, W_MIX), BF16),
        compiler_params=_params(1),
        name="mixer_b_sgu",
    )(proj, proj, norm_g.reshape(1, W_MIX), w_s, bexp)


def _qknorm_kernel(q_ref, k_ref, qg_ref, kg_ref, qo_ref, ko_ref):
    for src, g_ref, dst in ((q_ref, qg_ref, qo_ref), (k_ref, kg_ref, ko_ref)):
        g = g_ref[...]
        for h in range(SB_HEADS):
            cols = slice(h * SB_HEAD_DIM, (h + 1) * SB_HEAD_DIM)
            x = src[:, cols].astype(F32)
            ms = jnp.mean(x * x, axis=-1, keepdims=True)
            dst[:, cols] = (x * lax.rsqrt(ms + EPS) * g).astype(dst.dtype)


EXP2_ZERO_BELOW = -160.0
R_DEAD = -1e30


def _attn_kernel(q_ref, k_ref, v_ref, o_ref, acc_scr, *, tq, sub):
    i = pl.program_id(2)
    n_groups = tq // sub
    qs = [q_ref[g * sub:(g + 1) * sub, :] for g in range(n_groups)]
    jj = lax.broadcasted_iota(jnp.int32, (sub, sub), 0)
    ss = lax.broadcasted_iota(jnp.int32, (sub, sub), 1)
    uneg = jnp.where(jj > ss, -1.0, 0.0).astype(BF16)

    def tile(q, off, r, mask):
        rows = pl.ds(off, sub)
        z2 = lax.dot_general(q, k_ref[rows, :], (((1,), (1,)), ((), ())),
                             preferred_element_type=F32)
        neg_abs = pltpu.bitcast(pltpu.bitcast(z2, jnp.int32) | jnp.int32(-2 ** 31), F32)
        l2 = jnp.log(1.0 + jnp.exp2(neg_abs)) * (1.0 / math.log(2.0))
        sp = jnp.maximum(z2, 0.0) + l2
        lb = jnp.minimum(z2, 0.0) - l2
        if mask is not None:
            sp = jnp.where(mask, sp, 0.0)
        rev = jnp.dot(sp.astype(BF16), uneg, preferred_element_type=F32)
        a = jnp.exp2(lb + rev + r)
        if mask is not None:
            a = jnp.where(mask, a, 0.0)
        pv = jnp.dot(a.astype(BF16), v_ref[rows, :], preferred_element_type=F32)
        return pv, r - jnp.sum(sp, axis=1, keepdims=True)

    base = i * n_groups
    rs = []
    for g in range(n_groups):
        pv, r = tile(qs[g], pl.multiple_of((base + g) * sub, sub), jnp.zeros((sub, 1), F32), ss < jj)
        acc_scr[g * sub:(g + 1) * sub, :] = pv
        rs.append(r)

    n_steps = base + n_groups - 1

    def cond(carry):
        step, _, r_max = carry
        return jnp.logical_and(step < n_steps, r_max > EXP2_ZERO_BELOW)

    def body(carry):
        step, rs, _ = carry
        new_rs = []
        for g in range(n_groups):
            t_idx = base + g - 1 - step
            r_in = jnp.where(t_idx >= 0, rs[g], R_DEAD)
            pv, r_out = tile(qs[g], pl.multiple_of(jnp.maximum(t_idx, 0) * sub, sub), r_in, None)
            acc_scr[g * sub:(g + 1) * sub, :] += pv
            new_rs.append(r_out)
        r_max = functools.reduce(jnp.maximum, [jnp.max(r) for r in new_rs])
        return step + 1, tuple(new_rs), r_max

    r_max = functools.reduce(jnp.maximum, [jnp.max(r) for r in rs])
    lax.while_loop(cond, body, (jnp.int32(0), tuple(rs), r_max))
    o_ref[...] = acc_scr[...].astype(o_ref.dtype)


def _attention(qn, kn, proj, batch, seq, tq=TILES["attn_q"], sub=TILES["attn_keys"]):
    m = qn.shape[0]
    tq = min(tq, seq)
    nq = seq // tq
    v_col0 = COL_C_V * (W_MIX // SB_HEAD_DIM)
    return pl.pallas_call(
        functools.partial(_attn_kernel, tq=tq, sub=sub),
        grid=(batch, SB_HEADS, nq),
        in_specs=[pl.BlockSpec((tq, SB_HEAD_DIM), lambda b, h, i: (b * nq + i, h)),
                  pl.BlockSpec((seq, SB_HEAD_DIM), lambda b, h, i: (b, h)),
                  pl.BlockSpec((seq, SB_HEAD_DIM), lambda b, h, i: (b, v_col0 + h))],
        out_specs=pl.BlockSpec((tq, SB_HEAD_DIM), lambda b, h, i: (b * nq + i, h)),
        out_shape=jax.ShapeDtypeStruct((m, W_MIX), BF16),
        scratch_shapes=[pltpu.VMEM((tq, SB_HEAD_DIM), F32)],
        compiler_params=_params(3),
        name="stick_breaking_attn",
    )(qn, kn, proj)


def _split3(x):
    h1 = x.astype(BF16)
    r1 = x - h1.astype(F32)
    h2 = r1.astype(BF16)
    h3 = (r1 - h2.astype(F32)).astype(BF16)
    return [h1, h2, h3]


def _ssd_kernel(z_ref, xbc_ref, halo_ref, dt_ref, cw_ref, cb_ref, dtb_ref, alog_ref, dexp_ref,
                ng_ref, xmat_ref, o_ref, ht_scr, xp_scr):
    c = pl.program_id(1)
    L = CHUNK
    h = BF16_SUBLANES
    gw = W_MIX // SSM_GROUPS

    @pl.when(c == 0)
    def _():
        ht_scr[...] = jnp.zeros_like(ht_scr)
        xp_scr[...] = jnp.zeros_like(xp_scr)

    xp_scr[L - h:L, :] = jnp.where(c == 0, jnp.zeros_like(halo_ref), halo_ref[...])
    xp_scr[L:2 * L, :] = xbc_ref[...]
    n_shift = SSM_CONV - 1
    out_row = lax.broadcasted_iota(jnp.int32, (n_shift * L, 2 * L), 0)
    src_row = lax.broadcasted_iota(jnp.int32, (n_shift * L, 2 * L), 1)
    delay = out_row // L + 1
    shift_mat = jnp.where(src_row == out_row - (delay - 1) * L + L - delay, 1.0, 0.0).astype(BF16)
    shifted = jnp.dot(shift_mat, xp_scr[...], preferred_element_type=F32)
    cw = cw_ref[...]
    conv = cb_ref[...] + cw[SSM_CONV - 1:SSM_CONV, :] * xbc_ref[...].astype(F32)
    for d in range(1, SSM_CONV):
        k = SSM_CONV - 1 - d
        conv = conv + cw[k:k + 1, :] * shifted[(d - 1) * L:d * L, :]
    xc = conv * _sigmoid(conv)
    xs = xc[:, :W_MIX]
    bmat = xc[:, W_MIX:W_MIX + SSM_GROUPS * SSM_STATE].astype(BF16)
    cmat = xc[:, W_MIX + SSM_GROUPS * SSM_STATE:].astype(BF16)

    dt = _softplus(dt_ref[...] + dtb_ref[...])
    da = dt * (-jnp.exp(alog_ref[...]))
    t_i = lax.broadcasted_iota(jnp.int32, (L, L), 0)
    s_i = lax.broadcasted_iota(jnp.int32, (L, L), 1)
    causal = s_i <= t_i
    tri3 = jnp.concatenate([causal.astype(BF16)] * 3, axis=1)
    cum = jnp.dot(tri3, jnp.concatenate(_split3(da), axis=0), preferred_element_type=F32)
    cum_t = cum.T
    xmat3 = xmat_ref[...]
    dt_e = jnp.dot(jnp.concatenate(_split3(dt), axis=1), xmat3, preferred_element_type=F32)
    cum_e = jnp.dot(jnp.concatenate(_split3(cum), axis=1), xmat3, preferred_element_type=F32)
    cl_e = cum_e[L - 1:L, :]
    xdt = xs * dt_e
    xdec = (xdt * jnp.exp(cl_e - cum_e)).astype(BF16)
    ecum_e = jnp.exp(cum_e)
    chunk_dec = jnp.exp(cl_e)
    lane = lax.broadcasted_iota(jnp.int32, (L, LANES), 1)
    left = lane < SSM_HEAD_DIM

    ys = []
    for g in range(SSM_GROUPS):
        bg = bmat[:, g * SSM_STATE:(g + 1) * SSM_STATE]
        cg = cmat[:, g * SSM_STATE:(g + 1) * SSM_STATE]
        gcols = slice(g * gw, (g + 1) * gw)
        cb = lax.dot_general(cg, bg, (((1,), (1,)), ((), ())), preferred_element_type=F32)
        ht = ht_scr[g]
        y_off = jnp.dot(cg, ht.astype(BF16), preferred_element_type=F32)
        parts = []
        for pr in range(SSM_HPG // 2):
            ms = []
            for r in range(2):
                idx = g * SSM_HPG + 2 * pr + r
                seg = cum[:, idx:idx + 1] - cum_t[idx:idx + 1, :]
                dec = jnp.exp(jnp.where(causal, seg, -jnp.inf))
                ms.append((cb * dec).astype(BF16))
            c0 = g * gw + pr * LANES
            xpair = xdt[:, c0:c0 + LANES]
            rhs = jnp.concatenate(
                [jnp.where(left, xpair, 0.0), jnp.where(left, 0.0, xpair)], axis=0).astype(BF16)
            parts.append(jnp.dot(jnp.concatenate(ms, axis=1), rhs, preferred_element_type=F32))
        y_diag = jnp.concatenate(parts, axis=1)
        ys.append(y_diag + y_off * ecum_e[:, gcols])
        st = lax.dot_general(bg, xdec[:, gcols], (((0,), (0,)), ((), ())),
                             preferred_element_type=F32)
        ht_scr[g] = ht * chunk_dec[:, gcols] + st

    y = jnp.concatenate(ys, axis=1) + xs * dexp_ref[...]
    z = z_ref[...].astype(F32)
    y = y * (z * _sigmoid(z))
    outs = []
    for g in range(SSM_GROUPS):
        yg = y[:, g * gw:(g + 1) * gw]
        ms = jnp.mean(yg * yg, axis=-1, keepdims=True)
        outs.append(yg * lax.rsqrt(ms + EPS))
    o_ref[...] = (jnp.concatenate(outs, axis=1) * ng_ref[...]).astype(o_ref.dtype)


def _mixer_d(proj, dt_raw, conv_w, conv_b, dt_bias, a_log, d_skip, norm_g, batch, seq):
    m = proj.shape[0]
    nc = seq // CHUNK
    h = BF16_SUBLANES
    pad = LANES - SSM_HEADS
    row = lambda v: v.reshape(1, -1)
    xmat = (jnp.arange(W_MIX)[None, :] // SSM_HEAD_DIM == jnp.arange(LANES)[:, None]).astype(BF16)
    xmat = jnp.concatenate([xmat] * 3, axis=0)
    const = lambda shape: pl.BlockSpec(shape, lambda b, c: (0,) * len(shape))
    return pl.pallas_call(
        _ssd_kernel,
        grid=(batch, nc),
        in_specs=[pl.BlockSpec((CHUNK, W_MIX), lambda b, c: (b * nc + c, COL_D_Z)),
                  pl.BlockSpec((CHUNK, SSM_CONV_DIM), lambda b, c: (b * nc + c, COL_D_XBC)),
                  pl.BlockSpec((h, SSM_CONV_DIM),
                               lambda b, c: (jnp.maximum((b * nc + c) * (CHUNK // h) - 1, 0), COL_D_XBC)),
                  pl.BlockSpec((CHUNK, LANES), lambda b, c: (b * nc + c, 0)),
                  const((SSM_CONV, SSM_CONV_DIM)), const((1, SSM_CONV_DIM)),
                  const((1, LANES)), const((1, LANES)), const((1, W_MIX)), const((1, W_MIX)),
                  const((3 * LANES, W_MIX))],
        out_specs=pl.BlockSpec((CHUNK, W_MIX), lambda b, c: (b * nc + c, 0)),
        out_shape=jax.ShapeDtypeStruct((m, W_MIX), BF16),
        scratch_shapes=[pltpu.VMEM((SSM_GROUPS, SSM_STATE, W_MIX // SSM_GROUPS), F32),
                        pltpu.VMEM((2 * CHUNK, SSM_CONV_DIM), BF16)],
        compiler_params=_params(2),
        name="mixer_d_ssd",
    )(proj, proj, proj, dt_raw, conv_w, row(conv_b),
      row(jnp.pad(dt_bias, (0, pad))), row(jnp.pad(a_log, (0, pad))),
      row(jnp.repeat(d_skip, SSM_HEAD_DIM)), row(norm_g), xmat)


def _merge_kernel(ya_ref, yb_ref, yc_ref, yd_ref, wb_ref, g0_ref, g1_ref, g2_ref, g3_ref, wsrc_ref,
                  o_ref, wdst_ref):
    wdst_ref[...] = wsrc_ref[...].astype(wdst_ref.dtype)
    acc = None
    for j, (y_ref, g_ref) in enumerate(
            zip((ya_ref, yb_ref, yc_ref, yd_ref), (g0_ref, g1_ref, g2_ref, g3_ref))):
        t = jnp.dot(y_ref[...], wb_ref[j], preferred_element_type=F32) * g_ref[...].astype(F32)
        acc = t if acc is None else acc + t
    o_ref[...] = acc.astype(o_ref.dtype)


def _merge(ys, w_branch, gates, side_w, *, layer):
    m = gates.shape[0]
    bm, bn = TILES["merge"]
    bm = min(bm, m)
    nb = D_MODEL // bn
    nm = m // bm
    side_in, side_out, side_shape = _side_cast_specs(side_w, layer, nb * nm, lambda n, i: n * nm + i)
    yspec = pl.BlockSpec((bm, W_MIX), lambda n, i: (i, 0))
    gspec = lambda j: pl.BlockSpec((bm, bn), lambda n, i: (i, j * nb + n))
    return pl.pallas_call(
        _merge_kernel,
        grid=(nb, m // bm),
        in_specs=[yspec] * N_BRANCH
        + [pl.BlockSpec((N_BRANCH, W_MIX, bn), lambda n, i: (0, 0, n))]
        + [gspec(j) for j in range(N_BRANCH)] + [side_in],
        out_specs=[pl.BlockSpec((bm, bn), lambda n, i: (i, n)), side_out],
        out_shape=[jax.ShapeDtypeStruct((m, D_MODEL), BF16), side_shape],
        compiler_params=_params(2),
        name="branch_merge",
    )(*ys, w_branch, gates, gates, gates, gates, side_w)


def _ffn_gu_kernel(h_ref, s_ref, wg_ref, wu_ref, wsrc_ref, o_ref, wdst_ref, wg_scr, wu_scr):
    j, i = pl.program_id(0), pl.program_id(1)
    chunk = wg_ref.shape[0]
    slot = j % 2

    def convert_chunk():
        rows = pl.ds(pl.multiple_of(i * chunk, chunk), chunk)
        wg_scr[slot, rows, :] = wg_ref[...].astype(wg_scr.dtype)
        wu_scr[slot, rows, :] = wu_ref[...].astype(wu_scr.dtype)

    @pl.when(j == 0)
    def _():
        convert_chunk()
        o_ref[...] = jnp.zeros_like(o_ref)

    @pl.when(j > 0)
    def _():
        convert_chunk()
        wg, wu = wg_scr[1 - slot], wu_scr[1 - slot]
        rows_per = h_ref.shape[0] // ROW_CHUNKS
        for c in range(ROW_CHUNKS):
            rows = slice(c * rows_per, (c + 1) * rows_per)
            h = h_ref[rows, :]
            sc = s_ref[rows, :]
            g = _row_scale(jnp.dot(h, wg, preferred_element_type=F32), sc)
            u = _row_scale(jnp.dot(h, wu, preferred_element_type=F32), sc)
            o_ref[rows, :] = (g * _sigmoid(g) * u).astype(o_ref.dtype)
        wdst_ref[...] = wsrc_ref[...].astype(wdst_ref.dtype)


def _ffn_gate_up(h, s, wg, wu, side_w, *, layer):
    m, k = h.shape
    f = wg.shape[2]
    bm, bn = TILES["ffn_gate_up"]
    bm = min(bm, m)
    nm = m // bm
    nb = f // bn
    chunk = k // nm
    assert chunk * nm == k and chunk % BF16_SUBLANES == 0
    wspec = pl.BlockSpec((None, chunk, bn), lambda j, i: (layer, i, jnp.minimum(j, nb - 1)))
    side_in, side_out, side_shape = _side_cast_specs(
        side_w, layer, nb * nm, lambda j, i: jnp.where(j == 0, 0, (j - 1) * nm + i))
    return pl.pallas_call(
        _ffn_gu_kernel,
        grid=(nb + 1, nm),
        in_specs=[pl.BlockSpec((bm, k), lambda j, i: (i, 0)),
                  pl.BlockSpec((bm, LANES), lambda j, i: (i, 0)), wspec, wspec, side_in],
        out_specs=[pl.BlockSpec((bm, bn), _WARMUP_OUT_INDEX), side_out],
        out_shape=[jax.ShapeDtypeStruct((m, f), BF16), side_shape],
        scratch_shapes=[pltpu.VMEM((2, k, bn), BF16)] * 2,
        compiler_params=_params(2),
        name="ffn_gate_up",
    )(h, s, wg, wu, side_w)


def kernel(x, norm_mix, w_in, conv_a, sgu_norm, sgu_w, sgu_b, q_norm, k_norm, ssm_conv_w, ssm_conv_b, ssm_dt_bias, ssm_a_log, ssm_d, ssm_norm, w_branch, w_out, norm_ffn, w_ffn_gate, w_ffn_up, w_ffn_down):
    batch, seq, d = x.shape
    m = batch * seq
    depth = w_in.shape[0]
    xr = x.reshape(m, d)
    q_fold = (1.0 / math.sqrt(SB_HEAD_DIM)) * (1.0 / math.log(2.0))
    w_in_t = jnp.swapaxes(w_in, 1, 2)
    w_branch_2d = w_branch.reshape(depth, N_BRANCH * W_MIX, d)
    w_dt = jnp.pad(w_in[:, :, N_MIX:N_MIX + SSM_HEADS],
                   ((0, 0), (0, 0), (0, LANES - SSM_HEADS))).astype(BF16)
    xb, s, dt_raw = _norm_prep(xr, norm_mix[0], w_dt[0])
    for l in range(depth):
        proj = _in_proj_stream(xb, s, w_in_t, layer=l, row0=0, shift=0, n=N_MIX, out_dtype=BF16,
                               name="in_proj_mix")
        gates, w_branch_bf, y_a, qn, kn = _in_proj_stream(
            xb, s, w_in_t, layer=l, row0=N_MIX, shift=GATE_SHIFT, n=N_BRANCH * d, out_dtype=BF16,
            act="sigmoid", name="in_proj_gates",
            side=_mixer_side_jobs(w_branch_2d, l, proj, conv_a[l], q_norm[l] * q_fold, k_norm[l], seq))

        y_b = _mixer_b(proj, sgu_norm[l], sgu_w[l], sgu_b[l])
        y_c = _attention(qn, kn, proj, batch, seq)
        y_d = _mixer_d(proj, dt_raw, ssm_conv_w[l], ssm_conv_b[l], ssm_dt_bias[l], ssm_a_log[l],
                       ssm_d[l], ssm_norm[l], batch, seq)

        merged, w_out_bf = _merge((y_a, y_b, y_c, y_d), w_branch_bf.reshape(N_BRANCH, W_MIX, d), gates,
                                  w_out, layer=l)
        xr, xb, s = _resid_matmul(merged, w_out_bf, xr, norm_ffn[l], name="out_proj")

        act, w_down_bf = _ffn_gate_up(xb, s, w_ffn_gate, w_ffn_up, w_ffn_down, layer=l)
        if l + 1 < depth:
            xr, xb, s = _resid_matmul(act, w_down_bf, xr, norm_mix[l + 1], name="ffn_down")
            dt_raw = _dt_proj(xb, s, w_dt[l + 1])
        else:
            xr = _resid_matmul(act, w_down_bf, xr, None, name="ffn_down")
    return xr.reshape(batch, seq, d)
```
